```python
import math
import jax, jax.numpy as jnp
from jax import lax
import numpy as np

D_MODEL = 2048
BATCH = 8
SEQ = 2048
DEPTH = 2

CHUNK = 64
Q_BLOCK = 128
N_MIXERS = 2
N_POOL_LAYERS = (DEPTH + 1) // 2
N_DIFF_LAYERS = DEPTH // 2
POOL_WINDOWS = (2, 4, 8, 16)
POOL_GROUPS = len(POOL_WINDOWS)
POOL_CG = D_MODEL // POOL_GROUPS
N_HEADS = D_MODEL // 256
HEAD_DIM = D_MODEL // N_HEADS // 2
N_GROUPS = 4
EXPERTS_PER_GROUP = 8
N_EXPERTS = N_GROUPS * EXPERTS_PER_GROUP
TOP_K_IN_GROUP = 2
EXPERT_FF = D_MODEL // 4
NORM_EPS = 1e-6
SUBLN_EPS = 1e-5
NEG_INF = -1e30

kernel_name = "chunk_causal_pool_diffattn_hmoe_trunk"


def rmsnorm(x, g, eps=NORM_EPS):
    x32 = x.astype(jnp.float32)
    y = x32 * lax.rsqrt(jnp.mean(x32 * x32, axis=-1, keepdims=True) + eps)
    return y.astype(x.dtype) * g


def pool_mixer(h, w_pool, scale):
    B, S, D = h.shape
    h32 = h.astype(jnp.float32)
    cs = jnp.concatenate([jnp.zeros((B, 1, D), jnp.float32), jnp.cumsum(h32, axis=1)], axis=1)
    cs = cs.reshape(B, S + 1, POOL_GROUPS, POOL_CG)
    t = jnp.arange(S)
    win = jnp.array(POOL_WINDOWS, dtype=jnp.int32)
    lo = jnp.maximum(t[:, None] + 1 - win[None, :], 0)
    cnt = (t[:, None] + 1 - lo).astype(jnp.float32)
    cs_lo = cs[:, lo, jnp.arange(POOL_GROUPS)[None, :], :]
    mean = (cs[:, 1:] - cs_lo) / cnt[None, :, :, None]
    mix = (mean - h32.reshape(B, S, POOL_GROUPS, POOL_CG)).astype(h.dtype)
    out = jnp.einsum('bsgc,gcd->bsgd', mix, w_pool).reshape(B, S, D)
    return out * scale


def diff_attention(h, w_q, w_k, w_v, w_o, lq1, lk1, lq2, lk2, sub_g, lambda_init):
    B, S, D = h.shape
    q = (h @ w_q).reshape(B, S, N_HEADS, 2, HEAD_DIM)
    k = (h @ w_k).reshape(B, S, N_HEADS, 2, HEAD_DIM)
    v = (h @ w_v).reshape(B, S, N_HEADS, 2 * HEAD_DIM)
    lam = (jnp.exp(jnp.sum(lq1.astype(jnp.float32) * lk1.astype(jnp.float32)))
           - jnp.exp(jnp.sum(lq2.astype(jnp.float32) * lk2.astype(jnp.float32)))
           + lambda_init)
    scale = HEAD_DIM ** -0.5
    n_blk = S // Q_BLOCK
    k_chunk = jnp.arange(S) // CHUNK
    q_blocks = q.reshape(B, n_blk, Q_BLOCK, N_HEADS, 2, HEAD_DIM).transpose(1, 0, 2, 3, 4, 5)

    def one_block(args):
        qb, blk = args
        s = jnp.einsum('bqhcd,bkhcd->bhcqk', qb, k).astype(jnp.float32) * scale
        q_chunk = (blk * Q_BLOCK + jnp.arange(Q_BLOCK)) // CHUNK
        mask = k_chunk[None, :] <= q_chunk[:, None]
        s = jnp.where(mask[None, None, None], s, NEG_INF)
        p = jax.nn.softmax(s, axis=-1)
        a = p[:, :, 0] - lam * p[:, :, 1]
        return jnp.einsum('bhqk,bkhe->bqhe', a.astype(v.dtype), v)

    o = lax.map(one_block, (q_blocks, jnp.arange(n_blk)))
    o = o.transpose(1, 0, 2, 3, 4).reshape(B, S, N_HEADS, 2 * HEAD_DIM)
    o = rmsnorm(o, sub_g, SUBLN_EPS) * (1.0 - lambda_init)
    return o.reshape(B, S, D) @ w_o


def hier_moe(h, wr_g, br_g, wr_e, br_e, w_gate, w_up, w_down):
    B, S, D = h.shape
    T = B * S
    xf = h.reshape(T, D)
    g_logits = (xf @ wr_g + br_g).astype(jnp.float32)
    g_prob = jax.nn.softmax(g_logits, axis=-1)
    g_top = jnp.argmax(g_prob, axis=-1)
    p_g = jnp.max(g_prob, axis=-1)
    e_logits = (xf @ wr_e + br_e).astype(jnp.float32).reshape(T, N_GROUPS, EXPERTS_PER_GROUP)
    e_sel = jnp.take_along_axis(e_logits, g_top[:, None, None], axis=1)[:, 0]
    e_prob = jax.nn.softmax(e_sel, axis=-1)
    top_v, top_i = lax.top_k(e_prob, TOP_K_IN_GROUP)
    top_v = top_v / jnp.sum(top_v, axis=-1, keepdims=True)
    ids = g_top[:, None] * EXPERTS_PER_GROUP + top_i
    wts = p_g[:, None] * top_v
    gates = jnp.sum(jax.nn.one_hot(ids, N_EXPERTS, dtype=jnp.float32) * wts[..., None], axis=1)
    gates = gates.astype(h.dtype)

    def expert_step(y, xs):
        wg, wu, wd, gcol = xs
        hid = jax.nn.silu(xf @ wg) * (xf @ wu)
        return y + gcol[:, None] * (hid @ wd), None

    y, _ = lax.scan(expert_step, jnp.zeros_like(xf), (w_gate, w_up, w_down, gates.T))
    return y.reshape(B, S, D)


def setup_inputs(seed: int = 0) -> dict:
    key = jax.random.key(seed)
    ks = jax.random.split(key, 24)
    f32 = jnp.float32
    nrm = lambda k, shp, s: jax.random.normal(k, shp, f32) * s
    D, F = D_MODEL, EXPERT_FF
    return {
        "x": nrm(ks[0], (BATCH, SEQ, D), 1.0),
        "norm_mix_g": 1.0 + nrm(ks[1], (DEPTH, D), 0.02),
        "norm_ffn_g": 1.0 + nrm(ks[2], (DEPTH, D), 0.02),
        "pool_w": nrm(ks[3], (N_POOL_LAYERS, POOL_GROUPS, POOL_CG, POOL_CG), POOL_CG ** -0.5),
        "pool_scale": 1.0 + nrm(ks[4], (N_POOL_LAYERS, D), 0.02),
        "attn_wq": nrm(ks[5], (N_DIFF_LAYERS, D, D), D ** -0.5),
        "attn_wk": nrm(ks[6], (N_DIFF_LAYERS, D, D), D ** -0.5),
        "attn_wv": nrm(ks[7], (N_DIFF_LAYERS, D, D), D ** -0.5),
        "attn_wo": nrm(ks[8], (N_DIFF_LAYERS, D, D), D ** -0.5),
        "lam_q1": nrm(ks[9], (N_DIFF_LAYERS, HEAD_DIM), 0.1),
        "lam_k1": nrm(ks[10], (N_DIFF_LAYERS, HEAD_DIM), 0.1),
        "lam_q2": nrm(ks[11], (N_DIFF_LAYERS, HEAD_DIM), 0.1),
        "lam_k2": nrm(ks[12], (N_DIFF_LAYERS, HEAD_DIM), 0.1),
        "attn_sub_g": 1.0 + nrm(ks[13], (N_DIFF_LAYERS, 2 * HEAD_DIM), 0.02),
        "router_g_w": nrm(ks[14], (DEPTH, D, N_GROUPS), D ** -0.5),
        "router_g_b": nrm(ks[15], (DEPTH, N_GROUPS), 0.01),
        "router_e_w": nrm(ks[16], (DEPTH, D, N_EXPERTS), D ** -0.5),
        "router_e_b": nrm(ks[17], (DEPTH, N_EXPERTS), 0.01),
        "exp_w_gate": nrm(ks[18], (DEPTH, N_EXPERTS, D, F), D ** -0.5),
        "exp_w_up": nrm(ks[19], (DEPTH, N_EXPERTS, D, F), D ** -0.5),
        "exp_w_down": nrm(ks[20], (DEPTH, N_EXPERTS, F, D), F ** -0.5),
        "final_g": 1.0 + nrm(ks[21], (D,), 0.02),
    }


def reference(x, norm_mix_g, norm_ffn_g, pool_w, pool_scale, attn_wq, attn_wk, attn_wv, attn_wo,
              lam_q1, lam_k1, lam_q2, lam_k2, attn_sub_g, router_g_w, router_g_b, router_e_w,
              router_e_b, exp_w_gate, exp_w_up, exp_w_down, final_g):
    for i in range(DEPTH):
        h = rmsnorm(x, norm_mix_g[i])
        j = i // N_MIXERS
        if i % N_MIXERS == 0:
            x = x + pool_mixer(h, pool_w[j], pool_scale[j])
        else:
            lambda_init = 0.8 - 0.6 * math.exp(-0.3 * i)
            x = x + diff_attention(h, attn_wq[j], attn_wk[j], attn_wv[j], attn_wo[j],
                                   lam_q1[j], lam_k1[j], lam_q2[j], lam_k2[j], attn_sub_g[j],
                                   lambda_init)
        h = rmsnorm(x, norm_ffn_g[i])
        x = x + hier_moe(h, router_g_w[i], router_g_b[i], router_e_w[i], router_e_b[i],
                         exp_w_gate[i], exp_w_up[i], exp_w_down[i])
    return rmsnorm(x, final_g)
```

```python
import functools
import math

import jax
import jax.numpy as jnp
from jax import lax
from jax.experimental import pallas as pl
from jax.experimental.pallas import tpu as pltpu

F32 = jnp.float32
BF16 = jnp.bfloat16
I32 = jnp.int32

D_MODEL = 2048
BATCH = 8
SEQ = 2048
N_TOK = BATCH * SEQ
CHUNK = 64
POOL_WINDOWS = (2, 4, 8, 16)
POOL_CG = D_MODEL // len(POOL_WINDOWS)
POOL_HALO = 16
N_HEADS = 8
HEAD_DIM = 128
N_GROUPS = 4
EXPERTS_PER_GROUP = 8
N_EXPERTS = 32
EXPERT_FF = 512
NORM_EPS = 1e-6
SUBLN_EPS = 1e-5
NEG_INF = -1e30

SUBLANES = 8
LANES = 128
ROW_W = LANES
RPT = D_MODEL // ROW_W
MXU_K = 256
ROUTER_COLS = 128
E_COL0 = 8

TS = 256
TM = 256
N_SORT = 2 * N_TOK + N_EXPERTS * TM
N_TILES = N_SORT // TM
GATHER_ROWS = 512
TQ = 512
TK = 256
MM_TM = 512
VMEM_LIMIT = 56 * 1024 * 1024


def _rms(x, g, eps):
    ms = jnp.mean(x * x, axis=-1, keepdims=True)
    return x * lax.rsqrt(ms + eps) * g


def _to_rows(ref, val, n):
    for s in range(RPT):
        ref[pl.ds(s, n, stride=RPT), :] = val[:, s * ROW_W:(s + 1) * ROW_W]


def _from_rows(ref, n, lo=0, hi=RPT):
    return jnp.concatenate(
        [ref[pl.ds(s, n, stride=RPT), :] for s in range(lo, hi)], axis=-1)


def _router_tail(x_new, gffn_ref, wr_ref, br_ref, h8_ref, ids_ref, wts_ref):
    n = x_new.shape[0]
    h = _rms(x_new, gffn_ref[...], NORM_EPS)
    _to_rows(h8_ref, h, n)
    h_hi = h.astype(BF16)
    h_lo = (h - h_hi.astype(F32)).astype(BF16)
    w_hi = wr_ref[0]
    w_lo = wr_ref[1]
    logits = (jnp.dot(h_hi, w_hi, preferred_element_type=F32)
              + jnp.dot(h_hi, w_lo, preferred_element_type=F32)
              + jnp.dot(h_lo, w_hi, preferred_element_type=F32))
    logits = logits + br_ref[...]
    lt = logits.T
    gl = lt[0:SUBLANES, :]
    io8 = lax.broadcasted_iota(I32, gl.shape, 0)
    gmax = jnp.max(gl, axis=0, keepdims=True)
    gtop = jnp.min(jnp.where(gl == gmax, io8, SUBLANES), axis=0, keepdims=True)
    p_g = 1.0 / jnp.sum(jnp.exp(gl - gmax), axis=0, keepdims=True)
    esel = lt[E_COL0:E_COL0 + EXPERTS_PER_GROUP, :]
    for gi in range(1, N_GROUPS):
        lo = E_COL0 + gi * EXPERTS_PER_GROUP
        esel = jnp.where(gtop == gi, lt[lo:lo + EXPERTS_PER_GROUP, :], esel)
    m1 = jnp.max(esel, axis=0, keepdims=True)
    i1 = jnp.min(jnp.where(esel == m1, io8, SUBLANES), axis=0, keepdims=True)
    rest = jnp.where(io8 == i1, -jnp.inf, esel)
    m2 = jnp.max(rest, axis=0, keepdims=True)
    i2 = jnp.min(jnp.where(rest == m2, io8, SUBLANES), axis=0, keepdims=True)
    r = jnp.exp(m2 - m1)
    v1 = 1.0 / (1.0 + r)
    v2 = r / (1.0 + r)
    base = gtop * EXPERTS_PER_GROUP
    ids_ref[0:1, :] = base + i1
    ids_ref[1:2, :] = base + i2
    wts_ref[0:1, :] = p_g * v1
    wts_ref[1:2, :] = p_g * v2


def _router_out_shapes():
    return (jax.ShapeDtypeStruct((N_TOK * RPT, ROW_W), F32),
            jax.ShapeDtypeStruct((2, N_TOK), I32),
            jax.ShapeDtypeStruct((2, N_TOK), F32))


def _router_out_specs(tok_map):
    return (pl.BlockSpec((TS * RPT, ROW_W), lambda *a: (tok_map(*a), 0)),
            pl.BlockSpec((2, TS), lambda *a: (0, tok_map(*a))),
            pl.BlockSpec((2, TS), lambda *a: (0, tok_map(*a))))


def _pool_kernel(x_ref, halo_ref, gmix_ref, wpool_ref, pscale_ref, gffn_ref, wr_ref, br_ref,
                 x1_ref, h8_ref, ids_ref, wts_ref, hs_ref):
    s = pl.program_id(1)
    xt = x_ref[...]
    g = gmix_ref[...]
    h = _rms(xt, g, NORM_EPS)
    hh = _rms(halo_ref[...], g, NORM_EPS)
    hs_ref[0:POOL_HALO, :] = jnp.where(s > 0, hh, 0.0)
    hs_ref[POOL_HALO:, :] = h
    pos = s * TS + lax.broadcasted_iota(I32, (TS, 1), 0)
    outs = []
    for gi, w in enumerate(POOL_WINDOWS):
        c0 = gi * POOL_CG
        he = hs_ref[:, c0:c0 + POOL_CG]
        acc = he
        sh = 1
        while sh < w:
            acc = acc + pltpu.roll(acc, sh, axis=0)
            sh *= 2
        inv_cnt = 1.0 / jnp.minimum(pos + 1, w).astype(F32)
        mix = acc[POOL_HALO:, :] * inv_cnt - he[POOL_HALO:, :]
        outs.append(jnp.dot(mix.astype(BF16), wpool_ref[gi], preferred_element_type=F32))
    x1 = xt + jnp.concatenate(outs, axis=-1) * pscale_ref[...]
    x1_ref[...] = x1
    _router_tail(x1, gffn_ref, wr_ref, br_ref, h8_ref, ids_ref, wts_ref)


def _pool_layer(x2d, gmix, wpool_bf, pscale, gffn, wr2, br):
    n_s = SEQ // TS
    tok_map = lambda b, s: b * n_s + s
    halo_blocks = TS // POOL_HALO
    const2 = lambda b, s: (0, 0)
    return pl.pallas_call(
        _pool_kernel,
        grid=(BATCH, n_s),
        in_specs=[
            pl.BlockSpec((TS, D_MODEL), lambda b, s: (tok_map(b, s), 0)),
            pl.BlockSpec((POOL_HALO, D_MODEL),
                         lambda b, s: (jnp.maximum(tok_map(b, s) * halo_blocks - 1, 0), 0)),
            pl.BlockSpec((1, D_MODEL), const2),
            pl.BlockSpec((len(POOL_WINDOWS), POOL_CG, POOL_CG), lambda b, s: (0, 0, 0)),
            pl.BlockSpec((1, D_MODEL), const2),
            pl.BlockSpec((1, D_MODEL), const2),
            pl.BlockSpec((2, D_MODEL, ROUTER_COLS), lambda b, s: (0, 0, 0)),
            pl.BlockSpec((1, ROUTER_COLS), const2),
        ],
        out_specs=(pl.BlockSpec((TS, D_MODEL), lambda b, s: (tok_map(b, s), 0)),)
        + _router_out_specs(tok_map),
        out_shape=(jax.ShapeDtypeStruct((N_TOK, D_MODEL), F32),) + _router_out_shapes(),
        scratch_shapes=[pltpu.VMEM((TS + POOL_HALO, D_MODEL), F32)],
        compiler_params=pltpu.CompilerParams(
            dimension_semantics=("arbitrary", "arbitrary"), vmem_limit_bytes=VMEM_LIMIT),
        name="pool_mixer_router",
    )(x2d, x2d, gmix, wpool_bf, pscale, gffn, wr2, br)


def _add_router_kernel(x_ref, d_ref, gffn_ref, wr_ref, br_ref,
                       xo_ref, h8_ref, ids_ref, wts_ref):
    x_new = x_ref[...] + d_ref[...]
    xo_ref[...] = x_new
    _router_tail(x_new, gffn_ref, wr_ref, br_ref, h8_ref, ids_ref, wts_ref)


def _add_router(x2d, delta, gffn, wr2, br):
    tok_map = lambda i: i
    const2 = lambda i: (0, 0)
    return pl.pallas_call(
        _add_router_kernel,
        grid=(N_TOK // TS,),
        in_specs=[
            pl.BlockSpec((TS, D_MODEL), lambda i: (i, 0)),
            pl.BlockSpec((TS, D_MODEL), lambda i: (i, 0)),
            pl.BlockSpec((1, D_MODEL), const2),
            pl.BlockSpec((2, D_MODEL, ROUTER_COLS), lambda i: (0, 0, 0)),
            pl.BlockSpec((1, ROUTER_COLS), const2),
        ],
        out_specs=(pl.BlockSpec((TS, D_MODEL), lambda i: (i, 0)),) + _router_out_specs(tok_map),
        out_shape=(jax.ShapeDtypeStruct((N_TOK, D_MODEL), F32),) + _router_out_shapes(),
        compiler_params=pltpu.CompilerParams(
            dimension_semantics=("arbitrary",), vmem_limit_bytes=VMEM_LIMIT),
        name="add_router",
    )(x2d, delta, gffn, wr2, br)


def _gather_kernel(idx_ref, src_ref, out_ref, sem):
    i = pl.program_id(0)

    def row_copy(r):
        t = idx_ref[0, 0, r]
        src = src_ref.at[pl.ds(pl.multiple_of(t * RPT, RPT), RPT)]
        dst = out_ref.at[pl.ds(pl.multiple_of((i * GATHER_ROWS + r) * RPT, RPT), RPT)]
        return pltpu.make_async_copy(src, dst, sem)

    def issue(r, c):
        row_copy(r).start()
        return c

    def drain(r, c):
        row_copy(r).wait()
        return c

    lax.fori_loop(0, GATHER_ROWS, issue, 0)
    lax.fori_loop(0, GATHER_ROWS, drain, 0)


def _gather_rows(src8, idx):
    n = idx.shape[0]
    steps = n // GATHER_ROWS
    return pl.pallas_call(
        _gather_kernel,
        grid=(steps,),
        in_specs=[
            pl.BlockSpec((1, 1, GATHER_ROWS), lambda i: (i, 0, 0), memory_space=pltpu.SMEM),
            pl.BlockSpec(memory_space=pl.ANY),
        ],
        out_specs=pl.BlockSpec(memory_space=pl.ANY),
        out_shape=jax.ShapeDtypeStruct((n * RPT, ROW_W), src8.dtype),
        scratch_shapes=[pltpu.SemaphoreType.DMA(())],
        compiler_params=pltpu.CompilerParams(dimension_semantics=("arbitrary",)),
        name="gather_rows",
    )(idx.reshape(steps, 1, GATHER_ROWS), src8)


def _moe_kernel(te_ref, tv_ref, x_ref, wg_ref, wu_ref, wd_ref, gate_ref, o_ref,
                wg_bf, wu_bf, wd_bf):
    i = pl.program_id(0)
    e = te_ref[i]
    e_prev = te_ref[jnp.maximum(i - 1, 0)]

    @pl.when((i == 0) | (e != e_prev))
    def _():
        wg_bf[...] = wg_ref[0].astype(BF16)
        wu_bf[...] = wu_ref[0].astype(BF16)
        wd_bf[...] = wd_ref[0].astype(BF16)

    @pl.when(tv_ref[i] > 0)
    def _():
        g = jnp.zeros((TM, EXPERT_FF), F32)
        u = jnp.zeros((TM, EXPERT_FF), F32)
        per_pass = MXU_K // ROW_W
        for c in range(D_MODEL // MXU_K):
            xk = _from_rows(x_ref, TM, c * per_pass, (c + 1) * per_pass).astype(BF16)
            g = g + jnp.dot(xk, wg_bf[c * MXU_K:(c + 1) * MXU_K, :], preferred_element_type=F32)
            u = u + jnp.dot(xk, wu_bf[c * MXU_K:(c + 1) * MXU_K, :], preferred_element_type=F32)
        hid = g / (1.0 + jnp.exp(-g)) * u
        y = jnp.dot(hid.astype(BF16), wd_bf[...], preferred_element_type=F32)
        _to_rows(o_ref, y * gate_ref[...], TM)

    @pl.when(tv_ref[i] == 0)
    def _():
        o_ref[...] = jnp.zeros(o_ref.shape, o_ref.dtype)


def _moe_grouped(tile_expert, tile_valid, xs8, w_gate, w_up, w_down, gate_col):
    grid_spec = pltpu.PrefetchScalarGridSpec(
        num_scalar_prefetch=2,
        grid=(N_TILES,),
        in_specs=[
            pl.BlockSpec((TM * RPT, ROW_W), lambda i, te, tv: (i, 0)),
            pl.BlockSpec((1, D_MODEL, EXPERT_FF), lambda i, te, tv: (te[i], 0, 0)),
            pl.BlockSpec((1, D_MODEL, EXPERT_FF), lambda i, te, tv: (te[i], 0, 0)),
            pl.BlockSpec((1, EXPERT_FF, D_MODEL), lambda i, te, tv: (te[i], 0, 0)),
            pl.BlockSpec((TM, 1), lambda i, te, tv: (i, 0)),
        ],
        out_specs=pl.BlockSpec((TM * RPT, ROW_W), lambda i, te, tv: (i, 0)),
        scratch_shapes=[pltpu.VMEM((D_MODEL, EXPERT_FF), BF16),
                        pltpu.VMEM((D_MODEL, EXPERT_FF), BF16),
                        pltpu.VMEM((EXPERT_FF, D_MODEL), BF16)],
    )
    return pl.pallas_call(
        _moe_kernel,
        grid_spec=grid_spec,
        out_shape=jax.ShapeDtypeStruct((N_SORT * RPT, ROW_W), F32),
        compiler_params=pltpu.CompilerParams(
            dimension_semantics=("arbitrary",), vmem_limit_bytes=VMEM_LIMIT),
        name="moe_grouped",
    )(tile_expert, tile_valid, xs8, w_gate, w_up, w_down, gate_col)


def _route_plan(ids, wts):
    e = ids.reshape(-1)
    w = wts.reshape(-1)
    onehot = (e[:, None] == jnp.arange(N_EXPERTS, dtype=I32)[None, :]).astype(I32)
    csum = jnp.cumsum(onehot, axis=0)
    rank = jnp.sum(onehot * csum, axis=1) - 1
    counts = csum[-1]
    padded = (counts + TM - 1) // TM * TM
    ends = jnp.cumsum(padded)
    pos = (ends - padded)[e] + rank
    tok = jnp.tile(jnp.arange(N_TOK, dtype=I32), 2)
    src_tok = jnp.zeros((N_SORT,), I32).at[pos].set(tok)
    gate = jnp.zeros((N_SORT,), F32).at[pos].set(w)
    tile_start = jnp.arange(N_TILES, dtype=I32) * TM
    n_done = jnp.sum((ends[None, :] <= tile_start[:, None]).astype(I32), axis=1)
    tile_expert = jnp.minimum(n_done, N_EXPERTS - 1)
    tile_valid = (tile_start < ends[-1]).astype(I32)
    return pos.astype(I32), src_tok, gate.reshape(N_SORT, 1), tile_expert, tile_valid


def _moe_layer(h8, ids, wts, w_gate, w_up, w_down):
    pos, src_tok, gate_col, tile_expert, tile_valid = _route_plan(ids, wts)
    xs8 = _gather_rows(h8, src_tok)
    ys8 = _moe_grouped(tile_expert, tile_valid, xs8, w_gate, w_up, w_down, gate_col)
    return _gather_rows(ys8, pos)


def _combine_norm_kernel(x_ref, y0_ref, y1_ref, g_ref, *out_refs, emit_sum):
    x_new = x_ref[...] + _from_rows(y0_ref, TS) + _from_rows(y1_ref, TS)
    normed = _rms(x_new, g_ref[...], NORM_EPS)
    if emit_sum:
        out_refs[0][...] = x_new
        out_refs[1][...] = normed.astype(out_refs[1].dtype)
    else:
        out_refs[0][...] = normed.astype(out_refs[0].dtype)


def _combine_norm(x2d, yg8, g, *, emit_sum, norm_dtype):
    n_blk = N_TOK // TS
    tok_spec = pl.BlockSpec((TS, D_MODEL), lambda i: (i, 0))
    out_specs = [tok_spec]
    out_shape = [jax.ShapeDtypeStruct((N_TOK, D_MODEL), norm_dtype)]
    if emit_sum:
        out_specs = [tok_spec, tok_spec]
        out_shape = [jax.ShapeDtypeStruct((N_TOK, D_MODEL), F32)] + out_shape
    return pl.pallas_call(
        functools.partial(_combine_norm_kernel, emit_sum=emit_sum),
        grid=(n_blk,),
        in_specs=[
            tok_spec,
            pl.BlockSpec((TS * RPT, ROW_W), lambda i: (i, 0)),
            pl.BlockSpec((TS * RPT, ROW_W), lambda i: (n_blk + i, 0)),
            pl.BlockSpec((1, D_MODEL), lambda i: (0, 0)),
        ],
        out_specs=tuple(out_specs),
        out_shape=tuple(out_shape),
        compiler_params=pltpu.CompilerParams(
            dimension_semantics=("arbitrary",), vmem_limit_bytes=VMEM_LIMIT),
        name="combine_norm",
    )(x2d, yg8, yg8, g)


def _mm_kernel(a_ref, b_ref, o_ref):
    o_ref[...] = jnp.dot(a_ref[...], b_ref[...], preferred_element_type=F32).astype(o_ref.dtype)


def _matmul(a, b, out_dtype, tn):
    m, k = a.shape
    n = b.shape[1]
    return pl.pallas_call(
        _mm_kernel,
        grid=(n // tn, m // MM_TM),
        in_specs=[pl.BlockSpec((MM_TM, k), lambda j, i: (i, 0)),
                  pl.BlockSpec((k, tn), lambda j, i: (0, j))],
        out_specs=pl.BlockSpec((MM_TM, tn), lambda j, i: (i, j)),
        out_shape=jax.ShapeDtypeStruct((m, n), out_dtype),
        compiler_params=pltpu.CompilerParams(
            dimension_semantics=("arbitrary", "arbitrary"), vmem_limit_bytes=VMEM_LIMIT),
        name="dense_matmul",
    )(a, b)


def _attn_kernel(lam_ref, q_ref, k_ref, v_ref, subg_ref, o_ref,
                 m_ref, l_ref, acc_ref, *, lambda_init):
    qi = pl.program_id(2)
    scale = HEAD_DIM ** -0.5
    q = q_ref[...]
    qs = (q[:, :HEAD_DIM], q[:, HEAD_DIM:])
    m_ref[...] = jnp.full(m_ref.shape, NEG_INF, F32)
    l_ref[...] = jnp.zeros(l_ref.shape, F32)
    acc_ref[...] = jnp.zeros(acc_ref.shape, F32)

    def tile_update(j, mask):
        k0 = pl.multiple_of(j * TK, TK)
        kt = k_ref[pl.ds(k0, TK), :]
        vt = v_ref[pl.ds(k0, TK), :]
        for c in range(2):
            kc = kt[:, c * HEAD_DIM:(c + 1) * HEAD_DIM]
            s = lax.dot_general(qs[c], kc, (((1,), (1,)), ((), ())),
                                preferred_element_type=F32) * scale
            if mask is not None:
                s = jnp.where(mask, s, NEG_INF)
            m_old = m_ref[c]
            m_new = jnp.maximum(m_old, jnp.max(s, axis=-1, keepdims=True))
            alpha = jnp.exp(m_old - m_new)
            p = jnp.exp(s - m_new)
            l_ref[c] = alpha * l_ref[c] + jnp.sum(p, axis=-1, keepdims=True)
            acc_ref[c] = alpha * acc_ref[c] + jnp.dot(p.astype(BF16), vt,
                                                      preferred_element_type=F32)
            m_ref[c] = m_new

    n_full = qi * (TQ // TK)

    def full_body(j, c):
        tile_update(j, None)
        return c

    lax.fori_loop(0, n_full, full_body, 0)
    row_chunk = lax.broadcasted_iota(I32, (TQ, TK), 0) // CHUNK
    col_chunk = lax.broadcasted_iota(I32, (TQ, TK), 1) // CHUNK
    for d in range(TQ // TK):
        tile_update(n_full + d, col_chunk + d * (TK // CHUNK) <= row_chunk)

    lv = lam_ref[...]
    lam = (jnp.exp(jnp.sum(lv[0:1] * lv[1:2], axis=-1, keepdims=True))
           - jnp.exp(jnp.sum(lv[2:3] * lv[3:4], axis=-1, keepdims=True)) + lambda_init)
    o = acc_ref[0] / l_ref[0] - lam * (acc_ref[1] / l_ref[1])
    o = _rms(o, subg_ref[...], SUBLN_EPS) * (1.0 - lambda_init)
    o_ref[...] = o.astype(o_ref.dtype)


def _diff_attention(qkv, lam_rows, sub_g, lambda_init):
    n_q = SEQ // TQ
    head_w = 2 * HEAD_DIM
    return pl.pallas_call(
        functools.partial(_attn_kernel, lambda_init=lambda_init),
        grid=(BATCH, N_HEADS, n_q),
        in_specs=[
            pl.BlockSpec((SUBLANES, HEAD_DIM), lambda b, h, i: (0, 0)),
            pl.BlockSpec((TQ, head_w), lambda b, h, i: (b * n_q + i, h)),
            pl.BlockSpec((SEQ, head_w), lambda b, h, i: (b, N_HEADS + h)),
            pl.BlockSpec((SEQ, head_w), lambda b, h, i: (b, 2 * N_HEADS + h)),
            pl.BlockSpec((1, head_w), lambda b, h, i: (0, 0)),
        ],
        out_specs=pl.BlockSpec((TQ, head_w), lambda b, h, i: (b * n_q + i, h)),
        out_shape=jax.ShapeDtypeStruct((N_TOK, D_MODEL), BF16),
        scratch_shapes=[pltpu.VMEM((2, TQ, 1), F32),
                        pltpu.VMEM((2, TQ, 1), F32),
                        pltpu.VMEM((2, TQ, head_w), F32)],
        compiler_params=pltpu.CompilerParams(
            dimension_semantics=("arbitrary", "arbitrary", "arbitrary"),
            vmem_limit_bytes=VMEM_LIMIT),
        name="diff_attention",
    )(lam_rows, qkv, qkv, qkv, sub_g)


def _router_params(wr_g, br_g, wr_e, br_e):
    w = jnp.zeros((D_MODEL, ROUTER_COLS), F32)
    w = w.at[:, 0:N_GROUPS].set(wr_g).at[:, E_COL0:E_COL0 + N_EXPERTS].set(wr_e)
    b = jnp.full((1, ROUTER_COLS), NEG_INF, F32)
    b = b.at[0, 0:N_GROUPS].set(br_g).at[0, E_COL0:E_COL0 + N_EXPERTS].set(br_e)
    w_hi = w.astype(BF16)
    w_lo = (w - w_hi.astype(F32)).astype(BF16)
    return jnp.stack([w_hi, w_lo]), b


def kernel(x, norm_mix_g, norm_ffn_g, pool_w, pool_scale, attn_wq, attn_wk, attn_wv, attn_wo,
           lam_q1, lam_k1, lam_q2, lam_k2, attn_sub_g, router_g_w, router_g_b, router_e_w,
           router_e_b, exp_w_gate, exp_w_up, exp_w_down, final_g):
    x2d = x.reshape(N_TOK, D_MODEL)
    row = lambda v: v.reshape(1, -1)

    wr0, br0 = _router_params(router_g_w[0], router_g_b[0], router_e_w[0], router_e_b[0])
    x1, h8, ids, wts = _pool_layer(x2d, row(norm_mix_g[0]), pool_w[0].astype(BF16),
                                   row(pool_scale[0]), row(norm_ffn_g[0]), wr0, br0)
    yg8 = _moe_layer(h8, ids, wts, exp_w_gate[0], exp_w_up[0], exp_w_down[0])
    x2, hn = _combine_norm(x1, yg8, row(norm_mix_g[1]), emit_sum=True, norm_dtype=BF16)

    lambda_init = 0.8 - 0.6 * math.exp(-0.3 * 1)
    w_qkv = jnp.concatenate([attn_wq[0], attn_wk[0], attn_wv[0]], axis=1).astype(BF16)
    qkv = _matmul(hn, w_qkv, BF16, D_MODEL)
    lam_rows = jnp.zeros((SUBLANES, HEAD_DIM), F32)
    lam_rows = lam_rows.at[0].set(lam_q1[0]).at[1].set(lam_k1[0])
    lam_rows = lam_rows.at[2].set(lam_q2[0]).at[3].set(lam_k2[0])
    att = _diff_attention(qkv, lam_rows, row(attn_sub_g[0]), lambda_init)
    proj = _matmul(att, attn_wo[0].astype(BF16), F32, D_MODEL)
    wr1, br1 = _router_params(router_g_w[1], router_g_b[1], router_e_w[1], router_e_b[1])
    x3, h8, ids, wts = _add_router(x2, proj, row(norm_ffn_g[1]), wr1, br1)
    yg8 = _moe_layer(h8, ids, wts, exp_w_gate[1], exp_w_up[1], exp_w_down[1])
    (out,) = _combine_norm(x3, yg8, row(final_g), emit_sum=False, norm_dtype=F32)
    return out.reshape(BATCH, SEQ, D_MODEL)
```

```python
import functools
import math

import jax
import jax.numpy as jnp
from jax import lax
from jax.experimental import pallas as pl
from jax.experimental.pallas import tpu as pltpu

F32 = jnp.float32
BF16 = jnp.bfloat16
I32 = jnp.int32

D_MODEL = 2048
BATCH = 8
SEQ = 2048
N_TOK = BATCH * SEQ
CHUNK = 64
POOL_WINDOWS = (2, 4, 8, 16)
POOL_CG = D_MODEL // len(POOL_WINDOWS)
POOL_HALO = 16
N_HEADS = 8
HEAD_DIM = 128
N_GROUPS = 4
EXPERTS_PER_GROUP = 8
N_EXPERTS = 32
EXPERT_FF = 512
NORM_EPS = 1e-6
SUBLN_EPS = 1e-5
NEG_INF = -1e30

SUBLANES = 8
LANES = 128
ROUTER_COLS = LANES
E_COL0 = 8

TS = 256
TM = 256
N_SORT = 2 * N_TOK + N_EXPERTS * TM
N_TILES = N_SORT // TM
TQ = 512
TK = 512
MM_TM = 512
VMEM_LIMIT = 56 * 1024 * 1024


def _rms(x, g, eps):
    ms = jnp.mean(x * x, axis=-1, keepdims=True)
    return x * lax.rsqrt(ms + eps) * g


def _router_tail(x_new, gffn_ref, wr_ref, br_ref, h_ref, ids_ref, wts_ref):
    h = _rms(x_new, gffn_ref[...], NORM_EPS)
    h_ref[...] = h
    h_hi = h.astype(BF16)
    h_lo = (h - h_hi.astype(F32)).astype(BF16)
    w_hi = wr_ref[0]
    w_lo = wr_ref[1]
    logits = (jnp.dot(h_hi, w_hi, preferred_element_type=F32)
              + jnp.dot(h_hi, w_lo, preferred_element_type=F32)
              + jnp.dot(h_lo, w_hi, preferred_element_type=F32))
    logits = logits + br_ref[...]
    lt = logits.T
    gl = lt[0:SUBLANES, :]
    io8 = lax.broadcasted_iota(I32, gl.shape, 0)
    gmax = jnp.max(gl, axis=0, keepdims=True)
    gtop = jnp.min(jnp.where(gl == gmax, io8, SUBLANES), axis=0, keepdims=True)
    p_g = 1.0 / jnp.sum(jnp.exp(gl - gmax), axis=0, keepdims=True)
    esel = lt[E_COL0:E_COL0 + EXPERTS_PER_GROUP, :]
    for gi in range(1, N_GROUPS):
        lo = E_COL0 + gi * EXPERTS_PER_GROUP
        esel = jnp.where(gtop == gi, lt[lo:lo + EXPERTS_PER_GROUP, :], esel)
    m1 = jnp.max(esel, axis=0, keepdims=True)
    i1 = jnp.min(jnp.where(esel == m1, io8, SUBLANES), axis=0, keepdims=True)
    rest = jnp.where(io8 == i1, -jnp.inf, esel)
    m2 = jnp.max(rest, axis=0, keepdims=True)
    i2 = jnp.min(jnp.where(rest == m2, io8, SUBLANES), axis=0, keepdims=True)
    r = jnp.exp(m2 - m1)
    v1 = 1.0 / (1.0 + r)
    v2 = r / (1.0 + r)
    base = gtop * EXPERTS_PER_GROUP
    ids_ref[0:1, :] = base + i1
    ids_ref[1:2, :] = base + i2
    wts_ref[0:1, :] = p_g * v1
    wts_ref[1:2, :] = p_g * v2


def _router_out_shapes():
    return (jax.ShapeDtypeStruct((N_TOK, D_MODEL), F32),
            jax.ShapeDtypeStruct((2, N_TOK), I32),
            jax.ShapeDtypeStruct((2, N_TOK), F32))


def _router_out_specs(tok_map):
    return (pl.BlockSpec((TS, D_MODEL), lambda *a: (tok_map(*a), 0)),
            pl.BlockSpec((2, TS), lambda *a: (0, tok_map(*a))),
            pl.BlockSpec((2, TS), lambda *a: (0, tok_map(*a))))


def _pool_kernel(x_ref, halo_ref, gmix_ref, wpool_ref, pscale_ref, gffn_ref, wr_ref, br_ref,
                 x1_ref, h_ref, ids_ref, wts_ref, hs_ref):
    s = pl.program_id(1)
    xt = x_ref[...]
    g = gmix_ref[...]
    h = _rms(xt, g, NORM_EPS)
    hh = _rms(halo_ref[...], g, NORM_EPS)
    hs_ref[0:POOL_HALO, :] = jnp.where(s > 0, hh, 0.0)
    hs_ref[POOL_HALO:, :] = h
    pos = s * TS + lax.broadcasted_iota(I32, (TS, 1), 0)
    outs = []
    for gi, w in enumerate(POOL_WINDOWS):
        c0 = gi * POOL_CG
        he = hs_ref[:, c0:c0 + POOL_CG]
        acc = he
        sh = 1
        while sh < w:
            acc = acc + pltpu.roll(acc, sh, axis=0)
            sh *= 2
        inv_cnt = 1.0 / jnp.minimum(pos + 1, w).astype(F32)
        mix = acc[POOL_HALO:, :] * inv_cnt - he[POOL_HALO:, :]
        outs.append(jnp.dot(mix.astype(BF16), wpool_ref[gi], preferred_element_type=F32))
    x1 = xt + jnp.concatenate(outs, axis=-1) * pscale_ref[...]
    x1_ref[...] = x1
    _router_tail(x1, gffn_ref, wr_ref, br_ref, h_ref, ids_ref, wts_ref)


def _pool_layer(x2d, gmix, wpool_bf, pscale, gffn, wr2, br):
    n_s = SEQ // TS
    tok_map = lambda b, s: b * n_s + s
    halo_blocks = TS // POOL_HALO
    const2 = lambda b, s: (0, 0)
    return pl.pallas_call(
        _pool_kernel,
        grid=(BATCH, n_s),
        in_specs=[
            pl.BlockSpec((TS, D_MODEL), lambda b, s: (tok_map(b, s), 0)),
            pl.BlockSpec((POOL_HALO, D_MODEL),
                         lambda b, s: (jnp.maximum(tok_map(b, s) * halo_blocks - 1, 0), 0)),
            pl.BlockSpec((1, D_MODEL), const2),
            pl.BlockSpec((len(POOL_WINDOWS), POOL_CG, POOL_CG), lambda b, s: (0, 0, 0)),
            pl.BlockSpec((1, D_MODEL), const2),
            pl.BlockSpec((1, D_MODEL), const2),
            pl.BlockSpec((2, D_MODEL, ROUTER_COLS), lambda b, s: (0, 0, 0)),
            pl.BlockSpec((1, ROUTER_COLS), const2),
        ],
        out_specs=(pl.BlockSpec((TS, D_MODEL), lambda b, s: (tok_map(b, s), 0)),)
        + _router_out_specs(tok_map),
        out_shape=(jax.ShapeDtypeStruct((N_TOK, D_MODEL), F32),) + _router_out_shapes(),
        scratch_shapes=[pltpu.VMEM((TS + POOL_HALO, D_MODEL), F32)],
        compiler_params=pltpu.CompilerParams(
            dimension_semantics=("arbitrary", "arbitrary"), vmem_limit_bytes=VMEM_LIMIT),
        name="pool_mixer_router",
    )(x2d, x2d, gmix, wpool_bf, pscale, gffn, wr2, br)


def _add_router_kernel(x_ref, d_ref, gffn_ref, wr_ref, br_ref,
                       xo_ref, h_ref, ids_ref, wts_ref):
    x_new = x_ref[...] + d_ref[...]
    xo_ref[...] = x_new
    _router_tail(x_new, gffn_ref, wr_ref, br_ref, h_ref, ids_ref, wts_ref)


def _add_router(x2d, delta, gffn, wr2, br):
    tok_map = lambda i: i
    const2 = lambda i: (0, 0)
    return pl.pallas_call(
        _add_router_kernel,
        grid=(N_TOK // TS,),
        in_specs=[
            pl.BlockSpec((TS, D_MODEL), lambda i: (i, 0)),
            pl.BlockSpec((TS, D_MODEL), lambda i: (i, 0)),
            pl.BlockSpec((1, D_MODEL), const2),
            pl.BlockSpec((2, D_MODEL, ROUTER_COLS), lambda i: (0, 0, 0)),
            pl.BlockSpec((1, ROUTER_COLS), const2),
        ],
        out_specs=(pl.BlockSpec((TS, D_MODEL), lambda i: (i, 0)),) + _router_out_specs(tok_map),
        out_shape=(jax.ShapeDtypeStruct((N_TOK, D_MODEL), F32),) + _router_out_shapes(),
        compiler_params=pltpu.CompilerParams(
            dimension_semantics=("arbitrary",), vmem_limit_bytes=VMEM_LIMIT),
        name="add_router",
    )(x2d, delta, gffn, wr2, br)


def _moe_kernel(te_ref, nv_ref, src0_ref, srcn_ref, dst_ref, h_ref, wg_ref, wu_ref, wd_ref,
                gate_ref, yg_ref, xbuf, ybuf, wg_bf, wu_bf, wd_bf, sem_g, sem_s):
    i = pl.program_id(0)
    nv = nv_ref[0]
    slot = i % 2
    other = 1 - slot

    def gather_start(idx_ref, s, r):
        pltpu.make_async_copy(h_ref.at[pl.ds(idx_ref[0, 0, r], 1), :],
                              xbuf.at[s, pl.ds(r, 1), :], sem_g.at[s]).start()

    def gather_wait(s):
        pltpu.make_async_copy(h_ref.at[pl.ds(0, TM), :], xbuf.at[s], sem_g.at[s]).wait()

    def scatter_start(s, r):
        pltpu.make_async_copy(ybuf.at[s, pl.ds(r, 1), :],
                              yg_ref.at[pl.ds(dst_ref[0, 0, r], 1), :], sem_s.at[s]).start()

    def scatter_wait(s):
        pltpu.make_async_copy(ybuf.at[s], yg_ref.at[pl.ds(0, TM), :], sem_s.at[s]).wait()

    @pl.when(i == 0)
    def _():
        def body(r, c):
            gather_start(src0_ref, 0, r)
            return c
        lax.fori_loop(0, TM, body, 0)

    @pl.when(i >= 2)
    def _():
        scatter_wait(slot)

    @pl.when(i < nv)
    def _():
        e = te_ref[i]
        e_prev = te_ref[jnp.maximum(i - 1, 0)]

        @pl.when((i == 0) | (e != e_prev))
        def _():
            wg_bf[...] = wg_ref[0, 0].astype(BF16)
            wu_bf[...] = wu_ref[0, 0].astype(BF16)
            wd_bf[...] = wd_ref[0, 0].astype(BF16)

        gather_wait(slot)
        for r in range(TM):
            gather_start(srcn_ref, other, r)
        x = xbuf[slot].astype(BF16)
        g = jnp.dot(x, wg_bf[...], preferred_element_type=F32)
        u = jnp.dot(x, wu_bf[...], preferred_element_type=F32)
        hid = g / (1.0 + jnp.exp(-g)) * u
        y = jnp.dot(hid.astype(BF16), wd_bf[...], preferred_element_type=F32)
        ybuf[slot] = y * gate_ref[...]

    for r in range(TM):
        scatter_start(slot, r)

    @pl.when(i == nv - 1)
    def _():
        gather_wait(other)

    @pl.when(i == N_TILES - 1)
    def _():
        scatter_wait(slot)
        scatter_wait(other)


def _moe_grouped(layer, plan, h, w_gate, w_up, w_down):
    tile_expert, n_valid, src_tok, dst_row, gate_col = plan
    last = lambda nv: nv[0] - 1
    idx_blk = lambda f: pl.BlockSpec((1, 1, TM), f, memory_space=pltpu.SMEM)
    w_spec = lambda shape: pl.BlockSpec(
        (1, 1) + shape, lambda i, te, nv: (layer, te[jnp.minimum(i, last(nv))], 0, 0))
    grid_spec = pltpu.PrefetchScalarGridSpec(
        num_scalar_prefetch=2,
        grid=(N_TILES,),
        in_specs=[
            idx_blk(lambda i, te, nv: (0, 0, 0)),
            idx_blk(lambda i, te, nv: (jnp.minimum(i + 1, last(nv)), 0, 0)),
            idx_blk(lambda i, te, nv: (i, 0, 0)),
            pl.BlockSpec(memory_space=pl.ANY),
            w_spec((D_MODEL, EXPERT_FF)),
            w_spec((D_MODEL, EXPERT_FF)),
            w_spec((EXPERT_FF, D_MODEL)),
            pl.BlockSpec((TM, 1), lambda i, te, nv: (jnp.minimum(i, last(nv)), 0)),
        ],
        out_specs=pl.BlockSpec(memory_space=pl.ANY),
        scratch_shapes=[pltpu.VMEM((2, TM, D_MODEL), F32),
                        pltpu.VMEM((2, TM, D_MODEL), F32),
                        pltpu.VMEM((D_MODEL, EXPERT_FF), BF16),
                        pltpu.VMEM((D_MODEL, EXPERT_FF), BF16),
                        pltpu.VMEM((EXPERT_FF, D_MODEL), BF16),
                        pltpu.SemaphoreType.DMA((2,)),
                        pltpu.SemaphoreType.DMA((2,))],
    )
    idx3 = lambda v: v.reshape(N_TILES, 1, TM)
    return pl.pallas_call(
        _moe_kernel,
        grid_spec=grid_spec,
        out_shape=jax.ShapeDtypeStruct((N_SORT, D_MODEL), F32),
        compiler_params=pltpu.CompilerParams(
            dimension_semantics=("arbitrary",), vmem_limit_bytes=VMEM_LIMIT),
        name="moe_grouped",
    )(tile_expert, n_valid, idx3(src_tok), idx3(src_tok), idx3(dst_row), h,
      w_gate, w_up, w_down, gate_col)


def _route_plan(ids, wts):
    e = ids.reshape(-1)
    w = wts.reshape(-1)
    onehot = (e[:, None] == jnp.arange(N_EXPERTS, dtype=I32)[None, :]).astype(I32)
    csum = jnp.cumsum(onehot, axis=0)
    rank = jnp.sum(onehot * csum, axis=1) - 1
    counts = csum[-1]
    padded = (counts + TM - 1) // TM * TM
    ends = jnp.cumsum(padded)
    pos = (ends - padded)[e] + rank
    pair = jnp.full((N_SORT,), -1, I32).at[pos].set(jnp.arange(2 * N_TOK, dtype=I32))
    is_pad = pair < 0
    pad_row = 2 * N_TOK + jnp.cumsum(is_pad.astype(I32)) - 1
    src_tok = jnp.where(is_pad, 0, pair % N_TOK)
    gate = jnp.where(is_pad, 0.0, w[jnp.maximum(pair, 0)])
    dst_row = jnp.where(is_pad, pad_row, pair)
    n_valid = (ends[-1] // TM).astype(I32)
    tile_start = jnp.minimum(jnp.arange(N_TILES, dtype=I32), n_valid - 1) * TM
    n_done = jnp.sum((ends[None, :] <= tile_start[:, None]).astype(I32), axis=1)
    tile_expert = jnp.minimum(n_done, N_EXPERTS - 1)
    return tile_expert, n_valid.reshape(1), src_tok, dst_row, gate.reshape(N_SORT, 1)


def _combine_norm_kernel(x_ref, y0_ref, y1_ref, g_ref, *out_refs, emit_sum):
    x_new = x_ref[...] + y0_ref[...] + y1_ref[...]
    normed = _rms(x_new, g_ref[...], NORM_EPS)
    if emit_sum:
        out_refs[0][...] = x_new
        out_refs[1][...] = normed.astype(out_refs[1].dtype)
    else:
        out_refs[0][...] = normed.astype(out_refs[0].dtype)


def _combine_norm(x2d, yg, g, *, emit_sum, norm_dtype):
    n_blk = N_TOK // TS
    tok_spec = pl.BlockSpec((TS, D_MODEL), lambda i: (i, 0))
    out_specs = [tok_spec]
    out_shape = [jax.ShapeDtypeStruct((N_TOK, D_MODEL), norm_dtype)]
    if emit_sum:
        out_specs = [tok_spec, tok_spec]
        out_shape = [jax.ShapeDtypeStruct((N_TOK, D_MODEL), F32)] + out_shape
    return pl.pallas_call(
        functools.partial(_combine_norm_kernel, emit_sum=emit_sum),
        grid=(n_blk,),
        in_specs=[
            tok_spec,
            tok_spec,
            pl.BlockSpec((TS, D_MODEL), lambda i: (n_blk + i, 0)),
            pl.BlockSpec((1, D_MODEL), lambda i: (0, 0)),
        ],
        out_specs=tuple(out_specs),
        out_shape=tuple(out_shape),
        compiler_params=pltpu.CompilerParams(
            dimension_semantics=("arbitrary",), vmem_limit_bytes=VMEM_LIMIT),
        name="combine_norm",
    )(x2d, yg, yg, g)


def _mm_kernel(a_ref, b_ref, o_ref, *, first_block_scale):
    acc = jnp.dot(a_ref[...], b_ref[...], preferred_element_type=F32)
    if first_block_scale is not None:
        acc = acc * jnp.where(pl.program_id(0) == 0, first_block_scale, 1.0)
    o_ref[...] = acc.astype(o_ref.dtype)


def _matmul(a, b, out_dtype, tn, first_block_scale=None):
    m, k = a.shape
    n = b.shape[1]
    return pl.pallas_call(
        functools.partial(_mm_kernel, first_block_scale=first_block_scale),
        grid=(n // tn, m // MM_TM),
        in_specs=[pl.BlockSpec((MM_TM, k), lambda j, i: (i, 0)),
                  pl.BlockSpec((k, tn), lambda j, i: (0, j))],
        out_specs=pl.BlockSpec((MM_TM, tn), lambda j, i: (i, j)),
        out_shape=jax.ShapeDtypeStruct((m, n), out_dtype),
        compiler_params=pltpu.CompilerParams(
            dimension_semantics=("arbitrary", "arbitrary"), vmem_limit_bytes=VMEM_LIMIT),
        name="dense_matmul",
    )(a, b)


def _attn_kernel(lam_ref, q_ref, k_ref, v_ref, subg_ref, o_ref,
                 m_ref, l_ref, acc_ref, *, lambda_init):
    qi = pl.program_id(2)
    q = q_ref[...]
    qs = (q[:, :HEAD_DIM], q[:, HEAD_DIM:])
    m_ref[...] = jnp.full(m_ref.shape, NEG_INF, F32)
    l_ref[...] = jnp.zeros(l_ref.shape, F32)
    acc_ref[...] = jnp.zeros(acc_ref.shape, F32)
    lane_rep = lambda a, width: jnp.concatenate([a] * (width // LANES), axis=1)

    def tile_update(j, mask):
        k0 = pl.multiple_of(j * TK, TK)
        kt = k_ref[pl.ds(k0, TK), :]
        vt = v_ref[pl.ds(k0, TK), :]
        for c in range(2):
            kc = kt[:, c * HEAD_DIM:(c + 1) * HEAD_DIM]
            s = lax.dot_general(qs[c], kc, (((1,), (1,)), ((), ())),
                                preferred_element_type=F32)
            if mask is not None:
                s = jnp.where(mask, s, NEG_INF)
            m_old = m_ref[c]
            m_new = jnp.maximum(m_old, jnp.max(s, axis=-1, keepdims=True))
            alpha = jnp.exp2(m_old - m_new)
            p = jnp.exp2(s - lane_rep(m_new, TK))
            l_ref[c] = alpha * l_ref[c] + jnp.sum(p, axis=-1, keepdims=True)
            acc_ref[c] = lane_rep(alpha, 2 * HEAD_DIM) * acc_ref[c] + jnp.dot(
                p.astype(BF16), vt, preferred_element_type=F32)
            m_ref[c] = m_new

    def full_body(j, c):
        tile_update(j, None)
        return c

    lax.fori_loop(0, qi, full_body, 0)
    row_chunk = lax.broadcasted_iota(I32, (TQ, TK), 0) // CHUNK
    col_chunk = lax.broadcasted_iota(I32, (TQ, TK), 1) // CHUNK
    tile_update(qi, col_chunk <= row_chunk)

    lv = lam_ref[...]
    lam = (jnp.exp(jnp.sum(lv[0:1] * lv[1:2], axis=-1, keepdims=True))
           - jnp.exp(jnp.sum(lv[2:3] * lv[3:4], axis=-1, keepdims=True)) + lambda_init)
    width = 2 * HEAD_DIM
    o = (acc_ref[0] / lane_rep(l_ref[0], width)
         - lam * (acc_ref[1] / lane_rep(l_ref[1], width)))
    o = _rms(o, subg_ref[...], SUBLN_EPS) * (1.0 - lambda_init)
    o_ref[...] = o.astype(o_ref.dtype)


def _diff_attention(qkv, lam_rows, sub_g, lambda_init):
    assert TQ == TK
    n_q = SEQ // TQ
    head_w = 2 * HEAD_DIM
    return pl.pallas_call(
        functools.partial(_attn_kernel, lambda_init=lambda_init),
        grid=(BATCH, N_HEADS, n_q),
        in_specs=[
            pl.BlockSpec((SUBLANES, HEAD_DIM), lambda b, h, i: (0, 0)),
            pl.BlockSpec((TQ, head_w), lambda b, h, i: (b * n_q + i, h)),
            pl.BlockSpec((SEQ, head_w), lambda b, h, i: (b, N_HEADS + h)),
            pl.BlockSpec((SEQ, head_w), lambda b, h, i: (b, 2 * N_HEADS + h)),
            pl.BlockSpec((1, head_w), lambda b, h, i: (0, 0)),
        ],
        out_specs=pl.BlockSpec((TQ, head_w), lambda b, h, i: (b * n_q + i, h)),
        out_shape=jax.ShapeDtypeStruct((N_TOK, D_MODEL), BF16),
        scratch_shapes=[pltpu.VMEM((2, TQ, LANES), F32),
                        pltpu.VMEM((2, TQ, LANES), F32),
                        pltpu.VMEM((2, TQ, head_w), F32)],
        compiler_params=pltpu.CompilerParams(
            dimension_semantics=("arbitrary", "arbitrary", "arbitrary"),
            vmem_limit_bytes=VMEM_LIMIT),
        name="diff_attention",
    )(lam_rows, qkv, qkv, qkv, sub_g)


def _router_params(wr_g, br_g, wr_e, br_e):
    w = jnp.zeros((D_MODEL, ROUTER_COLS), F32)
    w = w.at[:, 0:N_GROUPS].set(wr_g).at[:, E_COL0:E_COL0 + N_EXPERTS].set(wr_e)
    b = jnp.full((1, ROUTER_COLS), NEG_INF, F32)
    b = b.at[0, 0:N_GROUPS].set(br_g).at[0, E_COL0:E_COL0 + N_EXPERTS].set(br_e)
    w_hi = w.astype(BF16)
    w_lo = (w - w_hi.astype(F32)).astype(BF16)
    return jnp.stack([w_hi, w_lo]), b


def kernel(x, norm_mix_g, norm_ffn_g, pool_w, pool_scale, attn_wq, attn_wk, attn_wv, attn_wo,
           lam_q1, lam_k1, lam_q2, lam_k2, attn_sub_g, router_g_w, router_g_b, router_e_w,
           router_e_b, exp_w_gate, exp_w_up, exp_w_down, final_g):
    x2d = x.reshape(N_TOK, D_MODEL)
    row = lambda v: v.reshape(1, -1)

    wr0, br0 = _router_params(router_g_w[0], router_g_b[0], router_e_w[0], router_e_b[0])
    x1, h, ids, wts = _pool_layer(x2d, row(norm_mix_g[0]), pool_w[0].astype(BF16),
                                  row(pool_scale[0]), row(norm_ffn_g[0]), wr0, br0)
    yg = _moe_grouped(0, _route_plan(ids, wts), h, exp_w_gate, exp_w_up, exp_w_down)
    x2, hn = _combine_norm(x1, yg, row(norm_mix_g[1]), emit_sum=True, norm_dtype=BF16)

    lambda_init = 0.8 - 0.6 * math.exp(-0.3 * 1)
    w_qkv = jnp.concatenate([attn_wq[0], attn_wk[0], attn_wv[0]], axis=1).astype(BF16)
    q_scale = HEAD_DIM ** -0.5 * math.log2(math.e)
    qkv = _matmul(hn, w_qkv, BF16, D_MODEL, first_block_scale=q_scale)
    lam_rows = jnp.zeros((SUBLANES, HEAD_DIM), F32)
    lam_rows = lam_rows.at[0].set(lam_q1[0]).at[1].set(lam_k1[0])
    lam_rows = lam_rows.at[2].set(lam_q2[0]).at[3].set(lam_k2[0])
    att = _diff_attention(qkv, lam_rows, row(attn_sub_g[0]), lambda_init)
    proj = _matmul(att, attn_wo[0].astype(BF16), F32, D_MODEL)
    wr1, br1 = _router_params(router_g_w[1], router_g_b[1], router_e_w[1], router_e_b[1])
    x3, h, ids, wts = _add_router(x2, proj, row(norm_ffn_g[1]), wr1, br1)
    yg = _moe_grouped(1, _route_plan(ids, wts), h, exp_w_gate, exp_w_up, exp_w_down)
    (out,) = _combine_norm(x3, yg, row(final_g), emit_sum=False, norm_dtype=F32)
    return out.reshape(BATCH, SEQ, D_MODEL)
```

```python
import functools
import math

import jax
import jax.numpy as jnp
from jax import lax
from jax.experimental import pallas as pl
from jax.experimental.pallas import tpu as pltpu

F32 = jnp.float32
BF16 = jnp.bfloat16
I32 = jnp.int32

D_MODEL = 2048
BATCH = 8
SEQ = 2048
N_TOK = BATCH * SEQ
CHUNK = 64
POOL_WINDOWS = (2, 4, 8, 16)
POOL_CG = D_MODEL // len(POOL_WINDOWS)
POOL_HALO = 16
N_HEADS = 8
HEAD_DIM = 128
N_GROUPS = 4
EXPERTS_PER_GROUP = 8
N_EXPERTS = 32
EXPERT_FF = 512
NORM_EPS = 1e-6
SUBLN_EPS = 1e-5
NEG_INF = -1e30

SUBLANES = 8
LANES = 128
RPT = D_MODEL // LANES
RPT_PAD = RPT + 1
MXU_K = 256
ROUTER_COLS = LANES
E_COL0 = 8

TS = 256
TM = 256
N_SORT = 2 * N_TOK + N_EXPERTS * TM
N_TILES = N_SORT // TM
TQ = 512
TK = 512
MM_TM = 512
VMEM_LIMIT = 56 * 1024 * 1024


def _rms(x, g, eps):
    ms = jnp.mean(x * x, axis=-1, keepdims=True)
    return x * lax.rsqrt(ms + eps) * g


def _to_rows(ref, val, n, pitch=RPT):
    for c in range(RPT):
        ref[pl.ds(c, n, stride=pitch), :] = val[:, c * LANES:(c + 1) * LANES]


def _from_rows(ref, n, lo=0, hi=RPT, pitch=RPT):
    return jnp.concatenate(
        [ref[pl.ds(c, n, stride=pitch), :] for c in range(lo, hi)], axis=-1)


def _router_tail(x_new, gffn_ref, wr_ref, br_ref, h_ref, ids_ref, wts_ref):
    h = _rms(x_new, gffn_ref[...], NORM_EPS)
    _to_rows(h_ref, h, x_new.shape[0])
    h_hi = h.astype(BF16)
    h_lo = (h - h_hi.astype(F32)).astype(BF16)
    w_hi = wr_ref[0]
    w_lo = wr_ref[1]
    logits = (jnp.dot(h_hi, w_hi, preferred_element_type=F32)
              + jnp.dot(h_hi, w_lo, preferred_element_type=F32)
              + jnp.dot(h_lo, w_hi, preferred_element_type=F32))
    logits = logits + br_ref[...]
    lt = logits.T
    gl = lt[0:SUBLANES, :]
    io8 = lax.broadcasted_iota(I32, gl.shape, 0)
    gmax = jnp.max(gl, axis=0, keepdims=True)
    gtop = jnp.min(jnp.where(gl == gmax, io8, SUBLANES), axis=0, keepdims=True)
    p_g = 1.0 / jnp.sum(jnp.exp(gl - gmax), axis=0, keepdims=True)
    esel = lt[E_COL0:E_COL0 + EXPERTS_PER_GROUP, :]
    for gi in range(1, N_GROUPS):
        lo = E_COL0 + gi * EXPERTS_PER_GROUP
        esel = jnp.where(gtop == gi, lt[lo:lo + EXPERTS_PER_GROUP, :], esel)
    m1 = jnp.max(esel, axis=0, keepdims=True)
    i1 = jnp.min(jnp.where(esel == m1, io8, SUBLANES), axis=0, keepdims=True)
    rest = jnp.where(io8 == i1, -jnp.inf, esel)
    m2 = jnp.max(rest, axis=0, keepdims=True)
    i2 = jnp.min(jnp.where(rest == m2, io8, SUBLANES), axis=0, keepdims=True)
    r = jnp.exp(m2 - m1)
    v1 = 1.0 / (1.0 + r)
    v2 = r / (1.0 + r)
    base = gtop * EXPERTS_PER_GROUP
    ids_ref[0:1, :] = base + i1
    ids_ref[1:2, :] = base + i2
    wts_ref[0:1, :] = p_g * v1
    wts_ref[1:2, :] = p_g * v2


def _router_out_shapes():
    return (jax.ShapeDtypeStruct((N_TOK * RPT, LANES), F32),
            jax.ShapeDtypeStruct((2, N_TOK), I32),
            jax.ShapeDtypeStruct((2, N_TOK), F32))


def _router_out_specs(tok_map):
    return (pl.BlockSpec((TS * RPT, LANES), lambda *a: (tok_map(*a), 0)),
            pl.BlockSpec((2, TS), lambda *a: (0, tok_map(*a))),
            pl.BlockSpec((2, TS), lambda *a: (0, tok_map(*a))))


def _pool_kernel(x_ref, halo_ref, gmix_ref, wpool_ref, pscale_ref, gffn_ref, wr_ref, br_ref,
                 x1_ref, h_ref, ids_ref, wts_ref, hs_ref):
    s = pl.program_id(1)
    xt = x_ref[...]
    g = gmix_ref[...]
    h = _rms(xt, g, NORM_EPS)
    hh = _rms(halo_ref[...], g, NORM_EPS)
    hs_ref[0:POOL_HALO, :] = jnp.where(s > 0, hh, 0.0)
    hs_ref[POOL_HALO:, :] = h
    pos = s * TS + lax.broadcasted_iota(I32, (TS, 1), 0)
    outs = []
    for gi, w in enumerate(POOL_WINDOWS):
        c0 = gi * POOL_CG
        he = hs_ref[:, c0:c0 + POOL_CG]
        acc = he
        sh = 1
        while sh < w:
            acc = acc + pltpu.roll(acc, sh, axis=0)
            sh *= 2
        inv_cnt = 1.0 / jnp.minimum(pos + 1, w).astype(F32)
        mix = acc[POOL_HALO:, :] * inv_cnt - he[POOL_HALO:, :]
        outs.append(jnp.dot(mix.astype(BF16), wpool_ref[gi], preferred_element_type=F32))
    x1 = xt + jnp.concatenate(outs, axis=-1) * pscale_ref[...]
    x1_ref[...] = x1
    _router_tail(x1, gffn_ref, wr_ref, br_ref, h_ref, ids_ref, wts_ref)


def _pool_layer(x2d, gmix, wpool_bf, pscale, gffn, wr2, br):
    n_s = SEQ // TS
    tok_map = lambda b, s: b * n_s + s
    halo_blocks = TS // POOL_HALO
    const2 = lambda b, s: (0, 0)
    return pl.pallas_call(
        _pool_kernel,
        grid=(BATCH, n_s),
        in_specs=[
            pl.BlockSpec((TS, D_MODEL), lambda b, s: (tok_map(b, s), 0)),
            pl.BlockSpec((POOL_HALO, D_MODEL),
                         lambda b, s: (jnp.maximum(tok_map(b, s) * halo_blocks - 1, 0), 0)),
            pl.BlockSpec((1, D_MODEL), const2),
            pl.BlockSpec((len(POOL_WINDOWS), POOL_CG, POOL_CG), lambda b, s: (0, 0, 0)),
            pl.BlockSpec((1, D_MODEL), const2),
            pl.BlockSpec((1, D_MODEL), const2),
            pl.BlockSpec((2, D_MODEL, ROUTER_COLS), lambda b, s: (0, 0, 0)),
            pl.BlockSpec((1, ROUTER_COLS), const2),
        ],
        out_specs=(pl.BlockSpec((TS, D_MODEL), lambda b, s: (tok_map(b, s), 0)),)
        + _router_out_specs(tok_map),
        out_shape=(jax.ShapeDtypeStruct((N_TOK, D_MODEL), F32),) + _router_out_shapes(),
        scratch_shapes=[pltpu.VMEM((TS + POOL_HALO, D_MODEL), F32)],
        compiler_params=pltpu.CompilerParams(
            dimension_semantics=("arbitrary", "arbitrary"), vmem_limit_bytes=VMEM_LIMIT),
        name="pool_mixer_router",
    )(x2d, x2d, gmix, wpool_bf, pscale, gffn, wr2, br)


def _proj_router_kernel(x_ref, a_ref, wo_ref, gffn_ref, wr_ref, br_ref,
                        xo_ref, h_ref, ids_ref, wts_ref):
    x_new = x_ref[...] + jnp.dot(a_ref[...], wo_ref[...], preferred_element_type=F32)
    xo_ref[...] = x_new
    _router_tail(x_new, gffn_ref, wr_ref, br_ref, h_ref, ids_ref, wts_ref)


def _proj_router(x2d, att, wo_bf, gffn, wr2, br):
    tok_map = lambda i: i
    const2 = lambda i: (0, 0)
    return pl.pallas_call(
        _proj_router_kernel,
        grid=(N_TOK // TS,),
        in_specs=[
            pl.BlockSpec((TS, D_MODEL), lambda i: (i, 0)),
            pl.BlockSpec((TS, D_MODEL), lambda i: (i, 0)),
            pl.BlockSpec((D_MODEL, D_MODEL), const2),
            pl.BlockSpec((1, D_MODEL), const2),
            pl.BlockSpec((2, D_MODEL, ROUTER_COLS), lambda i: (0, 0, 0)),
            pl.BlockSpec((1, ROUTER_COLS), const2),
        ],
        out_specs=(pl.BlockSpec((TS, D_MODEL), lambda i: (i, 0)),) + _router_out_specs(tok_map),
        out_shape=(jax.ShapeDtypeStruct((N_TOK, D_MODEL), F32),) + _router_out_shapes(),
        compiler_params=pltpu.CompilerParams(
            dimension_semantics=("arbitrary",), vmem_limit_bytes=VMEM_LIMIT),
        name="proj_router",
    )(x2d, att, wo_bf, gffn, wr2, br)


def _moe_kernel(te_ref, nv_ref, src0_ref, srcn_ref, dst_ref, h_ref, wg_ref, wu_ref, wd_ref,
                gate_ref, yg_ref, xbuf, ybuf, wg_bf, wu_bf, wd_bf, sem_g, sem_s):
    i = pl.program_id(0)
    nv = nv_ref[0]
    slot = i % 2
    other = 1 - slot

    def hbm_token(ref, t):
        return ref.at[pl.ds(pl.multiple_of(t * RPT, RPT), RPT), :]

    def gather_start(idx_ref, s, r):
        pltpu.make_async_copy(hbm_token(h_ref, idx_ref[0, 0, r]),
                              xbuf.at[s, pl.ds(r * RPT_PAD, RPT), :], sem_g.at[s]).start()

    def gather_wait(s):
        pltpu.make_async_copy(h_ref.at[pl.ds(0, TM * RPT), :],
                              xbuf.at[s, pl.ds(0, TM * RPT), :], sem_g.at[s]).wait()

    def scatter_start(s, r):
        pltpu.make_async_copy(ybuf.at[s, pl.ds(r * RPT_PAD, RPT), :],
                              hbm_token(yg_ref, dst_ref[0, 0, r]), sem_s.at[s]).start()

    def scatter_wait(s):
        pltpu.make_async_copy(ybuf.at[s, pl.ds(0, TM * RPT), :],
                              yg_ref.at[pl.ds(0, TM * RPT), :], sem_s.at[s]).wait()

    @pl.when(i == 0)
    def _():
        for r in range(TM):
            gather_start(src0_ref, 0, r)

    @pl.when(i >= 2)
    def _():
        scatter_wait(slot)

    @pl.when(i < nv)
    def _():
        e = te_ref[i]
        e_prev = te_ref[jnp.maximum(i - 1, 0)]

        @pl.when((i == 0) | (e != e_prev))
        def _():
            wg_bf[...] = wg_ref[0, 0].astype(BF16)
            wu_bf[...] = wu_ref[0, 0].astype(BF16)
            wd_bf[...] = wd_ref[0, 0].astype(BF16)

        gather_wait(slot)
        for r in range(TM):
            gather_start(srcn_ref, other, r)
        g = jnp.zeros((TM, EXPERT_FF), F32)
        u = jnp.zeros((TM, EXPERT_FF), F32)
        per_pass = MXU_K // LANES
        for c in range(D_MODEL // MXU_K):
            xk = _from_rows(xbuf.at[slot], TM, c * per_pass, (c + 1) * per_pass,
                            pitch=RPT_PAD).astype(BF16)
            g = g + jnp.dot(xk, wg_bf[c * MXU_K:(c + 1) * MXU_K, :], preferred_element_type=F32)
            u = u + jnp.dot(xk, wu_bf[c * MXU_K:(c + 1) * MXU_K, :], preferred_element_type=F32)
        hid = g / (1.0 + jnp.exp(-g)) * u
        y = jnp.dot(hid.astype(BF16), wd_bf[...], preferred_element_type=F32)
        _to_rows(ybuf.at[slot], y * gate_ref[...], TM, pitch=RPT_PAD)

    for r in range(TM):
        scatter_start(slot, r)

    @pl.when(i == nv - 1)
    def _():
        gather_wait(other)

    @pl.when(i == N_TILES - 1)
    def _():
        scatter_wait(slot)
        scatter_wait(other)


def _moe_grouped(layer, plan, h, w_gate, w_up, w_down):
    tile_expert, n_valid, src_tok, dst_row, gate_col = plan
    last = lambda nv: nv[0] - 1
    idx_blk = lambda f: pl.BlockSpec((1, 1, TM), f, memory_space=pltpu.SMEM)
    w_spec = lambda shape: pl.BlockSpec(
        (1, 1) + shape, lambda i, te, nv: (layer, te[jnp.minimum(i, last(nv))], 0, 0))
    grid_spec = pltpu.PrefetchScalarGridSpec(
        num_scalar_prefetch=2,
        grid=(N_TILES,),
        in_specs=[
            idx_blk(lambda i, te, nv: (0, 0, 0)),
            idx_blk(lambda i, te, nv: (jnp.minimum(i + 1, last(nv)), 0, 0)),
            idx_blk(lambda i, te, nv: (i, 0, 0)),
            pl.BlockSpec(memory_space=pl.ANY),
            w_spec((D_MODEL, EXPERT_FF)),
            w_spec((D_MODEL, EXPERT_FF)),
            w_spec((EXPERT_FF, D_MODEL)),
            pl.BlockSpec((TM, 1), lambda i, te, nv: (jnp.minimum(i, last(nv)), 0)),
        ],
        out_specs=pl.BlockSpec(memory_space=pl.ANY),
        scratch_shapes=[pltpu.VMEM((2, TM * RPT_PAD, LANES), F32),
                        pltpu.VMEM((2, TM * RPT_PAD, LANES), F32),
                        pltpu.VMEM((D_MODEL, EXPERT_FF), BF16),
                        pltpu.VMEM((D_MODEL, EXPERT_FF), BF16),
                        pltpu.VMEM((EXPERT_FF, D_MODEL), BF16),
                        pltpu.SemaphoreType.DMA((2,)),
                        pltpu.SemaphoreType.DMA((2,))],
    )
    idx3 = lambda v: v.reshape(N_TILES, 1, TM)
    return pl.pallas_call(
        _moe_kernel,
        grid_spec=grid_spec,
        out_shape=jax.ShapeDtypeStruct((N_SORT * RPT, LANES), F32),
        compiler_params=pltpu.CompilerParams(
            dimension_semantics=("arbitrary",), vmem_limit_bytes=VMEM_LIMIT),
        name="moe_grouped",
    )(tile_expert, n_valid, idx3(src_tok), idx3(src_tok), idx3(dst_row), h,
      w_gate, w_up, w_down, gate_col)


def _route_plan(ids, wts):
    e = ids.reshape(-1)
    w = wts.reshape(-1)
    onehot = (e[:, None] == jnp.arange(N_EXPERTS, dtype=I32)[None, :]).astype(I32)
    csum = jnp.cumsum(onehot, axis=0)
    rank = jnp.sum(onehot * csum, axis=1) - 1
    counts = csum[-1]
    padded = (counts + TM - 1) // TM * TM
    ends = jnp.cumsum(padded)
    pos = (ends - padded)[e] + rank
    pair = jnp.full((N_SORT,), -1, I32).at[pos].set(jnp.arange(2 * N_TOK, dtype=I32))
    is_pad = pair < 0
    pad_row = 2 * N_TOK + jnp.cumsum(is_pad.astype(I32)) - 1
    src_tok = jnp.where(is_pad, 0, pair % N_TOK)
    gate = jnp.where(is_pad, 0.0, w[jnp.maximum(pair, 0)])
    dst_row = jnp.where(is_pad, pad_row, pair)
    n_valid = (ends[-1] // TM).astype(I32)
    tile_start = jnp.minimum(jnp.arange(N_TILES, dtype=I32), n_valid - 1) * TM
    n_done = jnp.sum((ends[None, :] <= tile_start[:, None]).astype(I32), axis=1)
    tile_expert = jnp.minimum(n_done, N_EXPERTS - 1)
    return tile_expert, n_valid.reshape(1), src_tok, dst_row, gate.reshape(N_SORT, 1)


def _combine_norm_kernel(x_ref, y0_ref, y1_ref, g_ref, *out_refs, emit_sum):
    x_new = x_ref[...] + _from_rows(y0_ref, TS) + _from_rows(y1_ref, TS)
    normed = _rms(x_new, g_ref[...], NORM_EPS)
    if emit_sum:
        out_refs[0][...] = x_new
        out_refs[1][...] = normed.astype(out_refs[1].dtype)
    else:
        out_refs[0][...] = normed.astype(out_refs[0].dtype)


def _combine_norm(x2d, yg, g, *, emit_sum, norm_dtype):
    n_blk = N_TOK // TS
    tok_spec = pl.BlockSpec((TS, D_MODEL), lambda i: (i, 0))
    out_specs = [tok_spec]
    out_shape = [jax.ShapeDtypeStruct((N_TOK, D_MODEL), norm_dtype)]
    if emit_sum:
        out_specs = [tok_spec, tok_spec]
        out_shape = [jax.ShapeDtypeStruct((N_TOK, D_MODEL), F32)] + out_shape
    return pl.pallas_call(
        functools.partial(_combine_norm_kernel, emit_sum=emit_sum),
        grid=(n_blk,),
        in_specs=[
            tok_spec,
            pl.BlockSpec((TS * RPT, LANES), lambda i: (i, 0)),
            pl.BlockSpec((TS * RPT, LANES), lambda i: (n_blk + i, 0)),
            pl.BlockSpec((1, D_MODEL), lambda i: (0, 0)),
        ],
        out_specs=tuple(out_specs),
        out_shape=tuple(out_shape),
        compiler_params=pltpu.CompilerParams(
            dimension_semantics=("arbitrary",), vmem_limit_bytes=VMEM_LIMIT),
        name="combine_norm",
    )(x2d, yg, yg, g)


def _mm_kernel(a_ref, b_ref, o_ref, *, first_block_scale):
    acc = jnp.dot(a_ref[...], b_ref[...], preferred_element_type=F32)
    if first_block_scale is not None:
        acc = acc * jnp.where(pl.program_id(0) == 0, first_block_scale, 1.0)
    o_ref[...] = acc.astype(o_ref.dtype)


def _matmul(a, b, out_dtype, tn, first_block_scale=None):
    m, k = a.shape
    n = b.shape[1]
    return pl.pallas_call(
        functools.partial(_mm_kernel, first_block_scale=first_block_scale),
        grid=(n // tn, m // MM_TM),
        in_specs=[pl.BlockSpec((MM_TM, k), lambda j, i: (i, 0)),
                  pl.BlockSpec((k, tn), lambda j, i: (0, j))],
        out_specs=pl.BlockSpec((MM_TM, tn), lambda j, i: (i, j)),
        out_shape=jax.ShapeDtypeStruct((m, n), out_dtype),
        compiler_params=pltpu.CompilerParams(
            dimension_semantics=("arbitrary", "arbitrary"), vmem_limit_bytes=VMEM_LIMIT),
        name="dense_matmul",
    )(a, b)


def _attn_kernel(lam_ref, q_ref, k_ref, v_ref, subg_ref, o_ref,
                 m_ref, l_ref, acc_ref, *, lambda_init):
    qi = pl.program_id(2)
    q = q_ref[...]
    qs = (q[:, :HEAD_DIM], q[:, HEAD_DIM:])
    m_ref[...] = jnp.full(m_ref.shape, NEG_INF, F32)
    l_ref[...] = jnp.zeros(l_ref.shape, F32)
    acc_ref[...] = jnp.zeros(acc_ref.shape, F32)
    lane_rep = lambda a, width: jnp.concatenate([a] * (width // LANES), axis=1)

    def tile_update(j, mask):
        k0 = pl.multiple_of(j * TK, TK)
        kt = k_ref[pl.ds(k0, TK), :]
        vt = v_ref[pl.ds(k0, TK), :]
        for c in range(2):
            kc = kt[:, c * HEAD_DIM:(c + 1) * HEAD_DIM]
            s = lax.dot_general(qs[c], kc, (((1,), (1,)), ((), ())),
                                preferred_element_type=F32)
            if mask is not None:
                s = jnp.where(mask, s, NEG_INF)
            m_old = m_ref[c]
            m_new = jnp.maximum(m_old, jnp.max(s, axis=-1, keepdims=True))
            alpha = jnp.exp2(m_old - m_new)
            p = jnp.exp2(s - lane_rep(m_new, TK))
            l_ref[c] = alpha * l_ref[c] + jnp.sum(p, axis=-1, keepdims=True)
            acc_ref[c] = lane_rep(alpha, 2 * HEAD_DIM) * acc_ref[c] + jnp.dot(
                p.astype(BF16), vt, preferred_element_type=F32)
            m_ref[c] = m_new

    def full_body(j, c):
        tile_update(j, None)
        return c

    lax.fori_loop(0, qi, full_body, 0)
    row_chunk = lax.broadcasted_iota(I32, (TQ, TK), 0) // CHUNK
    col_chunk = lax.broadcasted_iota(I32, (TQ, TK), 1) // CHUNK
    tile_update(qi, col_chunk <= row_chunk)

    lv = lam_ref[...]
    lam = (jnp.exp(jnp.sum(lv[0:1] * lv[1:2], axis=-1, keepdims=True))
           - jnp.exp(jnp.sum(lv[2:3] * lv[3:4], axis=-1, keepdims=True)) + lambda_init)
    width = 2 * HEAD_DIM
    o = (acc_ref[0] / lane_rep(l_ref[0], width)
         - lam * (acc_ref[1] / lane_rep(l_ref[1], width)))
    o = _rms(o, subg_ref[...], SUBLN_EPS) * (1.0 - lambda_init)
    o_ref[...] = o.astype(o_ref.dtype)


def _diff_attention(qkv, lam_rows, sub_g, lambda_init):
    assert TQ == TK
    n_q = SEQ // TQ
    head_w = 2 * HEAD_DIM
    return pl.pallas_call(
        functools.partial(_attn_kernel, lambda_init=lambda_init),
        grid=(BATCH, N_HEADS, n_q),
        in_specs=[
            pl.BlockSpec((SUBLANES, HEAD_DIM), lambda b, h, i: (0, 0)),
            pl.BlockSpec((TQ, head_w), lambda b, h, i: (b * n_q + i, h)),
            pl.BlockSpec((SEQ, head_w), lambda b, h, i: (b, N_HEADS + h)),
            pl.BlockSpec((SEQ, head_w), lambda b, h, i: (b, 2 * N_HEADS + h)),
            pl.BlockSpec((1, head_w), lambda b, h, i: (0, 0)),
        ],
        out_specs=pl.BlockSpec((TQ, head_w), lambda b, h, i: (b * n_q + i, h)),
        out_shape=jax.ShapeDtypeStruct((N_TOK, D_MODEL), BF16),
        scratch_shapes=[pltpu.VMEM((2, TQ, LANES), F32),
                        pltpu.VMEM((2, TQ, LANES), F32),
                        pltpu.VMEM((2, TQ, head_w), F32)],
        compiler_params=pltpu.CompilerParams(
            dimension_semantics=("arbitrary", "arbitrary", "arbitrary"),
            vmem_limit_bytes=VMEM_LIMIT),
        name="diff_attention",
    )(lam_rows, qkv, qkv, qkv, sub_g)


def _router_params(wr_g, br_g, wr_e, br_e):
    w = jnp.zeros((D_MODEL, ROUTER_COLS), F32)
    w = w.at[:, 0:N_GROUPS].set(wr_g).at[:, E_COL0:E_COL0 + N_EXPERTS].set(wr_e)
    b = jnp.full((1, ROUTER_COLS), NEG_INF, F32)
    b = b.at[0, 0:N_GROUPS].set(br_g).at[0, E_COL0:E_COL0 + N_EXPERTS].set(br_e)
    w_hi = w.astype(BF16)
    w_lo = (w - w_hi.astype(F32)).astype(BF16)
    return jnp.stack([w_hi, w_lo]), b


def kernel(x, norm_mix_g, norm_ffn_g, pool_w, pool_scale, attn_wq, attn_wk, attn_wv, attn_wo,
           lam_q1, lam_k1, lam_q2, lam_k2, attn_sub_g, router_g_w, router_g_b, router_e_w,
           router_e_b, exp_w_gate, exp_w_up, exp_w_down, final_g):
    x2d = x.reshape(N_TOK, D_MODEL)
    row = lambda v: v.reshape(1, -1)

    wr0, br0 = _router_params(router_g_w[0], router_g_b[0], router_e_w[0], router_e_b[0])
    x1, h, ids, wts = _pool_layer(x2d, row(norm_mix_g[0]), pool_w[0].astype(BF16),
                                  row(pool_scale[0]), row(norm_ffn_g[0]), wr0, br0)
    yg = _moe_grouped(0, _route_plan(ids, wts), h, exp_w_gate, exp_w_up, exp_w_down)
    x2, hn = _combine_norm(x1, yg, row(norm_mix_g[1]), emit_sum=True, norm_dtype=BF16)

    lambda_init = 0.8 - 0.6 * math.exp(-0.3 * 1)
    w_qkv = jnp.concatenate([attn_wq[0], attn_wk[0], attn_wv[0]], axis=1).astype(BF16)
    q_scale = HEAD_DIM ** -0.5 * math.log2(math.e)
    qkv = _matmul(hn, w_qkv, BF16, D_MODEL, first_block_scale=q_scale)
    lam_rows = jnp.zeros((SUBLANES, HEAD_DIM), F32)
    lam_rows = lam_rows.at[0].set(lam_q1[0]).at[1].set(lam_k1[0])
    lam_rows = lam_rows.at[2].set(lam_q2[0]).at[3].set(lam_k2[0])
    att = _diff_attention(qkv, lam_rows, row(attn_sub_g[0]), lambda_init)
    wr1, br1 = _router_params(router_g_w[1], router_g_b[1], router_e_w[1], router_e_b[1])
    x3, h, ids, wts = _proj_router(x2, att, attn_wo[0].astype(BF16), row(norm_ffn_g[1]),
                                   wr1, br1)
    yg = _moe_grouped(1, _route_plan(ids, wts), h, exp_w_gate, exp_w_up, exp_w_down)
    (out,) = _combine_norm(x3, yg, row(final_g), emit_sum=False, norm_dtype=F32)
    return out.reshape(BATCH, SEQ, D_MODEL)
```

```python
import functools
import math

import jax
import jax.numpy as jnp
from jax import lax
from jax.experimental import pallas as pl
from jax.experimental.pallas import tpu as pltpu

F32 = jnp.float32
BF16 = jnp.bfloat16
I32 = jnp.int32

D_MODEL = 2048
BATCH = 8
SEQ = 2048
N_TOK = BATCH * SEQ
CHUNK = 64
POOL_WINDOWS = (2, 4, 8, 16)
POOL_CG = D_MODEL // len(POOL_WINDOWS)
POOL_HALO = 16
N_HEADS = 8
HEAD_DIM = 128
N_GROUPS = 4
EXPERTS_PER_GROUP = 8
N_EXPERTS = 32
EXPERT_FF = 512
NORM_EPS = 1e-6
SUBLN_EPS = 1e-5
NEG_INF = -1e30

SUBLANES = 8
LANES = 128
RPT = D_MODEL // LANES
RPT_PAD = RPT + 1
MXU_K = 256
ROUTER_COLS = LANES
E_COL0 = 8

TS = 256
TM = 256
N_SORT = 2 * N_TOK + N_EXPERTS * TM
N_TILES = N_SORT // TM
TQ = 512
TK = 512
MM_TM = 512
VMEM_LIMIT = 56 * 1024 * 1024


def _rms(x, g, eps):
    ms = jnp.mean(x * x, axis=-1, keepdims=True)
    return x * lax.rsqrt(ms + eps) * g


def _to_rows(ref, val, n, pitch=RPT):
    for c in range(RPT):
        ref[pl.ds(c, n, stride=pitch), :] = val[:, c * LANES:(c + 1) * LANES]


def _from_rows(ref, n, lo=0, hi=RPT, pitch=RPT):
    return jnp.concatenate(
        [ref[pl.ds(c, n, stride=pitch), :] for c in range(lo, hi)], axis=-1)


def _router_tail(x_new, gffn_ref, wr_ref, br_ref, h_ref, ids_ref, wts_ref):
    h = _rms(x_new, gffn_ref[...], NORM_EPS)
    _to_rows(h_ref, h, x_new.shape[0])
    h_hi = h.astype(BF16)
    h_lo = (h - h_hi.astype(F32)).astype(BF16)
    w_hi = wr_ref[0]
    w_lo = wr_ref[1]
    logits = (jnp.dot(h_hi, w_hi, preferred_element_type=F32)
              + jnp.dot(h_hi, w_lo, preferred_element_type=F32)
              + jnp.dot(h_lo, w_hi, preferred_element_type=F32))
    logits = logits + br_ref[...]
    lt = logits.T
    gl = lt[0:SUBLANES, :]
    io8 = lax.broadcasted_iota(I32, gl.shape, 0)
    gmax = jnp.max(gl, axis=0, keepdims=True)
    gtop = jnp.min(jnp.where(gl == gmax, io8, SUBLANES), axis=0, keepdims=True)
    p_g = 1.0 / jnp.sum(jnp.exp(gl - gmax), axis=0, keepdims=True)
    esel = lt[E_COL0:E_COL0 + EXPERTS_PER_GROUP, :]
    for gi in range(1, N_GROUPS):
        lo = E_COL0 + gi * EXPERTS_PER_GROUP
        esel = jnp.where(gtop == gi, lt[lo:lo + EXPERTS_PER_GROUP, :], esel)
    m1 = jnp.max(esel, axis=0, keepdims=True)
    i1 = jnp.min(jnp.where(esel == m1, io8, SUBLANES), axis=0, keepdims=True)
    rest = jnp.where(io8 == i1, -jnp.inf, esel)
    m2 = jnp.max(rest, axis=0, keepdims=True)
    i2 = jnp.min(jnp.where(rest == m2, io8, SUBLANES), axis=0, keepdims=True)
    r = jnp.exp(m2 - m1)
    v1 = 1.0 / (1.0 + r)
    v2 = r / (1.0 + r)
    base = gtop * EXPERTS_PER_GROUP
    ids_ref[0:1, :] = base + i1
    ids_ref[1:2, :] = base + i2
    wts_ref[0:1, :] = p_g * v1
    wts_ref[1:2, :] = p_g * v2


def _router_out_shapes():
    return (jax.ShapeDtypeStruct((N_TOK * RPT, LANES), F32),
            jax.ShapeDtypeStruct((2, N_TOK), I32),
            jax.ShapeDtypeStruct((2, N_TOK), F32))


def _router_out_specs(tok_map):
    return (pl.BlockSpec((TS * RPT, LANES), lambda *a: (tok_map(*a), 0)),
            pl.BlockSpec((2, TS), lambda *a: (0, tok_map(*a))),
            pl.BlockSpec((2, TS), lambda *a: (0, tok_map(*a))))


def _pool_kernel(x_ref, halo_ref, gmix_ref, wpool_ref, pscale_ref, gffn_ref, wr_ref, br_ref,
                 x1_ref, h_ref, ids_ref, wts_ref, hs_ref):
    s = pl.program_id(1)
    xt = x_ref[...]
    g = gmix_ref[...]
    h = _rms(xt, g, NORM_EPS)
    hh = _rms(halo_ref[...], g, NORM_EPS)
    hs_ref[0:POOL_HALO, :] = jnp.where(s > 0, hh, 0.0)
    hs_ref[POOL_HALO:, :] = h
    pos = s * TS + lax.broadcasted_iota(I32, (TS, 1), 0)
    outs = []
    for gi, w in enumerate(POOL_WINDOWS):
        c0 = gi * POOL_CG
        he = hs_ref[:, c0:c0 + POOL_CG]
        acc = he
        sh = 1
        while sh < w:
            acc = acc + pltpu.roll(acc, sh, axis=0)
            sh *= 2
        inv_cnt = 1.0 / jnp.minimum(pos + 1, w).astype(F32)
        mix = acc[POOL_HALO:, :] * inv_cnt - he[POOL_HALO:, :]
        outs.append(jnp.dot(mix.astype(BF16), wpool_ref[gi], preferred_element_type=F32))
    x1 = xt + jnp.concatenate(outs, axis=-1) * pscale_ref[...]
    x1_ref[...] = x1
    _router_tail(x1, gffn_ref, wr_ref, br_ref, h_ref, ids_ref, wts_ref)


def _pool_layer(x2d, gmix, wpool_bf, pscale, gffn, wr2, br):
    n_s = SEQ // TS
    tok_map = lambda b, s: b * n_s + s
    halo_blocks = TS // POOL_HALO
    const2 = lambda b, s: (0, 0)
    return pl.pallas_call(
        _pool_kernel,
        grid=(BATCH, n_s),
        in_specs=[
            pl.BlockSpec((TS, D_MODEL), lambda b, s: (tok_map(b, s), 0)),
            pl.BlockSpec((POOL_HALO, D_MODEL),
                         lambda b, s: (jnp.maximum(tok_map(b, s) * halo_blocks - 1, 0), 0)),
            pl.BlockSpec((1, D_MODEL), const2),
            pl.BlockSpec((len(POOL_WINDOWS), POOL_CG, POOL_CG), lambda b, s: (0, 0, 0)),
            pl.BlockSpec((1, D_MODEL), const2),
            pl.BlockSpec((1, D_MODEL), const2),
            pl.BlockSpec((2, D_MODEL, ROUTER_COLS), lambda b, s: (0, 0, 0)),
            pl.BlockSpec((1, ROUTER_COLS), const2),
        ],
        out_specs=(pl.BlockSpec((TS, D_MODEL), lambda b, s: (tok_map(b, s), 0)),)
        + _router_out_specs(tok_map),
        out_shape=(jax.ShapeDtypeStruct((N_TOK, D_MODEL), F32),) + _router_out_shapes(),
        scratch_shapes=[pltpu.VMEM((TS + POOL_HALO, D_MODEL), F32)],
        compiler_params=pltpu.CompilerParams(
            dimension_semantics=("arbitrary", "arbitrary"), vmem_limit_bytes=VMEM_LIMIT),
        name="pool_mixer_router",
    )(x2d, x2d, gmix, wpool_bf, pscale, gffn, wr2, br)


def _proj_router_kernel(x_ref, a_ref, wo_ref, gffn_ref, wr_ref, br_ref,
                        xo_ref, h_ref, ids_ref, wts_ref):
    x_new = x_ref[...] + jnp.dot(a_ref[...], wo_ref[...], preferred_element_type=F32)
    xo_ref[...] = x_new
    _router_tail(x_new, gffn_ref, wr_ref, br_ref, h_ref, ids_ref, wts_ref)


def _proj_router(x2d, att, wo_bf, gffn, wr2, br):
    tok_map = lambda i: i
    const2 = lambda i: (0, 0)
    return pl.pallas_call(
        _proj_router_kernel,
        grid=(N_TOK // TS,),
        in_specs=[
            pl.BlockSpec((TS, D_MODEL), lambda i: (i, 0)),
            pl.BlockSpec((TS, D_MODEL), lambda i: (i, 0)),
            pl.BlockSpec((D_MODEL, D_MODEL), const2),
            pl.BlockSpec((1, D_MODEL), const2),
            pl.BlockSpec((2, D_MODEL, ROUTER_COLS), lambda i: (0, 0, 0)),
            pl.BlockSpec((1, ROUTER_COLS), const2),
        ],
        out_specs=(pl.BlockSpec((TS, D_MODEL), lambda i: (i, 0)),) + _router_out_specs(tok_map),
        out_shape=(jax.ShapeDtypeStruct((N_TOK, D_MODEL), F32),) + _router_out_shapes(),
        compiler_params=pltpu.CompilerParams(
            dimension_semantics=("arbitrary",), vmem_limit_bytes=VMEM_LIMIT),
        name="proj_router",
    )(x2d, att, wo_bf, gffn, wr2, br)


def _moe_kernel(te_ref, nv_ref, src0_ref, srcn_ref, dst_ref, h_ref, wg_ref, wu_ref, wd_ref,
                gate_ref, yg_ref, xbuf, ybuf, wg_bf, wu_bf, wd_bf, sem_g, sem_s):
    i = pl.program_id(0)
    nv = nv_ref[0]
    slot = i % 2
    other = 1 - slot

    def hbm_token(ref, t):
        return ref.at[pl.ds(pl.multiple_of(t * RPT, RPT), RPT), :]

    def gather_start(idx_ref, s, r):
        pltpu.make_async_copy(hbm_token(h_ref, idx_ref[0, 0, r]),
                              xbuf.at[s, pl.ds(r * RPT_PAD, RPT), :],
                              sem_g.at[s]).start(priority=1)

    def gather_wait(s):
        pltpu.make_async_copy(h_ref.at[pl.ds(0, TM * RPT), :],
                              xbuf.at[s, pl.ds(0, TM * RPT), :], sem_g.at[s]).wait()

    def scatter_start(s, r):
        pltpu.make_async_copy(ybuf.at[s, pl.ds(r * RPT_PAD, RPT), :],
                              hbm_token(yg_ref, dst_ref[0, 0, r]),
                              sem_s.at[s]).start(priority=r % 2)

    def scatter_wait(s):
        pltpu.make_async_copy(ybuf.at[s, pl.ds(0, TM * RPT), :],
                              yg_ref.at[pl.ds(0, TM * RPT), :], sem_s.at[s]).wait()

    @pl.when(i == 0)
    def _():
        for r in range(TM):
            gather_start(src0_ref, 0, r)

    @pl.when(i >= 2)
    def _():
        scatter_wait(slot)

    @pl.when(i < nv)
    def _():
        e = te_ref[i]
        e_prev = te_ref[jnp.maximum(i - 1, 0)]

        @pl.when((i == 0) | (e != e_prev))
        def _():
            wg_bf[...] = wg_ref[0, 0].astype(BF16)
            wu_bf[...] = wu_ref[0, 0].astype(BF16)
            wd_bf[...] = wd_ref[0, 0].astype(BF16)

        gather_wait(slot)
        for r in range(TM):
            gather_start(srcn_ref, other, r)
        g = jnp.zeros((TM, EXPERT_FF), F32)
        u = jnp.zeros((TM, EXPERT_FF), F32)
        per_pass = MXU_K // LANES
        for c in range(D_MODEL // MXU_K):
            xk = _from_rows(xbuf.at[slot], TM, c * per_pass, (c + 1) * per_pass,
                            pitch=RPT_PAD).astype(BF16)
            g = g + jnp.dot(xk, wg_bf[c * MXU_K:(c + 1) * MXU_K, :], preferred_element_type=F32)
            u = u + jnp.dot(xk, wu_bf[c * MXU_K:(c + 1) * MXU_K, :], preferred_element_type=F32)
        hid = g / (1.0 + jnp.exp(-g)) * u
        y = jnp.dot(hid.astype(BF16), wd_bf[...], preferred_element_type=F32)
        _to_rows(ybuf.at[slot], y * gate_ref[...], TM, pitch=RPT_PAD)

    for r in range(TM):
        scatter_start(slot, r)

    @pl.when(i == nv - 1)
    def _():
        gather_wait(other)

    @pl.when(i == N_TILES - 1)
    def _():
        scatter_wait(slot)
        scatter_wait(other)


def _moe_grouped(layer, plan, h, w_gate, w_up, w_down):
    tile_expert, n_valid, src_tok, dst_row, gate_col = plan
    last = lambda nv: nv[0] - 1
    idx_blk = lambda f: pl.BlockSpec((1, 1, TM), f, memory_space=pltpu.SMEM)
    w_spec = lambda shape: pl.BlockSpec(
        (1, 1) + shape, lambda i, te, nv: (layer, te[jnp.minimum(i, last(nv))], 0, 0))
    grid_spec = pltpu.PrefetchScalarGridSpec(
        num_scalar_prefetch=2,
        grid=(N_TILES,),
        in_specs=[
            idx_blk(lambda i, te, nv: (0, 0, 0)),
            idx_blk(lambda i, te, nv: (jnp.minimum(i + 1, last(nv)), 0, 0)),
            idx_blk(lambda i, te, nv: (i, 0, 0)),
            pl.BlockSpec(memory_space=pl.ANY),
            w_spec((D_MODEL, EXPERT_FF)),
            w_spec((D_MODEL, EXPERT_FF)),
            w_spec((EXPERT_FF, D_MODEL)),
            pl.BlockSpec((TM, 1), lambda i, te, nv: (jnp.minimum(i, last(nv)), 0)),
        ],
        out_specs=pl.BlockSpec(memory_space=pl.ANY),
        scratch_shapes=[pltpu.VMEM((2, TM * RPT_PAD, LANES), F32),
                        pltpu.VMEM((2, TM * RPT_PAD, LANES), F32),
                        pltpu.VMEM((D_MODEL, EXPERT_FF), BF16),
                        pltpu.VMEM((D_MODEL, EXPERT_FF), BF16),
                        pltpu.VMEM((EXPERT_FF, D_MODEL), BF16),
                        pltpu.SemaphoreType.DMA((2,)),
                        pltpu.SemaphoreType.DMA((2,))],
    )
    idx3 = lambda v: v.reshape(N_TILES, 1, TM)
    return pl.pallas_call(
        _moe_kernel,
        grid_spec=grid_spec,
        out_shape=jax.ShapeDtypeStruct((N_SORT * RPT, LANES), F32),
        compiler_params=pltpu.CompilerParams(
            dimension_semantics=("arbitrary",), vmem_limit_bytes=VMEM_LIMIT),
        name="moe_grouped",
    )(tile_expert, n_valid, idx3(src_tok), idx3(src_tok), idx3(dst_row), h,
      w_gate, w_up, w_down, gate_col)


def _route_plan(ids, wts):
    e = ids.reshape(-1)
    w = wts.reshape(-1)
    onehot = (e[:, None] == jnp.arange(N_EXPERTS, dtype=I32)[None, :]).astype(I32)
    csum = jnp.cumsum(onehot, axis=0)
    rank = jnp.sum(onehot * csum, axis=1) - 1
    counts = csum[-1]
    padded = (counts + TM - 1) // TM * TM
    ends = jnp.cumsum(padded)
    pos = (ends - padded)[e] + rank
    pair = jnp.full((N_SORT,), -1, I32).at[pos].set(jnp.arange(2 * N_TOK, dtype=I32))
    is_pad = pair < 0
    pad_row = 2 * N_TOK + jnp.cumsum(is_pad.astype(I32)) - 1
    src_tok = jnp.where(is_pad, 0, pair % N_TOK)
    gate = jnp.where(is_pad, 0.0, w[jnp.maximum(pair, 0)])
    dst_row = jnp.where(is_pad, pad_row, pair)
    n_valid = (ends[-1] // TM).astype(I32)
    tile_start = jnp.minimum(jnp.arange(N_TILES, dtype=I32), n_valid - 1) * TM
    n_done = jnp.sum((ends[None, :] <= tile_start[:, None]).astype(I32), axis=1)
    tile_expert = jnp.minimum(n_done, N_EXPERTS - 1)
    return tile_expert, n_valid.reshape(1), src_tok, dst_row, gate.reshape(N_SORT, 1)


def _combine_norm_kernel(x_ref, y0_ref, y1_ref, g_ref, *out_refs, emit_sum):
    x_new = x_ref[...] + _from_rows(y0_ref, TS) + _from_rows(y1_ref, TS)
    normed = _rms(x_new, g_ref[...], NORM_EPS)
    if emit_sum:
        out_refs[0][...] = x_new
        out_refs[1][...] = normed.astype(out_refs[1].dtype)
    else:
        out_refs[0][...] = normed.astype(out_refs[0].dtype)


def _combine_norm(x2d, yg, g, *, emit_sum, norm_dtype):
    n_blk = N_TOK // TS
    tok_spec = pl.BlockSpec((TS, D_MODEL), lambda i: (i, 0))
    out_specs = [tok_spec]
    out_shape = [jax.ShapeDtypeStruct((N_TOK, D_MODEL), norm_dtype)]
    if emit_sum:
        out_specs = [tok_spec, tok_spec]
        out_shape = [jax.ShapeDtypeStruct((N_TOK, D_MODEL), F32)] + out_shape
    return pl.pallas_call(
        functools.partial(_combine_norm_kernel, emit_sum=emit_sum),
        grid=(n_blk,),
        in_specs=[
            tok_spec,
            pl.BlockSpec((TS * RPT, LANES), lambda i: (i, 0)),
            pl.BlockSpec((TS * RPT, LANES), lambda i: (n_blk + i, 0)),
            pl.BlockSpec((1, D_MODEL), lambda i: (0, 0)),
        ],
        out_specs=tuple(out_specs),
        out_shape=tuple(out_shape),
        compiler_params=pltpu.CompilerParams(
            dimension_semantics=("arbitrary",), vmem_limit_bytes=VMEM_LIMIT),
        name="combine_norm",
    )(x2d, yg, yg, g)


def _mm_kernel(a_ref, b_ref, o_ref, *, first_block_scale):
    acc = jnp.dot(a_ref[...], b_ref[...], preferred_element_type=F32)
    if first_block_scale is not None:
        acc = acc * jnp.where(pl.program_id(0) == 0, first_block_scale, 1.0)
    o_ref[...] = acc.astype(o_ref.dtype)


def _matmul(a, b, out_dtype, tn, first_block_scale=None):
    m, k = a.shape
    n = b.shape[1]
    return pl.pallas_call(
        functools.partial(_mm_kernel, first_block_scale=first_block_scale),
        grid=(n // tn, m // MM_TM),
        in_specs=[pl.BlockSpec((MM_TM, k), lambda j, i: (i, 0)),
                  pl.BlockSpec((k, tn), lambda j, i: (0, j))],
        out_specs=pl.BlockSpec((MM_TM, tn), lambda j, i: (i, j)),
        out_shape=jax.ShapeDtypeStruct((m, n), out_dtype),
        compiler_params=pltpu.CompilerParams(
            dimension_semantics=("arbitrary", "arbitrary"), vmem_limit_bytes=VMEM_LIMIT),
        name="dense_matmul",
    )(a, b)


def _attn_kernel(lam_ref, q_ref, k_ref, v_ref, subg_ref, o_ref, *scratch, lambda_init):
    m_ref, l_ref, acc_ref, s_ref, p_ref, a_ref = (scratch[2 * n:2 * n + 2] for n in range(6))
    qi = pl.program_id(2)
    q = q_ref[...]
    qs = (q[:, :HEAD_DIM], q[:, HEAD_DIM:])
    for c in range(2):
        m_ref[c][...] = jnp.full(m_ref[c].shape, NEG_INF, F32)
        l_ref[c][...] = jnp.zeros(l_ref[c].shape, F32)
        acc_ref[c][...] = jnp.zeros(acc_ref[c].shape, F32)
    lane_rep = lambda a, width: jnp.concatenate([a] * (width // LANES), axis=1)

    def tile_update(j, diagonal):
        k0 = pl.multiple_of(j * TK, TK)
        kt = k_ref[pl.ds(k0, TK), :]
        vt = v_ref[pl.ds(k0, TK), :]
        def scores(c):
            kc = kt[:, c * HEAD_DIM:(c + 1) * HEAD_DIM]
            s_ref[c][...] = lax.dot_general(qs[c], kc, (((1,), (1,)), ((), ())),
                                            preferred_element_type=F32)

        def softmax(c):
            for rb in range(TQ // CHUNK):
                rows = pl.ds(rb * CHUNK, CHUNK)
                ncol = TK
                if diagonal:
                    seen = (rb + 1) * CHUNK
                    ncol = min(TK, -(-seen // LANES) * LANES)
                s = s_ref[c][rows, :ncol]
                if diagonal and seen < ncol:
                    col = lax.broadcasted_iota(I32, (CHUNK, ncol), 1)
                    s = jnp.where(col < seen, s, NEG_INF)
                m_old = m_ref[c][rows, :]
                m_new = jnp.maximum(m_old, jnp.max(s, axis=-1, keepdims=True))
                alpha = jnp.exp2(m_old - m_new)
                p = jnp.exp2(s - lane_rep(m_new, ncol))
                l_ref[c][rows, :] = alpha * l_ref[c][rows, :] + jnp.sum(p, axis=-1, keepdims=True)
                m_ref[c][rows, :] = m_new
                a_ref[c][rows, :] = alpha
                p_ref[c][rows, :ncol] = p.astype(BF16)
                if ncol < TK:
                    p_ref[c][rows, ncol:] = jnp.zeros((CHUNK, TK - ncol), BF16)

        def weighted_values(c):
            acc_ref[c][...] = lane_rep(a_ref[c][...], 2 * HEAD_DIM) * acc_ref[c][...] + jnp.dot(
                p_ref[c][...], vt, preferred_element_type=F32)

        scores(0)
        scores(1)
        softmax(0)
        weighted_values(0)
        softmax(1)
        weighted_values(1)

    def full_body(j, c):
        tile_update(j, False)
        return c

    lax.fori_loop(0, qi, full_body, 0)
    tile_update(qi, True)

    lv = lam_ref[...]
    lam = (jnp.exp(jnp.sum(lv[0:1] * lv[1:2], axis=-1, keepdims=True))
           - jnp.exp(jnp.sum(lv[2:3] * lv[3:4], axis=-1, keepdims=True)) + lambda_init)
    width = 2 * HEAD_DIM
    sub_g = subg_ref[...] * (1.0 - lambda_init)
    for rb in range(TQ // CHUNK):
        rows = pl.ds(rb * CHUNK, CHUNK)
        o = (acc_ref[0][rows, :] * lane_rep(1.0 / l_ref[0][rows, :], width)
             - acc_ref[1][rows, :] * lane_rep(lam * (1.0 / l_ref[1][rows, :]), width))
        o_ref[rows, :] = _rms(o, sub_g, SUBLN_EPS).astype(o_ref.dtype)


def _diff_attention(qkv, lam_rows, sub_g, lambda_init):
    assert TQ == TK
    n_q = SEQ // TQ
    head_w = 2 * HEAD_DIM
    return pl.pallas_call(
        functools.partial(_attn_kernel, lambda_init=lambda_init),
        grid=(BATCH, N_HEADS, n_q),
        in_specs=[
            pl.BlockSpec((SUBLANES, HEAD_DIM), lambda b, h, i: (0, 0)),
            pl.BlockSpec((TQ, head_w), lambda b, h, i: (b * n_q + i, h)),
            pl.BlockSpec((SEQ, head_w), lambda b, h, i: (b, N_HEADS + h)),
            pl.BlockSpec((SEQ, head_w), lambda b, h, i: (b, 2 * N_HEADS + h)),
            pl.BlockSpec((1, head_w), lambda b, h, i: (0, 0)),
        ],
        out_specs=pl.BlockSpec((TQ, head_w), lambda b, h, i: (b * n_q + i, h)),
        out_shape=jax.ShapeDtypeStruct((N_TOK, D_MODEL), BF16),
        scratch_shapes=(
            [pltpu.VMEM((TQ, LANES), F32)] * 2
            + [pltpu.VMEM((TQ, LANES), F32)] * 2
            + [pltpu.VMEM((TQ, head_w), F32)] * 2
            + [pltpu.VMEM((TQ, TK), F32)] * 2
            + [pltpu.VMEM((TQ, TK), BF16)] * 2
            + [pltpu.VMEM((TQ, LANES), F32)] * 2),
        compiler_params=pltpu.CompilerParams(
            dimension_semantics=("arbitrary", "arbitrary", "arbitrary"),
            vmem_limit_bytes=VMEM_LIMIT),
        name="diff_attention",
    )(lam_rows, qkv, qkv, qkv, sub_g)


def _router_params(wr_g, br_g, wr_e, br_e):
    w = jnp.zeros((D_MODEL, ROUTER_COLS), F32)
    w = w.at[:, 0:N_GROUPS].set(wr_g).at[:, E_COL0:E_COL0 + N_EXPERTS].set(wr_e)
    b = jnp.full((1, ROUTER_COLS), NEG_INF, F32)
    b = b.at[0, 0:N_GROUPS].set(br_g).at[0, E_COL0:E_COL0 + N_EXPERTS].set(br_e)
    w_hi = w.astype(BF16)
    w_lo = (w - w_hi.astype(F32)).astype(BF16)
    return jnp.stack([w_hi, w_lo]), b


def kernel(x, norm_mix_g, norm_ffn_g, pool_w, pool_scale, attn_wq, attn_wk, attn_wv, attn_wo,
           lam_q1, lam_k1, lam_q2, lam_k2, attn_sub_g, router_g_w, router_g_b, router_e_w,
           router_e_b, exp_w_gate, exp_w_up, exp_w_down, final_g):
    x2d = x.reshape(N_TOK, D_MODEL)
    row = lambda v: v.reshape(1, -1)

    wr0, br0 = _router_params(router_g_w[0], router_g_b[0], router_e_w[0], router_e_b[0])
    x1, h, ids, wts = _pool_layer(x2d, row(norm_mix_g[0]), pool_w[0].astype(BF16),
                                  row(pool_scale[0]), row(norm_ffn_g[0]), wr0, br0)
    yg = _moe_grouped(0, _route_plan(ids, wts), h, exp_w_gate, exp_w_up, exp_w_down)
    x2, hn = _combine_norm(x1, yg, row(norm_mix_g[1]), emit_sum=True, norm_dtype=BF16)

    lambda_init = 0.8 - 0.6 * math.exp(-0.3 * 1)
    w_qkv = jnp.concatenate([attn_wq[0], attn_wk[0], attn_wv[0]], axis=1).astype(BF16)
    q_scale = HEAD_DIM ** -0.5 * math.log2(math.e)
    qkv = _matmul(hn, w_qkv, BF16, D_MODEL, first_block_scale=q_scale)
    lam_rows = jnp.zeros((SUBLANES, HEAD_DIM), F32)
    lam_rows = lam_rows.at[0].set(lam_q1[0]).at[1].set(lam_k1[0])
    lam_rows = lam_rows.at[2].set(lam_q2[0]).at[3].set(lam_k2[0])
    att = _diff_attention(qkv, lam_rows, row(attn_sub_g[0]), lambda_init)
    wr1, br1 = _router_params(router_g_w[1], router_g_b[1], router_e_w[1], router_e_b[1])
    x3, h, ids, wts = _proj_router(x2, att, attn_wo[0].astype(BF16), row(norm_ffn_g[1]),
                                   wr1, br1)
    yg = _moe_grouped(1, _route_plan(ids, wts), h, exp_w_gate, exp_w_up, exp_w_down)
    (out,) = _combine_norm(x3, yg, row(final_g), emit_sum=False, norm_dtype=F32)
    return out.reshape(BATCH, SEQ, D_MODEL)
```

```python
import functools
import math

import jax
import jax.numpy as jnp
from jax import lax
from jax.experimental import pallas as pl
from jax.experimental.pallas import tpu as pltpu

F32 = jnp.float32
BF16 = jnp.bfloat16
I32 = jnp.int32

D_MODEL = 2048
BATCH = 8
SEQ = 2048
N_TOK = BATCH * SEQ
CHUNK = 64
POOL_WINDOWS = (2, 4, 8, 16)
POOL_CG = D_MODEL // len(POOL_WINDOWS)
POOL_HALO = 16
N_HEADS = 8
HEAD_DIM = 128
N_GROUPS = 4
EXPERTS_PER_GROUP = 8
N_EXPERTS = 32
EXPERT_FF = 512
NORM_EPS = 1e-6
SUBLN_EPS = 1e-5
NEG_INF = -1e30

SUBLANES = 8
LANES = 128
RPT = D_MODEL // LANES
RPT_PAD = RPT + 1
MXU_K = 256
ROUTER_COLS = LANES
E_COL0 = 8

TS = 256
TM = 256
N_SORT = 2 * N_TOK + N_EXPERTS * TM
N_TILES = N_SORT // TM
TQ = 512
TK = 512
MM_TM = 512
VMEM_LIMIT = 56 * 1024 * 1024


def _rms(x, g, eps):
    ms = jnp.mean(x * x, axis=-1, keepdims=True)
    return x * lax.rsqrt(ms + eps) * g


def _to_rows(ref, val, n, pitch=RPT):
    for c in range(RPT):
        ref[pl.ds(c, n, stride=pitch), :] = val[:, c * LANES:(c + 1) * LANES]


def _from_rows(ref, n, lo=0, hi=RPT, pitch=RPT):
    return jnp.concatenate(
        [ref[pl.ds(c, n, stride=pitch), :] for c in range(lo, hi)], axis=-1)


def _router_tail(x_new, gffn_ref, wr_ref, br_ref, h_ref, ids_ref, wts_ref):
    h = _rms(x_new, gffn_ref[...], NORM_EPS)
    _to_rows(h_ref, h, x_new.shape[0])
    h_hi = h.astype(BF16)
    h_lo = (h - h_hi.astype(F32)).astype(BF16)
    w_hi = wr_ref[0]
    w_lo = wr_ref[1]
    logits = (jnp.dot(h_hi, w_hi, preferred_element_type=F32)
              + jnp.dot(h_hi, w_lo, preferred_element_type=F32)
              + jnp.dot(h_lo, w_hi, preferred_element_type=F32))
    logits = logits + br_ref[...]
    lt = logits.T
    gl = lt[0:SUBLANES, :]
    io8 = lax.broadcasted_iota(I32, gl.shape, 0)
    gmax = jnp.max(gl, axis=0, keepdims=True)
    gtop = jnp.min(jnp.where(gl == gmax, io8, SUBLANES), axis=0, keepdims=True)
    p_g = 1.0 / jnp.sum(jnp.exp(gl - gmax), axis=0, keepdims=True)
    esel = lt[E_COL0:E_COL0 + EXPERTS_PER_GROUP, :]
    for gi in range(1, N_GROUPS):
        lo = E_COL0 + gi * EXPERTS_PER_GROUP
        esel = jnp.where(gtop == gi, lt[lo:lo + EXPERTS_PER_GROUP, :], esel)
    m1 = jnp.max(esel, axis=0, keepdims=True)
    i1 = jnp.min(jnp.where(esel == m1, io8, SUBLANES), axis=0, keepdims=True)
    rest = jnp.where(io8 == i1, -jnp.inf, esel)
    m2 = jnp.max(rest, axis=0, keepdims=True)
    i2 = jnp.min(jnp.where(rest == m2, io8, SUBLANES), axis=0, keepdims=True)
    r = jnp.exp(m2 - m1)
    v1 = 1.0 / (1.0 + r)
    v2 = r / (1.0 + r)
    base = gtop * EXPERTS_PER_GROUP
    ids_ref[0:1, :] = base + i1
    ids_ref[1:2, :] = base + i2
    wts_ref[0:1, :] = p_g * v1
    wts_ref[1:2, :] = p_g * v2


def _router_out_shapes():
    return (jax.ShapeDtypeStruct((N_TOK * RPT, LANES), F32),
            jax.ShapeDtypeStruct((2, N_TOK), I32),
            jax.ShapeDtypeStruct((2, N_TOK), F32))


def _router_out_specs(tok_map):
    return (pl.BlockSpec((TS * RPT, LANES), lambda *a: (tok_map(*a), 0)),
            pl.BlockSpec((2, TS), lambda *a: (0, tok_map(*a))),
            pl.BlockSpec((2, TS), lambda *a: (0, tok_map(*a))))


def _pool_kernel(x_ref, halo_ref, gmix_ref, wpool_ref, pscale_ref, gffn_ref, wr_ref, br_ref,
                 x1_ref, h_ref, ids_ref, wts_ref, hs_ref):
    s = pl.program_id(1)
    xt = x_ref[...]
    g = gmix_ref[...]
    h = _rms(xt, g, NORM_EPS)
    hh = _rms(halo_ref[...], g, NORM_EPS)
    hs_ref[0:POOL_HALO, :] = jnp.where(s > 0, hh, 0.0)
    hs_ref[POOL_HALO:, :] = h
    pos = s * TS + lax.broadcasted_iota(I32, (TS, 1), 0)
    outs = []
    for gi, w in enumerate(POOL_WINDOWS):
        c0 = gi * POOL_CG
        he = hs_ref[:, c0:c0 + POOL_CG]
        acc = he
        sh = 1
        while sh < w:
            acc = acc + pltpu.roll(acc, sh, axis=0)
            sh *= 2
        inv_cnt = 1.0 / jnp.minimum(pos + 1, w).astype(F32)
        mix = acc[POOL_HALO:, :] * inv_cnt - he[POOL_HALO:, :]
        outs.append(jnp.dot(mix.astype(BF16), wpool_ref[gi], preferred_element_type=F32))
    x1 = xt + jnp.concatenate(outs, axis=-1) * pscale_ref[...]
    x1_ref[...] = x1
    _router_tail(x1, gffn_ref, wr_ref, br_ref, h_ref, ids_ref, wts_ref)


def _pool_layer(x2d, gmix, wpool_bf, pscale, gffn, wr2, br):
    n_s = SEQ // TS
    tok_map = lambda b, s: b * n_s + s
    halo_blocks = TS // POOL_HALO
    const2 = lambda b, s: (0, 0)
    return pl.pallas_call(
        _pool_kernel,
        grid=(BATCH, n_s),
        in_specs=[
            pl.BlockSpec((TS, D_MODEL), lambda b, s: (tok_map(b, s), 0)),
            pl.BlockSpec((POOL_HALO, D_MODEL),
                         lambda b, s: (jnp.maximum(tok_map(b, s) * halo_blocks - 1, 0), 0)),
            pl.BlockSpec((1, D_MODEL), const2),
            pl.BlockSpec((len(POOL_WINDOWS), POOL_CG, POOL_CG), lambda b, s: (0, 0, 0)),
            pl.BlockSpec((1, D_MODEL), const2),
            pl.BlockSpec((1, D_MODEL), const2),
            pl.BlockSpec((2, D_MODEL, ROUTER_COLS), lambda b, s: (0, 0, 0)),
            pl.BlockSpec((1, ROUTER_COLS), const2),
        ],
        out_specs=(pl.BlockSpec((TS, D_MODEL), lambda b, s: (tok_map(b, s), 0)),)
        + _router_out_specs(tok_map),
        out_shape=(jax.ShapeDtypeStruct((N_TOK, D_MODEL), F32),) + _router_out_shapes(),
        scratch_shapes=[pltpu.VMEM((TS + POOL_HALO, D_MODEL), F32)],
        compiler_params=pltpu.CompilerParams(
            dimension_semantics=("arbitrary", "arbitrary"), vmem_limit_bytes=VMEM_LIMIT),
        name="pool_mixer_router",
    )(x2d, x2d, gmix, wpool_bf, pscale, gffn, wr2, br)


def _proj_router_kernel(x_ref, a_ref, wo_ref, gffn_ref, wr_ref, br_ref,
                        xo_ref, h_ref, ids_ref, wts_ref):
    x_new = x_ref[...] + jnp.dot(a_ref[...], wo_ref[...], preferred_element_type=F32)
    xo_ref[...] = x_new
    _router_tail(x_new, gffn_ref, wr_ref, br_ref, h_ref, ids_ref, wts_ref)


def _proj_router(x2d, att, wo_bf, gffn, wr2, br):
    tok_map = lambda i: i
    const2 = lambda i: (0, 0)
    return pl.pallas_call(
        _proj_router_kernel,
        grid=(N_TOK // TS,),
        in_specs=[
            pl.BlockSpec((TS, D_MODEL), lambda i: (i, 0)),
            pl.BlockSpec((TS, D_MODEL), lambda i: (i, 0)),
            pl.BlockSpec((D_MODEL, D_MODEL), const2),
            pl.BlockSpec((1, D_MODEL), const2),
            pl.BlockSpec((2, D_MODEL, ROUTER_COLS), lambda i: (0, 0, 0)),
            pl.BlockSpec((1, ROUTER_COLS), const2),
        ],
        out_specs=(pl.BlockSpec((TS, D_MODEL), lambda i: (i, 0)),) + _router_out_specs(tok_map),
        out_shape=(jax.ShapeDtypeStruct((N_TOK, D_MODEL), F32),) + _router_out_shapes(),
        compiler_params=pltpu.CompilerParams(
            dimension_semantics=("arbitrary",), vmem_limit_bytes=VMEM_LIMIT),
        name="proj_router",
    )(x2d, att, wo_bf, gffn, wr2, br)


def _moe_kernel(te_ref, nv_ref, src0_ref, srcn_ref, dstp_ref, h_ref, wg_ref, wu_ref, wd_ref,
                gate_ref, yg_ref, xbuf, ybuf, wg_bf, wu_bf, wd_bf, sem_g, sem_s):
    i = pl.program_id(0)
    nv = nv_ref[0]
    slot = i % 2
    other = 1 - slot
    k_chunks = D_MODEL // MXU_K
    n_chunks = D_MODEL // EXPERT_FF
    gathers_per_chunk = TM // k_chunks
    scatters_per_chunk = TM // n_chunks

    def hbm_token(ref, t):
        return ref.at[pl.ds(pl.multiple_of(t * RPT, RPT), RPT), :]

    def gather_start(idx_ref, s, r):
        pltpu.make_async_copy(hbm_token(h_ref, idx_ref[0, 0, r]),
                              xbuf.at[s, pl.ds(r * RPT_PAD, RPT), :],
                              sem_g.at[s]).start(priority=1)

    def gather_wait(s):
        pltpu.make_async_copy(h_ref.at[pl.ds(0, TM * RPT), :],
                              xbuf.at[s, pl.ds(0, TM * RPT), :], sem_g.at[s]).wait()

    def scatter_start(s, r):
        pltpu.make_async_copy(ybuf.at[s, pl.ds(r * RPT_PAD, RPT), :],
                              hbm_token(yg_ref, dstp_ref[0, 0, r]),
                              sem_s.at[s]).start(priority=r % 2)

    def scatter_wait(s):
        pltpu.make_async_copy(ybuf.at[s, pl.ds(0, TM * RPT), :],
                              yg_ref.at[pl.ds(0, TM * RPT), :], sem_s.at[s]).wait()

    @pl.when(i == 0)
    def _():
        for r in range(TM):
            gather_start(src0_ref, 0, r)

    @pl.when(i >= 2)
    def _():
        scatter_wait(slot)

    @pl.when(i < nv)
    def _():
        e = te_ref[i]
        e_prev = te_ref[jnp.maximum(i - 1, 0)]

        @pl.when((i == 0) | (e != e_prev))
        def _():
            wg_bf[...] = wg_ref[0, 0].astype(BF16)
            wu_bf[...] = wu_ref[0, 0].astype(BF16)
            wd_bf[...] = wd_ref[0, 0].astype(BF16)

        gather_wait(slot)

    def expert_mlp(send_previous):
        g = jnp.zeros((TM, EXPERT_FF), F32)
        u = jnp.zeros((TM, EXPERT_FF), F32)
        per_pass = MXU_K // LANES
        for c in range(k_chunks):
            for r in range(c * gathers_per_chunk, (c + 1) * gathers_per_chunk):
                gather_start(srcn_ref, other, r)
            xk = _from_rows(xbuf.at[slot], TM, c * per_pass, (c + 1) * per_pass,
                            pitch=RPT_PAD).astype(BF16)
            g = g + jnp.dot(xk, wg_bf[c * MXU_K:(c + 1) * MXU_K, :], preferred_element_type=F32)
            u = u + jnp.dot(xk, wu_bf[c * MXU_K:(c + 1) * MXU_K, :], preferred_element_type=F32)
        hid = (g / (1.0 + jnp.exp(-g)) * u).astype(BF16)
        gate = gate_ref[...]
        lanes_per_chunk = EXPERT_FF // LANES
        for n in range(n_chunks):
            if send_previous:
                for r in range(n * scatters_per_chunk, (n + 1) * scatters_per_chunk):
                    scatter_start(other, r)
            y = jnp.dot(hid, wd_bf[:, n * EXPERT_FF:(n + 1) * EXPERT_FF],
                        preferred_element_type=F32) * gate
            for c in range(lanes_per_chunk):
                ybuf[slot, pl.ds(n * lanes_per_chunk + c, TM, stride=RPT_PAD), :] = (
                    y[:, c * LANES:(c + 1) * LANES])

    pl.when((i < nv) & (i == 0))(functools.partial(expert_mlp, False))
    pl.when((i < nv) & (i > 0))(functools.partial(expert_mlp, True))

    @pl.when(i >= nv)
    def _():
        for r in range(TM):
            scatter_start(other, r)

    @pl.when(i == nv - 1)
    def _():
        gather_wait(other)

    @pl.when(i == N_TILES)
    def _():
        scatter_wait(other)


def _moe_grouped(layer, plan, h, w_gate, w_up, w_down):
    tile_expert, n_valid, src_tok, dst_row, gate_col = plan
    last = lambda nv: nv[0] - 1
    idx_blk = lambda f: pl.BlockSpec((1, 1, TM), f, memory_space=pltpu.SMEM)
    w_spec = lambda shape: pl.BlockSpec(
        (1, 1) + shape, lambda i, te, nv: (layer, te[jnp.minimum(i, last(nv))], 0, 0))
    grid_spec = pltpu.PrefetchScalarGridSpec(
        num_scalar_prefetch=2,
        grid=(N_TILES + 1,),
        in_specs=[
            idx_blk(lambda i, te, nv: (0, 0, 0)),
            idx_blk(lambda i, te, nv: (jnp.minimum(i + 1, last(nv)), 0, 0)),
            idx_blk(lambda i, te, nv: (jnp.maximum(i - 1, 0), 0, 0)),
            pl.BlockSpec(memory_space=pl.ANY),
            w_spec((D_MODEL, EXPERT_FF)),
            w_spec((D_MODEL, EXPERT_FF)),
            w_spec((EXPERT_FF, D_MODEL)),
            pl.BlockSpec((TM, 1), lambda i, te, nv: (jnp.minimum(i, last(nv)), 0)),
        ],
        out_specs=pl.BlockSpec(memory_space=pl.ANY),
        scratch_shapes=[pltpu.VMEM((2, TM * RPT_PAD, LANES), F32),
                        pltpu.VMEM((2, TM * RPT_PAD, LANES), F32),
                        pltpu.VMEM((D_MODEL, EXPERT_FF), BF16),
                        pltpu.VMEM((D_MODEL, EXPERT_FF), BF16),
                        pltpu.VMEM((EXPERT_FF, D_MODEL), BF16),
                        pltpu.SemaphoreType.DMA((2,)),
                        pltpu.SemaphoreType.DMA((2,))],
    )
    idx3 = lambda v: v.reshape(N_TILES, 1, TM)
    return pl.pallas_call(
        _moe_kernel,
        grid_spec=grid_spec,
        out_shape=jax.ShapeDtypeStruct((N_SORT * RPT, LANES), F32),
        compiler_params=pltpu.CompilerParams(
            dimension_semantics=("arbitrary",), vmem_limit_bytes=VMEM_LIMIT),
        name="moe_grouped",
    )(tile_expert, n_valid, idx3(src_tok), idx3(src_tok), idx3(dst_row), h,
      w_gate, w_up, w_down, gate_col)


def _route_plan(ids, wts):
    e = ids.reshape(-1)
    w = wts.reshape(-1)
    onehot = (e[:, None] == jnp.arange(N_EXPERTS, dtype=I32)[None, :]).astype(I32)
    csum = jnp.cumsum(onehot, axis=0)
    rank = jnp.sum(onehot * csum, axis=1) - 1
    counts = csum[-1]
    padded = (counts + TM - 1) // TM * TM
    ends = jnp.cumsum(padded)
    pos = (ends - padded)[e] + rank
    pair = jnp.full((N_SORT,), -1, I32).at[pos].set(jnp.arange(2 * N_TOK, dtype=I32))
    is_pad = pair < 0
    pad_row = 2 * N_TOK + jnp.cumsum(is_pad.astype(I32)) - 1
    src_tok = jnp.where(is_pad, 0, pair % N_TOK)
    gate = jnp.where(is_pad, 0.0, w[jnp.maximum(pair, 0)])
    dst_row = jnp.where(is_pad, pad_row, pair)
    n_valid = (ends[-1] // TM).astype(I32)
    tile_start = jnp.minimum(jnp.arange(N_TILES, dtype=I32), n_valid - 1) * TM
    n_done = jnp.sum((ends[None, :] <= tile_start[:, None]).astype(I32), axis=1)
    tile_expert = jnp.minimum(n_done, N_EXPERTS - 1)
    return tile_expert, n_valid.reshape(1), src_tok, dst_row, gate.reshape(N_SORT, 1)


def _combine_norm_kernel(x_ref, y0_ref, y1_ref, g_ref, *out_refs, emit_sum):
    x_new = x_ref[...] + _from_rows(y0_ref, TS) + _from_rows(y1_ref, TS)
    normed = _rms(x_new, g_ref[...], NORM_EPS)
    if emit_sum:
        out_refs[0][...] = x_new
        out_refs[1][...] = normed.astype(out_refs[1].dtype)
    else:
        out_refs[0][...] = normed.astype(out_refs[0].dtype)


def _combine_norm(x2d, yg, g, *, emit_sum, norm_dtype):
    n_blk = N_TOK // TS
    tok_spec = pl.BlockSpec((TS, D_MODEL), lambda i: (i, 0))
    out_specs = [tok_spec]
    out_shape = [jax.ShapeDtypeStruct((N_TOK, D_MODEL), norm_dtype)]
    if emit_sum:
        out_specs = [tok_spec, tok_spec]
        out_shape = [jax.ShapeDtypeStruct((N_TOK, D_MODEL), F32)] + out_shape
    return pl.pallas_call(
        functools.partial(_combine_norm_kernel, emit_sum=emit_sum),
        grid=(n_blk,),
        in_specs=[
            tok_spec,
            pl.BlockSpec((TS * RPT, LANES), lambda i: (i, 0)),
            pl.BlockSpec((TS * RPT, LANES), lambda i: (n_blk + i, 0)),
            pl.BlockSpec((1, D_MODEL), lambda i: (0, 0)),
        ],
        out_specs=tuple(out_specs),
        out_shape=tuple(out_shape),
        compiler_params=pltpu.CompilerParams(
            dimension_semantics=("arbitrary",), vmem_limit_bytes=VMEM_LIMIT),
        name="combine_norm",
    )(x2d, yg, yg, g)


def _mm_kernel(a_ref, b_ref, o_ref, *, first_block_scale):
    acc = jnp.dot(a_ref[...], b_ref[...], preferred_element_type=F32)
    if first_block_scale is not None:
        acc = acc * jnp.where(pl.program_id(0) == 0, first_block_scale, 1.0)
    o_ref[...] = acc.astype(o_ref.dtype)


def _matmul(a, b, out_dtype, tn, first_block_scale=None):
    m, k = a.shape
    n = b.shape[1]
    return pl.pallas_call(
        functools.partial(_mm_kernel, first_block_scale=first_block_scale),
        grid=(n // tn, m // MM_TM),
        in_specs=[pl.BlockSpec((MM_TM, k), lambda j, i: (i, 0)),
                  pl.BlockSpec((k, tn), lambda j, i: (0, j))],
        out_specs=pl.BlockSpec((MM_TM, tn), lambda j, i: (i, j)),
        out_shape=jax.ShapeDtypeStruct((m, n), out_dtype),
        compiler_params=pltpu.CompilerParams(
            dimension_semantics=("arbitrary", "arbitrary"), vmem_limit_bytes=VMEM_LIMIT),
        name="dense_matmul",
    )(a, b)


def _attn_kernel(lam_ref, q_ref, k_ref, v_ref, subg_ref, o_ref, *scratch, lambda_init):
    m_ref, l_ref, acc_ref, s_ref, p_ref, a_ref = (scratch[2 * n:2 * n + 2] for n in range(6))
    qi = pl.program_id(2)
    q = q_ref[...]
    qs = (q[:, :HEAD_DIM], q[:, HEAD_DIM:])
    for c in range(2):
        m_ref[c][...] = jnp.full(m_ref[c].shape, NEG_INF, F32)
        l_ref[c][...] = jnp.zeros(l_ref[c].shape, F32)
        acc_ref[c][...] = jnp.zeros(acc_ref[c].shape, F32)
    lane_rep = lambda a, width: jnp.concatenate([a] * (width // LANES), axis=1)

    def tile_update(j, diagonal):
        k0 = pl.multiple_of(j * TK, TK)
        kt = k_ref[pl.ds(k0, TK), :]
        vt = v_ref[pl.ds(k0, TK), :]
        def scores(c):
            kc = kt[:, c * HEAD_DIM:(c + 1) * HEAD_DIM]
            s_ref[c][...] = lax.dot_general(qs[c], kc, (((1,), (1,)), ((), ())),
                                            preferred_element_type=F32)

        def softmax(c):
            for rb in range(TQ // CHUNK):
                rows = pl.ds(rb * CHUNK, CHUNK)
                ncol = TK
                if diagonal:
                    seen = (rb + 1) * CHUNK
                    ncol = min(TK, -(-seen // LANES) * LANES)
                s = s_ref[c][rows, :ncol]
                if diagonal and seen < ncol:
                    col = lax.broadcasted_iota(I32, (CHUNK, ncol), 1)
                    s = jnp.where(col < seen, s, NEG_INF)
                m_old = m_ref[c][rows, :]
                m_new = jnp.maximum(m_old, jnp.max(s, axis=-1, keepdims=True))
                alpha = jnp.exp2(m_old - m_new)
                p = jnp.exp2(s - lane_rep(m_new, ncol))
                l_ref[c][rows, :] = alpha * l_ref[c][rows, :] + jnp.sum(p, axis=-1, keepdims=True)
                m_ref[c][rows, :] = m_new
                a_ref[c][rows, :] = alpha
                p_ref[c][rows, :ncol] = p.astype(BF16)
                if ncol < TK:
                    p_ref[c][rows, ncol:] = jnp.zeros((CHUNK, TK - ncol), BF16)

        def weighted_values(c):
            acc_ref[c][...] = lane_rep(a_ref[c][...], 2 * HEAD_DIM) * acc_ref[c][...] + jnp.dot(
                p_ref[c][...], vt, preferred_element_type=F32)

        scores(0)
        scores(1)
        softmax(0)
        weighted_values(0)
        softmax(1)
        weighted_values(1)

    def full_body(j, c):
        tile_update(j, False)
        return c

    lax.fori_loop(0, qi, full_body, 0)
    tile_update(qi, True)

    lv = lam_ref[...]
    lam = (jnp.exp(jnp.sum(lv[0:1] * lv[1:2], axis=-1, keepdims=True))
           - jnp.exp(jnp.sum(lv[2:3] * lv[3:4], axis=-1, keepdims=True)) + lambda_init)
    width = 2 * HEAD_DIM
    sub_g = subg_ref[...] * (1.0 - lambda_init)
    for rb in range(TQ // CHUNK):
        rows = pl.ds(rb * CHUNK, CHUNK)
        o = (acc_ref[0][rows, :] * lane_rep(1.0 / l_ref[0][rows, :], width)
             - acc_ref[1][rows, :] * lane_rep(lam * (1.0 / l_ref[1][rows, :]), width))
        o_ref[rows, :] = _rms(o, sub_g, SUBLN_EPS).astype(o_ref.dtype)


def _diff_attention(qkv, lam_rows, sub_g, lambda_init):
    assert TQ == TK
    n_q = SEQ // TQ
    head_w = 2 * HEAD_DIM
    return pl.pallas_call(
        functools.partial(_attn_kernel, lambda_init=lambda_init),
        grid=(BATCH, N_HEADS, n_q),
        in_specs=[
            pl.BlockSpec((SUBLANES, HEAD_DIM), lambda b, h, i: (0, 0)),
            pl.BlockSpec((TQ, head_w), lambda b, h, i: (b * n_q + i, h)),
            pl.BlockSpec((SEQ, head_w), lambda b, h, i: (b, N_HEADS + h)),
            pl.BlockSpec((SEQ, head_w), lambda b, h, i: (b, 2 * N_HEADS + h)),
            pl.BlockSpec((1, head_w), lambda b, h, i: (0, 0)),
        ],
        out_specs=pl.BlockSpec((TQ, head_w), lambda b, h, i: (b * n_q + i, h)),
        out_shape=jax.ShapeDtypeStruct((N_TOK, D_MODEL), BF16),
        scratch_shapes=(
            [pltpu.VMEM((TQ, LANES), F32)] * 2
            + [pltpu.VMEM((TQ, LANES), F32)] * 2
            + [pltpu.VMEM((TQ, head_w), F32)] * 2
            + [pltpu.VMEM((TQ, TK), F32)] * 2
            + [pltpu.VMEM((TQ, TK), BF16)] * 2
            + [pltpu.VMEM((TQ, LANES), F32)] * 2),
        compiler_params=pltpu.CompilerParams(
            dimension_semantics=("arbitrary", "arbitrary", "arbitrary"),
            vmem_limit_bytes=VMEM_LIMIT),
        name="diff_attention",
    )(lam_rows, qkv, qkv, qkv, sub_g)


def _router_params(wr_g, br_g, wr_e, br_e):
    w = jnp.zeros((D_MODEL, ROUTER_COLS), F32)
    w = w.at[:, 0:N_GROUPS].set(wr_g).at[:, E_COL0:E_COL0 + N_EXPERTS].set(wr_e)
    b = jnp.full((1, ROUTER_COLS), NEG_INF, F32)
    b = b.at[0, 0:N_GROUPS].set(br_g).at[0, E_COL0:E_COL0 + N_EXPERTS].set(br_e)
    w_hi = w.astype(BF16)
    w_lo = (w - w_hi.astype(F32)).astype(BF16)
    return jnp.stack([w_hi, w_lo]), b


def kernel(x, norm_mix_g, norm_ffn_g, pool_w, pool_scale, attn_wq, attn_wk, attn_wv, attn_wo,
           lam_q1, lam_k1, lam_q2, lam_k2, attn_sub_g, router_g_w, router_g_b, router_e_w,
           router_e_b, exp_w_gate, exp_w_up, exp_w_down, final_g):
    x2d = x.reshape(N_TOK, D_MODEL)
    row = lambda v: v.reshape(1, -1)

    wr0, br0 = _router_params(router_g_w[0], router_g_b[0], router_e_w[0], router_e_b[0])
    x1, h, ids, wts = _pool_layer(x2d, row(norm_mix_g[0]), pool_w[0].astype(BF16),
                                  row(pool_scale[0]), row(norm_ffn_g[0]), wr0, br0)
    yg = _moe_grouped(0, _route_plan(ids, wts), h, exp_w_gate, exp_w_up, exp_w_down)
    x2, hn = _combine_norm(x1, yg, row(norm_mix_g[1]), emit_sum=True, norm_dtype=BF16)

    lambda_init = 0.8 - 0.6 * math.exp(-0.3 * 1)
    w_qkv = jnp.concatenate([attn_wq[0], attn_wk[0], attn_wv[0]], axis=1).astype(BF16)
    q_scale = HEAD_DIM ** -0.5 * math.log2(math.e)
    qkv = _matmul(hn, w_qkv, BF16, D_MODEL, first_block_scale=q_scale)
    lam_rows = jnp.zeros((SUBLANES, HEAD_DIM), F32)
    lam_rows = lam_rows.at[0].set(lam_q1[0]).at[1].set(lam_k1[0])
    lam_rows = lam_rows.at[2].set(lam_q2[0]).at[3].set(lam_k2[0])
    att = _diff_attention(qkv, lam_rows, row(attn_sub_g[0]), lambda_init)
    wr1, br1 = _router_params(router_g_w[1], router_g_b[1], router_e_w[1], router_e_b[1])
    x3, h, ids, wts = _proj_router(x2, att, attn_wo[0].astype(BF16), row(norm_ffn_g[1]),
                                   wr1, br1)
    yg = _moe_grouped(1, _route_plan(ids, wts), h, exp_w_gate, exp_w_up, exp_w_down)
    (out,) = _combine_norm(x3, yg, row(final_g), emit_sum=False, norm_dtype=F32)
    return out.reshape(BATCH, SEQ, D_MODEL)
```

```python
import functools
import math

import jax
import jax.numpy as jnp
from jax import lax
from jax.experimental import pallas as pl
from jax.experimental.pallas import tpu as pltpu

F32 = jnp.float32
BF16 = jnp.bfloat16
I32 = jnp.int32

D_MODEL = 2048
BATCH = 8
SEQ = 2048
N_TOK = BATCH * SEQ
CHUNK = 64
POOL_WINDOWS = (2, 4, 8, 16)
POOL_CG = D_MODEL // len(POOL_WINDOWS)
POOL_HALO = 16
N_HEADS = 8
HEAD_DIM = 128
N_GROUPS = 4
EXPERTS_PER_GROUP = 8
N_EXPERTS = 32
EXPERT_FF = 512
NORM_EPS = 1e-6
SUBLN_EPS = 1e-5
NEG_INF = -1e30

SUBLANES = 8
LANES = 128
RPT = D_MODEL // LANES
RPT_PAD = RPT + 1
MXU_K = 256
ROUTER_COLS = LANES
E_COL0 = 8

TS = 256
TM = 256
N_SORT = 2 * N_TOK + N_EXPERTS * TM
N_TILES = N_SORT // TM
TQ = 512
TK = 512
MM_TM = 512
VMEM_LIMIT = 56 * 1024 * 1024


def _rms(x, g, eps):
    ms = jnp.mean(x * x, axis=-1, keepdims=True)
    return x * lax.rsqrt(ms + eps) * g


def _to_rows(ref, val, n, pitch=RPT):
    for c in range(RPT):
        ref[pl.ds(c, n, stride=pitch), :] = val[:, c * LANES:(c + 1) * LANES]


def _from_rows(ref, n, lo=0, hi=RPT, pitch=RPT):
    return jnp.concatenate(
        [ref[pl.ds(c, n, stride=pitch), :] for c in range(lo, hi)], axis=-1)


def _router_tail(x_new, gffn_ref, wr_ref, br_ref, h_ref, ids_ref, wts_ref):
    h = _rms(x_new, gffn_ref[...], NORM_EPS)
    _to_rows(h_ref, h, x_new.shape[0])
    h_hi = h.astype(BF16)
    h_lo = (h - h_hi.astype(F32)).astype(BF16)
    w_hi = wr_ref[0]
    w_lo = wr_ref[1]
    logits = (jnp.dot(h_hi, w_hi, preferred_element_type=F32)
              + jnp.dot(h_hi, w_lo, preferred_element_type=F32)
              + jnp.dot(h_lo, w_hi, preferred_element_type=F32))
    logits = logits + br_ref[...]
    lt = logits.T
    gl = lt[0:SUBLANES, :]
    io8 = lax.broadcasted_iota(I32, gl.shape, 0)
    gmax = jnp.max(gl, axis=0, keepdims=True)
    gtop = jnp.min(jnp.where(gl == gmax, io8, SUBLANES), axis=0, keepdims=True)
    p_g = 1.0 / jnp.sum(jnp.exp(gl - gmax), axis=0, keepdims=True)
    esel = lt[E_COL0:E_COL0 + EXPERTS_PER_GROUP, :]
    for gi in range(1, N_GROUPS):
        lo = E_COL0 + gi * EXPERTS_PER_GROUP
        esel = jnp.where(gtop == gi, lt[lo:lo + EXPERTS_PER_GROUP, :], esel)
    m1 = jnp.max(esel, axis=0, keepdims=True)
    i1 = jnp.min(jnp.where(esel == m1, io8, SUBLANES), axis=0, keepdims=True)
    rest = jnp.where(io8 == i1, -jnp.inf, esel)
    m2 = jnp.max(rest, axis=0, keepdims=True)
    i2 = jnp.min(jnp.where(rest == m2, io8, SUBLANES), axis=0, keepdims=True)
    r = jnp.exp(m2 - m1)
    v1 = 1.0 / (1.0 + r)
    v2 = r / (1.0 + r)
    base = gtop * EXPERTS_PER_GROUP
    ids_ref[0:1, :] = base + i1
    ids_ref[1:2, :] = base + i2
    wts_ref[0:1, :] = p_g * v1
    wts_ref[1:2, :] = p_g * v2


def _router_out_shapes():
    return (jax.ShapeDtypeStruct((N_TOK * RPT, LANES), F32),
            jax.ShapeDtypeStruct((2, N_TOK), I32),
            jax.ShapeDtypeStruct((2, N_TOK), F32))


def _router_out_specs(tok_map):
    return (pl.BlockSpec((TS * RPT, LANES), lambda *a: (tok_map(*a), 0)),
            pl.BlockSpec((2, TS), lambda *a: (0, tok_map(*a))),
            pl.BlockSpec((2, TS), lambda *a: (0, tok_map(*a))))


def _pool_kernel(x_ref, halo_ref, gmix_ref, wpool_ref, pscale_ref, gffn_ref, wr_ref, br_ref,
                 x1_ref, h_ref, ids_ref, wts_ref, hs_ref):
    s = pl.program_id(1)
    xt = x_ref[...]
    g = gmix_ref[...]
    h = _rms(xt, g, NORM_EPS)
    hh = _rms(halo_ref[...], g, NORM_EPS)
    hs_ref[0:POOL_HALO, :] = jnp.where(s > 0, hh, 0.0)
    hs_ref[POOL_HALO:, :] = h
    pos = s * TS + lax.broadcasted_iota(I32, (TS, 1), 0)
    outs = []
    for gi, w in enumerate(POOL_WINDOWS):
        c0 = gi * POOL_CG
        he = hs_ref[:, c0:c0 + POOL_CG]
        acc = he
        sh = 1
        while sh < w:
            acc = acc + pltpu.roll(acc, sh, axis=0)
            sh *= 2
        inv_cnt = 1.0 / jnp.minimum(pos + 1, w).astype(F32)
        mix = acc[POOL_HALO:, :] * inv_cnt - he[POOL_HALO:, :]
        outs.append(jnp.dot(mix.astype(BF16), wpool_ref[gi], preferred_element_type=F32))
    x1 = xt + jnp.concatenate(outs, axis=-1) * pscale_ref[...]
    x1_ref[...] = x1
    _router_tail(x1, gffn_ref, wr_ref, br_ref, h_ref, ids_ref, wts_ref)


def _pool_layer(x2d, gmix, wpool_bf, pscale, gffn, wr2, br):
    n_s = SEQ // TS
    tok_map = lambda b, s: b * n_s + s
    halo_blocks = TS // POOL_HALO
    const2 = lambda b, s: (0, 0)
    return pl.pallas_call(
        _pool_kernel,
        grid=(BATCH, n_s),
        in_specs=[
            pl.BlockSpec((TS, D_MODEL), lambda b, s: (tok_map(b, s), 0)),
            pl.BlockSpec((POOL_HALO, D_MODEL),
                         lambda b, s: (jnp.maximum(tok_map(b, s) * halo_blocks - 1, 0), 0)),
            pl.BlockSpec((1, D_MODEL), const2),
            pl.BlockSpec((len(POOL_WINDOWS), POOL_CG, POOL_CG), lambda b, s: (0, 0, 0)),
            pl.BlockSpec((1, D_MODEL), const2),
            pl.BlockSpec((1, D_MODEL), const2),
            pl.BlockSpec((2, D_MODEL, ROUTER_COLS), lambda b, s: (0, 0, 0)),
            pl.BlockSpec((1, ROUTER_COLS), const2),
        ],
        out_specs=(pl.BlockSpec((TS, D_MODEL), lambda b, s: (tok_map(b, s), 0)),)
        + _router_out_specs(tok_map),
        out_shape=(jax.ShapeDtypeStruct((N_TOK, D_MODEL), F32),) + _router_out_shapes(),
        scratch_shapes=[pltpu.VMEM((TS + POOL_HALO, D_MODEL), F32)],
        compiler_params=pltpu.CompilerParams(
            dimension_semantics=("arbitrary", "arbitrary"), vmem_limit_bytes=VMEM_LIMIT),
        name="pool_mixer_router",
    )(x2d, x2d, gmix, wpool_bf, pscale, gffn, wr2, br)


def _proj_router_kernel(x_ref, a_ref, wo_ref, gffn_ref, wr_ref, br_ref,
                        xo_ref, h_ref, ids_ref, wts_ref):
    x_new = x_ref[...] + jnp.dot(a_ref[...], wo_ref[...], preferred_element_type=F32)
    xo_ref[...] = x_new
    _router_tail(x_new, gffn_ref, wr_ref, br_ref, h_ref, ids_ref, wts_ref)


def _proj_router(x2d, att, wo_bf, gffn, wr2, br):
    tok_map = lambda i: i
    const2 = lambda i: (0, 0)
    return pl.pallas_call(
        _proj_router_kernel,
        grid=(N_TOK // TS,),
        in_specs=[
            pl.BlockSpec((TS, D_MODEL), lambda i: (i, 0)),
            pl.BlockSpec((TS, D_MODEL), lambda i: (i, 0)),
            pl.BlockSpec((D_MODEL, D_MODEL), const2),
            pl.BlockSpec((1, D_MODEL), const2),
            pl.BlockSpec((2, D_MODEL, ROUTER_COLS), lambda i: (0, 0, 0)),
            pl.BlockSpec((1, ROUTER_COLS), const2),
        ],
        out_specs=(pl.BlockSpec((TS, D_MODEL), lambda i: (i, 0)),) + _router_out_specs(tok_map),
        out_shape=(jax.ShapeDtypeStruct((N_TOK, D_MODEL), F32),) + _router_out_shapes(),
        compiler_params=pltpu.CompilerParams(
            dimension_semantics=("arbitrary",), vmem_limit_bytes=VMEM_LIMIT),
        name="proj_router",
    )(x2d, att, wo_bf, gffn, wr2, br)


def _moe_kernel(te_ref, nv_ref, src0_ref, srcn_ref, dstp_ref, h_ref, wg_ref, wu_ref, wd_ref,
                gate_ref, yg_ref, xbuf, ybuf, wg_bf, wu_bf, wd_bf, sem_g, sem_s):
    i = pl.program_id(0)
    nv = nv_ref[0]
    slot = i % 2
    other = 1 - slot
    k_chunks = D_MODEL // MXU_K
    n_chunks = D_MODEL // EXPERT_FF
    gathers_per_chunk = TM // k_chunks
    scatters_per_chunk = TM // n_chunks

    def hbm_token(ref, t):
        return ref.at[pl.ds(pl.multiple_of(t * RPT, RPT), RPT), :]

    def gather_start(idx_ref, s, r):
        pltpu.make_async_copy(hbm_token(h_ref, idx_ref[0, 0, r]),
                              xbuf.at[s, pl.ds(r * RPT_PAD, RPT), :],
                              sem_g.at[s]).start(priority=1)

    def gather_wait(s):
        pltpu.make_async_copy(h_ref.at[pl.ds(0, TM * RPT), :],
                              xbuf.at[s, pl.ds(0, TM * RPT), :], sem_g.at[s]).wait()

    def scatter_start(s, r):
        pltpu.make_async_copy(ybuf.at[s, pl.ds(r * RPT_PAD, RPT), :],
                              hbm_token(yg_ref, dstp_ref[0, 0, r]),
                              sem_s.at[s]).start(priority=r % 2)

    def scatter_wait(s):
        pltpu.make_async_copy(ybuf.at[s, pl.ds(0, TM * RPT), :],
                              yg_ref.at[pl.ds(0, TM * RPT), :], sem_s.at[s]).wait()

    @pl.when(i == 0)
    def _():
        for r in range(TM):
            gather_start(src0_ref, 0, r)

    @pl.when(i >= 2)
    def _():
        scatter_wait(slot)

    @pl.when(i < nv)
    def _():
        e = te_ref[i]
        e_prev = te_ref[jnp.maximum(i - 1, 0)]

        @pl.when((i == 0) | (e != e_prev))
        def _():
            wg_bf[...] = wg_ref[0, 0].astype(BF16)
            wu_bf[...] = wu_ref[0, 0].astype(BF16)
            wd_bf[...] = wd_ref[0, 0].astype(BF16)

        gather_wait(slot)

    def expert_mlp(send_previous):
        g = jnp.zeros((TM, EXPERT_FF), F32)
        u = jnp.zeros((TM, EXPERT_FF), F32)
        per_pass = MXU_K // LANES
        for c in range(k_chunks):
            for r in range(c * gathers_per_chunk, (c + 1) * gathers_per_chunk):
                gather_start(srcn_ref, other, r)
            xk = _from_rows(xbuf.at[slot], TM, c * per_pass, (c + 1) * per_pass,
                            pitch=RPT_PAD).astype(BF16)
            g = g + jnp.dot(xk, wg_bf[c * MXU_K:(c + 1) * MXU_K, :], preferred_element_type=F32)
            u = u + jnp.dot(xk, wu_bf[c * MXU_K:(c + 1) * MXU_K, :], preferred_element_type=F32)
        hid = (g / (1.0 + jnp.exp(-g)) * u).astype(BF16)
        gate = gate_ref[...]
        lanes_per_chunk = EXPERT_FF // LANES
        for n in range(n_chunks):
            if send_previous:
                for r in range(n * scatters_per_chunk, (n + 1) * scatters_per_chunk):
                    scatter_start(other, r)
            y = jnp.dot(hid, wd_bf[:, n * EXPERT_FF:(n + 1) * EXPERT_FF],
                        preferred_element_type=F32) * gate
            for c in range(lanes_per_chunk):
                ybuf[slot, pl.ds(n * lanes_per_chunk + c, TM, stride=RPT_PAD), :] = (
                    y[:, c * LANES:(c + 1) * LANES])

    pl.when((i < nv) & (i == 0))(functools.partial(expert_mlp, False))
    pl.when((i < nv) & (i > 0))(functools.partial(expert_mlp, True))

    @pl.when(i >= nv)
    def _():
        for r in range(TM):
            scatter_start(other, r)

    @pl.when(i == nv - 1)
    def _():
        gather_wait(other)

    @pl.when(i == N_TILES)
    def _():
        scatter_wait(other)


def _moe_grouped(layer, plan, h, w_gate, w_up, w_down):
    tile_expert, n_valid, src_tok, dst_row, gate_col = plan
    last = lambda nv: nv[0] - 1
    idx_blk = lambda f: pl.BlockSpec((1, 1, TM), f, memory_space=pltpu.SMEM)
    w_spec = lambda shape: pl.BlockSpec(
        (1, 1) + shape, lambda i, te, nv: (layer, te[jnp.minimum(i, last(nv))], 0, 0))
    grid_spec = pltpu.PrefetchScalarGridSpec(
        num_scalar_prefetch=2,
        grid=(N_TILES + 1,),
        in_specs=[
            idx_blk(lambda i, te, nv: (0, 0, 0)),
            idx_blk(lambda i, te, nv: (jnp.minimum(i + 1, last(nv)), 0, 0)),
            idx_blk(lambda i, te, nv: (jnp.maximum(i - 1, 0), 0, 0)),
            pl.BlockSpec(memory_space=pl.ANY),
            w_spec((D_MODEL, EXPERT_FF)),
            w_spec((D_MODEL, EXPERT_FF)),
            w_spec((EXPERT_FF, D_MODEL)),
            pl.BlockSpec((TM, 1), lambda i, te, nv: (jnp.minimum(i, last(nv)), 0)),
        ],
        out_specs=pl.BlockSpec(memory_space=pl.ANY),
        scratch_shapes=[pltpu.VMEM((2, TM * RPT_PAD, LANES), F32),
                        pltpu.VMEM((2, TM * RPT_PAD, LANES), F32),
                        pltpu.VMEM((D_MODEL, EXPERT_FF), BF16),
                        pltpu.VMEM((D_MODEL, EXPERT_FF), BF16),
                        pltpu.VMEM((EXPERT_FF, D_MODEL), BF16),
                        pltpu.SemaphoreType.DMA((2,)),
                        pltpu.SemaphoreType.DMA((2,))],
    )
    idx3 = lambda v: v.reshape(N_TILES, 1, TM)
    return pl.pallas_call(
        _moe_kernel,
        grid_spec=grid_spec,
        out_shape=jax.ShapeDtypeStruct((N_SORT * RPT, LANES), F32),
        compiler_params=pltpu.CompilerParams(
            dimension_semantics=("arbitrary",), vmem_limit_bytes=VMEM_LIMIT),
        name="moe_grouped",
    )(tile_expert, n_valid, idx3(src_tok), idx3(src_tok), idx3(dst_row), h,
      w_gate, w_up, w_down, gate_col)


def _route_plan(ids, wts):
    e = ids.reshape(-1)
    w = wts.reshape(-1)
    onehot = (e[:, None] == jnp.arange(N_EXPERTS, dtype=I32)[None, :]).astype(I32)
    csum = jnp.cumsum(onehot, axis=0)
    rank = jnp.sum(onehot * csum, axis=1) - 1
    counts = csum[-1]
    padded = (counts + TM - 1) // TM * TM
    ends = jnp.cumsum(padded)
    pos = (ends - padded)[e] + rank
    pair = jnp.full((N_SORT,), -1, I32).at[pos].set(
        jnp.arange(2 * N_TOK, dtype=I32), unique_indices=True, mode="promise_in_bounds")
    is_pad = pair < 0
    pad_row = 2 * N_TOK + jnp.cumsum(is_pad.astype(I32)) - 1
    src_tok = jnp.where(is_pad, 0, pair % N_TOK)
    gate = jnp.where(is_pad, 0.0, w[jnp.maximum(pair, 0)])
    dst_row = jnp.where(is_pad, pad_row, pair)
    n_valid = (ends[-1] // TM).astype(I32)
    tile_start = jnp.minimum(jnp.arange(N_TILES, dtype=I32), n_valid - 1) * TM
    n_done = jnp.sum((ends[None, :] <= tile_start[:, None]).astype(I32), axis=1)
    tile_expert = jnp.minimum(n_done, N_EXPERTS - 1)
    return tile_expert, n_valid.reshape(1), src_tok, dst_row, gate.reshape(N_SORT, 1)


def _combine_norm_kernel(x_ref, y0_ref, y1_ref, g_ref, *out_refs, emit_sum):
    x_new = x_ref[...] + _from_rows(y0_ref, TS) + _from_rows(y1_ref, TS)
    normed = _rms(x_new, g_ref[...], NORM_EPS)
    if emit_sum:
        out_refs[0][...] = x_new
        out_refs[1][...] = normed.astype(out_refs[1].dtype)
    else:
        out_refs[0][...] = normed.astype(out_refs[0].dtype)


def _combine_norm(x2d, yg, g, *, emit_sum, norm_dtype):
    n_blk = N_TOK // TS
    tok_spec = pl.BlockSpec((TS, D_MODEL), lambda i: (i, 0))
    out_specs = [tok_spec]
    out_shape = [jax.ShapeDtypeStruct((N_TOK, D_MODEL), norm_dtype)]
    if emit_sum:
        out_specs = [tok_spec, tok_spec]
        out_shape = [jax.ShapeDtypeStruct((N_TOK, D_MODEL), F32)] + out_shape
    return pl.pallas_call(
        functools.partial(_combine_norm_kernel, emit_sum=emit_sum),
        grid=(n_blk,),
        in_specs=[
            tok_spec,
            pl.BlockSpec((TS * RPT, LANES), lambda i: (i, 0)),
            pl.BlockSpec((TS * RPT, LANES), lambda i: (n_blk + i, 0)),
            pl.BlockSpec((1, D_MODEL), lambda i: (0, 0)),
        ],
        out_specs=tuple(out_specs),
        out_shape=tuple(out_shape),
        compiler_params=pltpu.CompilerParams(
            dimension_semantics=("arbitrary",), vmem_limit_bytes=VMEM_LIMIT),
        name="combine_norm",
    )(x2d, yg, yg, g)


def _mm_kernel(a_ref, b_ref, o_ref, *, first_block_scale):
    acc = jnp.dot(a_ref[...], b_ref[...], preferred_element_type=F32)
    if first_block_scale is not None:
        acc = acc * jnp.where(pl.program_id(0) == 0, first_block_scale, 1.0)
    o_ref[...] = acc.astype(o_ref.dtype)


def _matmul(a, b, out_dtype, tn, first_block_scale=None):
    m, k = a.shape
    n = b.shape[1]
    return pl.pallas_call(
        functools.partial(_mm_kernel, first_block_scale=first_block_scale),
        grid=(n // tn, m // MM_TM),
        in_specs=[pl.BlockSpec((MM_TM, k), lambda j, i: (i, 0)),
                  pl.BlockSpec((k, tn), lambda j, i: (0, j))],
        out_specs=pl.BlockSpec((MM_TM, tn), lambda j, i: (i, j)),
        out_shape=jax.ShapeDtypeStruct((m, n), out_dtype),
        compiler_params=pltpu.CompilerParams(
            dimension_semantics=("arbitrary", "arbitrary"), vmem_limit_bytes=VMEM_LIMIT),
        name="dense_matmul",
    )(a, b)


def _attn_kernel(lam_ref, q_ref, k_ref, v_ref, subg_ref, o_ref, *scratch, lambda_init):
    m_ref, l_ref, acc_ref = (scratch[2 * n:2 * n + 2] for n in range(3))
    s_ref, p_ref, a_ref = (scratch[6 + 4 * n:10 + 4 * n] for n in range(3))
    lane_rep = lambda a, width: jnp.concatenate([a] * (width // LANES), axis=1)
    lv = lam_ref[...]
    lam = (jnp.exp(jnp.sum(lv[0:1] * lv[1:2], axis=-1, keepdims=True))
           - jnp.exp(jnp.sum(lv[2:3] * lv[3:4], axis=-1, keepdims=True)) + lambda_init)
    width = 2 * HEAD_DIM
    sub_g = subg_ref[...] * (1.0 - lambda_init)
    n_tile = 0

    for qi in range(SEQ // TQ):
        q = q_ref[qi * TQ:(qi + 1) * TQ, :]
        qs = (q[:, :HEAD_DIM], q[:, HEAD_DIM:])
        for c in range(2):
            m_ref[c][...] = jnp.full(m_ref[c].shape, NEG_INF, F32)
            l_ref[c][...] = jnp.zeros(l_ref[c].shape, F32)
            acc_ref[c][...] = jnp.zeros(acc_ref[c].shape, F32)

        for j in range(qi + 1):
            diagonal = j == qi
            buf = 2 * (n_tile % 2)
            n_tile += 1
            kt = k_ref[j * TK:(j + 1) * TK, :]
            vt = v_ref[j * TK:(j + 1) * TK, :]
            for c in range(2):
                kc = kt[:, c * HEAD_DIM:(c + 1) * HEAD_DIM]
                s_ref[buf + c][...] = lax.dot_general(qs[c], kc, (((1,), (1,)), ((), ())),
                                                      preferred_element_type=F32)
            for c in range(2):
                for rb in range(TQ // CHUNK):
                    rows = pl.ds(rb * CHUNK, CHUNK)
                    ncol = TK
                    if diagonal:
                        seen = (rb + 1) * CHUNK
                        ncol = min(TK, -(-seen // LANES) * LANES)
                    s = s_ref[buf + c][rows, :ncol]
                    if diagonal and seen < ncol:
                        col = lax.broadcasted_iota(I32, (CHUNK, ncol), 1)
                        s = jnp.where(col < seen, s, NEG_INF)
                    m_old = m_ref[c][rows, :]
                    m_new = jnp.maximum(m_old, jnp.max(s, axis=-1, keepdims=True))
                    alpha = jnp.exp2(m_old - m_new)
                    p = jnp.exp2(s - lane_rep(m_new, ncol))
                    l_ref[c][rows, :] = (alpha * l_ref[c][rows, :]
                                         + jnp.sum(p, axis=-1, keepdims=True))
                    m_ref[c][rows, :] = m_new
                    a_ref[buf + c][rows, :] = alpha
                    p_ref[buf + c][rows, :ncol] = p.astype(BF16)
                    if ncol < TK:
                        p_ref[buf + c][rows, ncol:] = jnp.zeros((CHUNK, TK - ncol), BF16)
                acc_ref[c][...] = (lane_rep(a_ref[buf + c][...], width) * acc_ref[c][...]
                                   + jnp.dot(p_ref[buf + c][...], vt, preferred_element_type=F32))

        for rb in range(TQ // CHUNK):
            rows = pl.ds(rb * CHUNK, CHUNK)
            o = (acc_ref[0][rows, :] * lane_rep(1.0 / l_ref[0][rows, :], width)
                 - acc_ref[1][rows, :] * lane_rep(lam * (1.0 / l_ref[1][rows, :]), width))
            o_ref[pl.ds(qi * TQ + rb * CHUNK, CHUNK), :] = _rms(o, sub_g, SUBLN_EPS).astype(o_ref.dtype)


def _diff_attention(qkv, lam_rows, sub_g, lambda_init):
    assert TQ == TK
    head_w = 2 * HEAD_DIM
    return pl.pallas_call(
        functools.partial(_attn_kernel, lambda_init=lambda_init),
        grid=(BATCH, N_HEADS),
        in_specs=[
            pl.BlockSpec((SUBLANES, HEAD_DIM), lambda b, h: (0, 0)),
            pl.BlockSpec((SEQ, head_w), lambda b, h: (b, h)),
            pl.BlockSpec((SEQ, head_w), lambda b, h: (b, N_HEADS + h)),
            pl.BlockSpec((SEQ, head_w), lambda b, h: (b, 2 * N_HEADS + h)),
            pl.BlockSpec((1, head_w), lambda b, h: (0, 0)),
        ],
        out_specs=pl.BlockSpec((SEQ, head_w), lambda b, h: (b, h)),
        out_shape=jax.ShapeDtypeStruct((N_TOK, D_MODEL), BF16),
        scratch_shapes=(
            [pltpu.VMEM((TQ, LANES), F32)] * 2
            + [pltpu.VMEM((TQ, LANES), F32)] * 2
            + [pltpu.VMEM((TQ, head_w), F32)] * 2
            + [pltpu.VMEM((TQ, TK), F32)] * 4
            + [pltpu.VMEM((TQ, TK), BF16)] * 4
            + [pltpu.VMEM((TQ, LANES), F32)] * 4),
        compiler_params=pltpu.CompilerParams(
            dimension_semantics=("arbitrary", "arbitrary"),
            vmem_limit_bytes=VMEM_LIMIT),
        name="diff_attention",
    )(lam_rows, qkv, qkv, qkv, sub_g)


def _router_params(wr_g, br_g, wr_e, br_e):
    w = jnp.zeros((D_MODEL, ROUTER_COLS), F32)
    w = w.at[:, 0:N_GROUPS].set(wr_g).at[:, E_COL0:E_COL0 + N_EXPERTS].set(wr_e)
    b = jnp.full((1, ROUTER_COLS), NEG_INF, F32)
    b = b.at[0, 0:N_GROUPS].set(br_g).at[0, E_COL0:E_COL0 + N_EXPERTS].set(br_e)
    w_hi = w.astype(BF16)
    w_lo = (w - w_hi.astype(F32)).astype(BF16)
    return jnp.stack([w_hi, w_lo]), b


def kernel(x, norm_mix_g, norm_ffn_g, pool_w, pool_scale, attn_wq, attn_wk, attn_wv, attn_wo,
           lam_q1, lam_k1, lam_q2, lam_k2, attn_sub_g, router_g_w, router_g_b, router_e_w,
           router_e_b, exp_w_gate, exp_w_up, exp_w_down, final_g):
    x2d = x.reshape(N_TOK, D_MODEL)
    row = lambda v: v.reshape(1, -1)

    wr0, br0 = _router_params(router_g_w[0], router_g_b[0], router_e_w[0], router_e_b[0])
    x1, h, ids, wts = _pool_layer(x2d, row(norm_mix_g[0]), pool_w[0].astype(BF16),
                                  row(pool_scale[0]), row(norm_ffn_g[0]), wr0, br0)
    yg = _moe_grouped(0, _route_plan(ids, wts), h, exp_w_gate, exp_w_up, exp_w_down)
    x2, hn = _combine_norm(x1, yg, row(norm_mix_g[1]), emit_sum=True, norm_dtype=BF16)

    lambda_init = 0.8 - 0.6 * math.exp(-0.3 * 1)
    w_qkv = jnp.concatenate([attn_wq[0], attn_wk[0], attn_wv[0]], axis=1).astype(BF16)
    q_scale = HEAD_DIM ** -0.5 * math.log2(math.e)
    qkv = _matmul(hn, w_qkv, BF16, D_MODEL, first_block_scale=q_scale)
    lam_rows = jnp.zeros((SUBLANES, HEAD_DIM), F32)
    lam_rows = lam_rows.at[0].set(lam_q1[0]).at[1].set(lam_k1[0])
    lam_rows = lam_rows.at[2].set(lam_q2[0]).at[3].set(lam_k2[0])
    att = _diff_attention(qkv, lam_rows, row(attn_sub_g[0]), lambda_init)
    wr1, br1 = _router_params(router_g_w[1], router_g_b[1], router_e_w[1], router_e_b[1])
    x3, h, ids, wts = _proj_router(x2, att, attn_wo[0].astype(BF16), row(norm_ffn_g[1]),
                                   wr1, br1)
    yg = _moe_grouped(1, _route_plan(ids, wts), h, exp_w_gate, exp_w_up, exp_w_down)
    (out,) = _combine_norm(x3, yg, row(final_g), emit_sum=False, norm_dtype=F32)
    return out.reshape(BATCH, SEQ, D_MODEL)
```

```python
import functools
import math

import jax
import jax.numpy as jnp
from jax import lax
from jax.experimental import pallas as pl
from jax.experimental.pallas import tpu as pltpu

F32 = jnp.float32
BF16 = jnp.bfloat16
I32 = jnp.int32

D_MODEL = 2048
BATCH = 8
SEQ = 2048
N_TOK = BATCH * SEQ
CHUNK = 64
POOL_WINDOWS = (2, 4, 8, 16)
POOL_CG = D_MODEL // len(POOL_WINDOWS)
POOL_HALO = 16
N_HEADS = 8
HEAD_DIM = 128
N_GROUPS = 4
EXPERTS_PER_GROUP = 8
N_EXPERTS = 32
EXPERT_FF = 512
NORM_EPS = 1e-6
SUBLN_EPS = 1e-5
NEG_INF = -1e30

SUBLANES = 8
LANES = 128
RPT = D_MODEL // LANES
RPT_PAD = RPT + 1
MXU_K = 256
ROUTER_COLS = LANES
E_COL0 = 8

TS = 256
TM = 256
N_SORT = 2 * N_TOK + N_EXPERTS * TM
N_TILES = N_SORT // TM
TQ = 512
TK = 512
MM_TM = 512
VMEM_LIMIT = 56 * 1024 * 1024


def _rms(x, g, eps):
    ms = jnp.mean(x * x, axis=-1, keepdims=True)
    return x * lax.rsqrt(ms + eps) * g


def _to_rows(ref, val, n, pitch=RPT):
    for c in range(RPT):
        ref[pl.ds(c, n, stride=pitch), :] = val[:, c * LANES:(c + 1) * LANES]


def _from_rows(ref, n, lo=0, hi=RPT, pitch=RPT):
    return jnp.concatenate(
        [ref[pl.ds(c, n, stride=pitch), :] for c in range(lo, hi)], axis=-1)


def _router_tail(x_new, gffn_ref, wr_ref, br_ref, h_ref, ids_ref, wts_ref):
    h = _rms(x_new, gffn_ref[...], NORM_EPS)
    _to_rows(h_ref, h, x_new.shape[0])
    h_hi = h.astype(BF16)
    h_lo = (h - h_hi.astype(F32)).astype(BF16)
    w_hi = wr_ref[0]
    w_lo = wr_ref[1]
    logits = (jnp.dot(h_hi, w_hi, preferred_element_type=F32)
              + jnp.dot(h_hi, w_lo, preferred_element_type=F32)
              + jnp.dot(h_lo, w_hi, preferred_element_type=F32))
    logits = logits + br_ref[...]
    lt = logits.T
    gl = lt[0:SUBLANES, :]
    io8 = lax.broadcasted_iota(I32, gl.shape, 0)
    gmax = jnp.max(gl, axis=0, keepdims=True)
    gtop = jnp.min(jnp.where(gl == gmax, io8, SUBLANES), axis=0, keepdims=True)
    p_g = 1.0 / jnp.sum(jnp.exp(gl - gmax), axis=0, keepdims=True)
    esel = lt[E_COL0:E_COL0 + EXPERTS_PER_GROUP, :]
    for gi in range(1, N_GROUPS):
        lo = E_COL0 + gi * EXPERTS_PER_GROUP
        esel = jnp.where(gtop == gi, lt[lo:lo + EXPERTS_PER_GROUP, :], esel)
    m1 = jnp.max(esel, axis=0, keepdims=True)
    i1 = jnp.min(jnp.where(esel == m1, io8, SUBLANES), axis=0, keepdims=True)
    rest = jnp.where(io8 == i1, -jnp.inf, esel)
    m2 = jnp.max(rest, axis=0, keepdims=True)
    i2 = jnp.min(jnp.where(rest == m2, io8, SUBLANES), axis=0, keepdims=True)
    r = jnp.exp(m2 - m1)
    v1 = 1.0 / (1.0 + r)
    v2 = r / (1.0 + r)
    base = gtop * EXPERTS_PER_GROUP
    ids_ref[0:1, :] = base + i1
    ids_ref[1:2, :] = base + i2
    wts_ref[0:1, :] = p_g * v1
    wts_ref[1:2, :] = p_g * v2


def _router_out_shapes():
    return (jax.ShapeDtypeStruct((N_TOK * RPT, LANES), F32),
            jax.ShapeDtypeStruct((2, N_TOK), I32),
            jax.ShapeDtypeStruct((2, N_TOK), F32))


def _router_out_specs(tok_map):
    return (pl.BlockSpec((TS * RPT, LANES), lambda *a: (tok_map(*a), 0)),
            pl.BlockSpec((2, TS), lambda *a: (0, tok_map(*a))),
            pl.BlockSpec((2, TS), lambda *a: (0, tok_map(*a))))


def _pool_kernel(x_ref, halo_ref, gmix_ref, wpool_ref, pscale_ref, gffn_ref, wr_ref, br_ref,
                 x1_ref, h_ref, ids_ref, wts_ref, hs_ref):
    s = pl.program_id(1)
    xt = x_ref[...]
    g = gmix_ref[...]
    h = _rms(xt, g, NORM_EPS)
    hh = _rms(halo_ref[...], g, NORM_EPS)
    hs_ref[0:POOL_HALO, :] = jnp.where(s > 0, hh, 0.0)
    hs_ref[POOL_HALO:, :] = h
    pos = s * TS + lax.broadcasted_iota(I32, (TS, 1), 0)
    outs = []
    for gi, w in enumerate(POOL_WINDOWS):
        c0 = gi * POOL_CG
        he = hs_ref[:, c0:c0 + POOL_CG]
        acc = he
        sh = 1
        while sh < w:
            acc = acc + pltpu.roll(acc, sh, axis=0)
            sh *= 2
        inv_cnt = 1.0 / jnp.minimum(pos + 1, w).astype(F32)
        mix = acc[POOL_HALO:, :] * inv_cnt - he[POOL_HALO:, :]
        outs.append(jnp.dot(mix.astype(BF16), wpool_ref[gi], preferred_element_type=F32))
    x1 = xt + jnp.concatenate(outs, axis=-1) * pscale_ref[...]
    x1_ref[...] = x1
    _router_tail(x1, gffn_ref, wr_ref, br_ref, h_ref, ids_ref, wts_ref)


def _pool_layer(x2d, gmix, wpool_bf, pscale, gffn, wr2, br):
    n_s = SEQ // TS
    tok_map = lambda b, s: b * n_s + s
    halo_blocks = TS // POOL_HALO
    const2 = lambda b, s: (0, 0)
    return pl.pallas_call(
        _pool_kernel,
        grid=(BATCH, n_s),
        in_specs=[
            pl.BlockSpec((TS, D_MODEL), lambda b, s: (tok_map(b, s), 0)),
            pl.BlockSpec((POOL_HALO, D_MODEL),
                         lambda b, s: (jnp.maximum(tok_map(b, s) * halo_blocks - 1, 0), 0)),
            pl.BlockSpec((1, D_MODEL), const2),
            pl.BlockSpec((len(POOL_WINDOWS), POOL_CG, POOL_CG), lambda b, s: (0, 0, 0)),
            pl.BlockSpec((1, D_MODEL), const2),
            pl.BlockSpec((1, D_MODEL), const2),
            pl.BlockSpec((2, D_MODEL, ROUTER_COLS), lambda b, s: (0, 0, 0)),
            pl.BlockSpec((1, ROUTER_COLS), const2),
        ],
        out_specs=(pl.BlockSpec((TS, D_MODEL), lambda b, s: (tok_map(b, s), 0)),)
        + _router_out_specs(tok_map),
        out_shape=(jax.ShapeDtypeStruct((N_TOK, D_MODEL), F32),) + _router_out_shapes(),
        scratch_shapes=[pltpu.VMEM((TS + POOL_HALO, D_MODEL), F32)],
        compiler_params=pltpu.CompilerParams(
            dimension_semantics=("arbitrary", "arbitrary"), vmem_limit_bytes=VMEM_LIMIT),
        name="pool_mixer_router",
    )(x2d, x2d, gmix, wpool_bf, pscale, gffn, wr2, br)


def _proj_router_kernel(x_ref, a_ref, wo_ref, gffn_ref, wr_ref, br_ref,
                        xo_ref, h_ref, ids_ref, wts_ref):
    x_new = x_ref[...] + jnp.dot(a_ref[...], wo_ref[...], preferred_element_type=F32)
    xo_ref[...] = x_new
    _router_tail(x_new, gffn_ref, wr_ref, br_ref, h_ref, ids_ref, wts_ref)


def _proj_router(x2d, att, wo_bf, gffn, wr2, br):
    tok_map = lambda i: i
    const2 = lambda i: (0, 0)
    return pl.pallas_call(
        _proj_router_kernel,
        grid=(N_TOK // TS,),
        in_specs=[
            pl.BlockSpec((TS, D_MODEL), lambda i: (i, 0)),
            pl.BlockSpec((TS, D_MODEL), lambda i: (i, 0)),
            pl.BlockSpec((D_MODEL, D_MODEL), const2),
            pl.BlockSpec((1, D_MODEL), const2),
            pl.BlockSpec((2, D_MODEL, ROUTER_COLS), lambda i: (0, 0, 0)),
            pl.BlockSpec((1, ROUTER_COLS), const2),
        ],
        out_specs=(pl.BlockSpec((TS, D_MODEL), lambda i: (i, 0)),) + _router_out_specs(tok_map),
        out_shape=(jax.ShapeDtypeStruct((N_TOK, D_MODEL), F32),) + _router_out_shapes(),
        compiler_params=pltpu.CompilerParams(
            dimension_semantics=("arbitrary",), vmem_limit_bytes=VMEM_LIMIT),
        name="proj_router",
    )(x2d, att, wo_bf, gffn, wr2, br)


def _moe_kernel(te_ref, nv_ref, src0_ref, srcn_ref, dstp_ref, h_ref, wg_ref, wu_ref, wd_ref,
                gate_ref, yg_ref, xbuf, ybuf, wg_bf, wu_bf, wd_bf, sem_g, sem_s):
    i = pl.program_id(0)
    nv = nv_ref[0]
    slot = i % 2
    other = 1 - slot
    k_chunks = D_MODEL // MXU_K
    n_chunks = D_MODEL // EXPERT_FF
    gathers_per_chunk = TM // k_chunks
    scatters_per_chunk = TM // n_chunks

    def hbm_token(ref, t):
        return ref.at[pl.ds(pl.multiple_of(t * RPT, RPT), RPT), :]

    def gather_start(idx_ref, s, r):
        pltpu.make_async_copy(hbm_token(h_ref, idx_ref[0, 0, r]),
                              xbuf.at[s, pl.ds(r * RPT_PAD, RPT), :],
                              sem_g.at[s]).start(priority=1)

    def gather_wait(s):
        pltpu.make_async_copy(h_ref.at[pl.ds(0, TM * RPT), :],
                              xbuf.at[s, pl.ds(0, TM * RPT), :], sem_g.at[s]).wait()

    def scatter_start(s, r):
        pltpu.make_async_copy(ybuf.at[s, pl.ds(r * RPT_PAD, RPT), :],
                              hbm_token(yg_ref, dstp_ref[0, 0, r]),
                              sem_s.at[s]).start(priority=r % 2)

    def scatter_wait(s):
        pltpu.make_async_copy(ybuf.at[s, pl.ds(0, TM * RPT), :],
                              yg_ref.at[pl.ds(0, TM * RPT), :], sem_s.at[s]).wait()

    @pl.when(i == 0)
    def _():
        for r in range(TM):
            gather_start(src0_ref, 0, r)

    @pl.when(i >= 2)
    def _():
        scatter_wait(slot)

    @pl.when(i < nv)
    def _():
        e = te_ref[i]
        e_prev = te_ref[jnp.maximum(i - 1, 0)]

        @pl.when((i == 0) | (e != e_prev))
        def _():
            wg_bf[...] = wg_ref[0, 0].astype(BF16)
            wu_bf[...] = wu_ref[0, 0].astype(BF16)
            wd_bf[...] = wd_ref[0, 0].astype(BF16)

        gather_wait(slot)

    def expert_mlp(send_previous):
        g = jnp.zeros((TM, EXPERT_FF), F32)
        u = jnp.zeros((TM, EXPERT_FF), F32)
        per_pass = MXU_K // LANES
        for c in range(k_chunks):
            for r in range(c * gathers_per_chunk, (c + 1) * gathers_per_chunk):
                gather_start(srcn_ref, other, r)
            xk = _from_rows(xbuf.at[slot], TM, c * per_pass, (c + 1) * per_pass,
                            pitch=RPT_PAD).astype(BF16)
            g = g + jnp.dot(xk, wg_bf[c * MXU_K:(c + 1) * MXU_K, :], preferred_element_type=F32)
            u = u + jnp.dot(xk, wu_bf[c * MXU_K:(c + 1) * MXU_K, :], preferred_element_type=F32)
        hid = (g / (1.0 + jnp.exp(-g)) * u).astype(BF16)
        gate = gate_ref[...]
        lanes_per_chunk = EXPERT_FF // LANES
        for n in range(n_chunks):
            if send_previous:
                for r in range(n * scatters_per_chunk, (n + 1) * scatters_per_chunk):
                    scatter_start(other, r)
            y = jnp.dot(hid, wd_bf[:, n * EXPERT_FF:(n + 1) * EXPERT_FF],
                        preferred_element_type=F32) * gate
            for c in range(lanes_per_chunk):
                ybuf[slot, pl.ds(n * lanes_per_chunk + c, TM, stride=RPT_PAD), :] = (
                    y[:, c * LANES:(c + 1) * LANES])

    pl.when((i < nv) & (i == 0))(functools.partial(expert_mlp, False))
    pl.when((i < nv) & (i > 0))(functools.partial(expert_mlp, True))

    @pl.when(i >= nv)
    def _():
        for r in range(TM):
            scatter_start(other, r)

    @pl.when(i == nv - 1)
    def _():
        gather_wait(other)

    @pl.when(i == N_TILES)
    def _():
        scatter_wait(other)


def _moe_grouped(layer, plan, h, w_gate, w_up, w_down):
    tile_expert, n_valid, src_tok, dst_row, gate_col = plan
    last = lambda nv: nv[0] - 1
    idx_blk = lambda f: pl.BlockSpec((1, 1, TM), f, memory_space=pltpu.SMEM)
    w_spec = lambda shape: pl.BlockSpec(
        (1, 1) + shape, lambda i, te, nv: (layer, te[jnp.minimum(i, last(nv))], 0, 0))
    grid_spec = pltpu.PrefetchScalarGridSpec(
        num_scalar_prefetch=2,
        grid=(N_TILES + 1,),
        in_specs=[
            idx_blk(lambda i, te, nv: (0, 0, 0)),
            idx_blk(lambda i, te, nv: (jnp.minimum(i + 1, last(nv)), 0, 0)),
            idx_blk(lambda i, te, nv: (jnp.maximum(i - 1, 0), 0, 0)),
            pl.BlockSpec(memory_space=pl.ANY),
            w_spec((D_MODEL, EXPERT_FF)),
            w_spec((D_MODEL, EXPERT_FF)),
            w_spec((EXPERT_FF, D_MODEL)),
            pl.BlockSpec((TM, 1), lambda i, te, nv: (jnp.minimum(i, last(nv)), 0)),
        ],
        out_specs=pl.BlockSpec(memory_space=pl.ANY),
        scratch_shapes=[pltpu.VMEM((2, TM * RPT_PAD, LANES), F32),
                        pltpu.VMEM((2, TM * RPT_PAD, LANES), F32),
                        pltpu.VMEM((D_MODEL, EXPERT_FF), BF16),
                        pltpu.VMEM((D_MODEL, EXPERT_FF), BF16),
                        pltpu.VMEM((EXPERT_FF, D_MODEL), BF16),
                        pltpu.SemaphoreType.DMA((2,)),
                        pltpu.SemaphoreType.DMA((2,))],
    )
    idx3 = lambda v: v.reshape(N_TILES, 1, TM)
    return pl.pallas_call(
        _moe_kernel,
        grid_spec=grid_spec,
        out_shape=jax.ShapeDtypeStruct((N_SORT * RPT, LANES), F32),
        compiler_params=pltpu.CompilerParams(
            dimension_semantics=("arbitrary",), vmem_limit_bytes=VMEM_LIMIT),
        name="moe_grouped",
    )(tile_expert, n_valid, idx3(src_tok), idx3(src_tok), idx3(dst_row), h,
      w_gate, w_up, w_down, gate_col)


def _route_plan(ids, wts):
    e = ids.reshape(-1)
    w = wts.reshape(-1)
    onehot = (e[:, None] == jnp.arange(N_EXPERTS, dtype=I32)[None, :]).astype(I32)
    csum = jnp.cumsum(onehot, axis=0)
    rank = jnp.sum(onehot * csum, axis=1) - 1
    counts = csum[-1]
    padded = (counts + TM - 1) // TM * TM
    ends = jnp.cumsum(padded)
    pos = (ends - padded)[e] + rank
    pair = jnp.full((N_SORT,), -1, I32).at[pos].set(jnp.arange(2 * N_TOK, dtype=I32))
    is_pad = pair < 0
    pad_row = 2 * N_TOK + jnp.cumsum(is_pad.astype(I32)) - 1
    src_tok = jnp.where(is_pad, 0, pair % N_TOK)
    gate = jnp.where(is_pad, 0.0, w[jnp.maximum(pair, 0)])
    dst_row = jnp.where(is_pad, pad_row, pair)
    n_valid = (ends[-1] // TM).astype(I32)
    tile_start = jnp.minimum(jnp.arange(N_TILES, dtype=I32), n_valid - 1) * TM
    n_done = jnp.sum((ends[None, :] <= tile_start[:, None]).astype(I32), axis=1)
    tile_expert = jnp.minimum(n_done, N_EXPERTS - 1)
    return tile_expert, n_valid.reshape(1), src_tok, dst_row, gate.reshape(N_SORT, 1)


def _combine_norm_kernel(x_ref, y0_ref, y1_ref, g_ref, *out_refs, emit_sum):
    x_new = x_ref[...] + _from_rows(y0_ref, TS) + _from_rows(y1_ref, TS)
    normed = _rms(x_new, g_ref[...], NORM_EPS)
    if emit_sum:
        out_refs[0][...] = x_new
        out_refs[1][...] = normed.astype(out_refs[1].dtype)
    else:
        out_refs[0][...] = normed.astype(out_refs[0].dtype)


def _combine_norm(x2d, yg, g, *, emit_sum, norm_dtype):
    n_blk = N_TOK // TS
    tok_spec = pl.BlockSpec((TS, D_MODEL), lambda i: (i, 0))
    out_specs = [tok_spec]
    out_shape = [jax.ShapeDtypeStruct((N_TOK, D_MODEL), norm_dtype)]
    if emit_sum:
        out_specs = [tok_spec, tok_spec]
        out_shape = [jax.ShapeDtypeStruct((N_TOK, D_MODEL), F32)] + out_shape
    return pl.pallas_call(
        functools.partial(_combine_norm_kernel, emit_sum=emit_sum),
        grid=(n_blk,),
        in_specs=[
            tok_spec,
            pl.BlockSpec((TS * RPT, LANES), lambda i: (i, 0)),
            pl.BlockSpec((TS * RPT, LANES), lambda i: (n_blk + i, 0)),
            pl.BlockSpec((1, D_MODEL), lambda i: (0, 0)),
        ],
        out_specs=tuple(out_specs),
        out_shape=tuple(out_shape),
        compiler_params=pltpu.CompilerParams(
            dimension_semantics=("arbitrary",), vmem_limit_bytes=VMEM_LIMIT),
        name="combine_norm",
    )(x2d, yg, yg, g)


def _mm_kernel(a_ref, b_ref, o_ref, *, first_block_scale):
    acc = jnp.dot(a_ref[...], b_ref[...], preferred_element_type=F32)
    if first_block_scale is not None:
        acc = acc * jnp.where(pl.program_id(0) == 0, first_block_scale, 1.0)
    o_ref[...] = acc.astype(o_ref.dtype)


def _matmul(a, b, out_dtype, tn, first_block_scale=None):
    m, k = a.shape
    n = b.shape[1]
    return pl.pallas_call(
        functools.partial(_mm_kernel, first_block_scale=first_block_scale),
        grid=(n // tn, m // MM_TM),
        in_specs=[pl.BlockSpec((MM_TM, k), lambda j, i: (i, 0)),
                  pl.BlockSpec((k, tn), lambda j, i: (0, j))],
        out_specs=pl.BlockSpec((MM_TM, tn), lambda j, i: (i, j)),
        out_shape=jax.ShapeDtypeStruct((m, n), out_dtype),
        compiler_params=pltpu.CompilerParams(
            dimension_semantics=("arbitrary", "arbitrary"), vmem_limit_bytes=VMEM_LIMIT),
        name="dense_matmul",
    )(a, b)


def _attn_kernel(lam_ref, q_ref, k_ref, v_ref, subg_ref, o_ref, *scratch, lambda_init):
    m_ref, l_ref, acc_ref, s_ref, p_ref, a_ref = (scratch[2 * n:2 * n + 2] for n in range(6))
    lane_rep = lambda a, width: jnp.concatenate([a] * (width // LANES), axis=1)
    lv = lam_ref[...]
    lam = (jnp.exp(jnp.sum(lv[0:1] * lv[1:2], axis=-1, keepdims=True))
           - jnp.exp(jnp.sum(lv[2:3] * lv[3:4], axis=-1, keepdims=True)) + lambda_init)
    width = 2 * HEAD_DIM
    sub_g = subg_ref[...] * (1.0 - lambda_init)

    def query_tile(qi, carry):
        q0 = pl.multiple_of(qi * TQ, TQ)
        q = q_ref[pl.ds(q0, TQ), :]
        qs = (q[:, :HEAD_DIM], q[:, HEAD_DIM:])
        for c in range(2):
            m_ref[c][...] = jnp.full(m_ref[c].shape, NEG_INF, F32)
            l_ref[c][...] = jnp.zeros(l_ref[c].shape, F32)
            acc_ref[c][...] = jnp.zeros(acc_ref[c].shape, F32)

        def tile_update(j, diagonal):
            k0 = pl.multiple_of(j * TK, TK)
            kt = k_ref[pl.ds(k0, TK), :]
            vt = v_ref[pl.ds(k0, TK), :]
            for c in range(2):
                kc = kt[:, c * HEAD_DIM:(c + 1) * HEAD_DIM]
                s_ref[c][...] = lax.dot_general(qs[c], kc, (((1,), (1,)), ((), ())),
                                                preferred_element_type=F32)
            for c in range(2):
                for rb in range(TQ // CHUNK):
                    rows = pl.ds(rb * CHUNK, CHUNK)
                    ncol = TK
                    if diagonal:
                        seen = (rb + 1) * CHUNK
                        ncol = min(TK, -(-seen // LANES) * LANES)
                    s = s_ref[c][rows, :ncol]
                    if diagonal and seen < ncol:
                        col = lax.broadcasted_iota(I32, (CHUNK, ncol), 1)
                        s = jnp.where(col < seen, s, NEG_INF)
                    m_old = m_ref[c][rows, :]
                    m_new = jnp.maximum(m_old, jnp.max(s, axis=-1, keepdims=True))
                    alpha = jnp.exp2(m_old - m_new)
                    p = jnp.exp2(s - lane_rep(m_new, ncol))
                    l_ref[c][rows, :] = (alpha * l_ref[c][rows, :]
                                         + jnp.sum(p, axis=-1, keepdims=True))
                    m_ref[c][rows, :] = m_new
                    a_ref[c][rows, :] = alpha
                    p_ref[c][rows, :ncol] = p.astype(BF16)
                    if ncol < TK:
                        p_ref[c][rows, ncol:] = jnp.zeros((CHUNK, TK - ncol), BF16)
                acc_ref[c][...] = (lane_rep(a_ref[c][...], width) * acc_ref[c][...]
                                   + jnp.dot(p_ref[c][...], vt, preferred_element_type=F32))

        def full_body(j, c):
            tile_update(j, False)
            return c

        lax.fori_loop(0, qi, full_body, 0)
        tile_update(qi, True)

        for rb in range(TQ // CHUNK):
            rows = pl.ds(rb * CHUNK, CHUNK)
            o = (acc_ref[0][rows, :] * lane_rep(1.0 / l_ref[0][rows, :], width)
                 - acc_ref[1][rows, :] * lane_rep(lam * (1.0 / l_ref[1][rows, :]), width))
            o_ref[pl.ds(pl.multiple_of(q0 + rb * CHUNK, CHUNK), CHUNK), :] = _rms(
                o, sub_g, SUBLN_EPS).astype(o_ref.dtype)
        return carry

    lax.fori_loop(0, SEQ // TQ, query_tile, 0)


def _diff_attention(qkv, lam_rows, sub_g, lambda_init):
    assert TQ == TK
    head_w = 2 * HEAD_DIM
    return pl.pallas_call(
        functools.partial(_attn_kernel, lambda_init=lambda_init),
        grid=(BATCH, N_HEADS),
        in_specs=[
            pl.BlockSpec((SUBLANES, HEAD_DIM), lambda b, h: (0, 0)),
            pl.BlockSpec((SEQ, head_w), lambda b, h: (b, h)),
            pl.BlockSpec((SEQ, head_w), lambda b, h: (b, N_HEADS + h)),
            pl.BlockSpec((SEQ, head_w), lambda b, h: (b, 2 * N_HEADS + h)),
            pl.BlockSpec((1, head_w), lambda b, h: (0, 0)),
        ],
        out_specs=pl.BlockSpec((SEQ, head_w), lambda b, h: (b, h)),
        out_shape=jax.ShapeDtypeStruct((N_TOK, D_MODEL), BF16),
        scratch_shapes=(
            [pltpu.VMEM((TQ, LANES), F32)] * 2
            + [pltpu.VMEM((TQ, LANES), F32)] * 2
            + [pltpu.VMEM((TQ, head_w), F32)] * 2
            + [pltpu.VMEM((TQ, TK), F32)] * 2
            + [pltpu.VMEM((TQ, TK), BF16)] * 2
            + [pltpu.VMEM((TQ, LANES), F32)] * 2),
        compiler_params=pltpu.CompilerParams(
            dimension_semantics=("arbitrary", "arbitrary"),
            vmem_limit_bytes=VMEM_LIMIT),
        name="diff_attention",
    )(lam_rows, qkv, qkv, qkv, sub_g)


def _router_params(wr_g, br_g, wr_e, br_e):
    w = jnp.zeros((D_MODEL, ROUTER_COLS), F32)
    w = w.at[:, 0:N_GROUPS].set(wr_g).at[:, E_COL0:E_COL0 + N_EXPERTS].set(wr_e)
    b = jnp.full((1, ROUTER_COLS), NEG_INF, F32)
    b = b.at[0, 0:N_GROUPS].set(br_g).at[0, E_COL0:E_COL0 + N_EXPERTS].set(br_e)
    w_hi = w.astype(BF16)
    w_lo = (w - w_hi.astype(F32)).astype(BF16)
    return jnp.stack([w_hi, w_lo]), b


def kernel(x, norm_mix_g, norm_ffn_g, pool_w, pool_scale, attn_wq, attn_wk, attn_wv, attn_wo,
           lam_q1, lam_k1, lam_q2, lam_k2, attn_sub_g, router_g_w, router_g_b, router_e_w,
           router_e_b, exp_w_gate, exp_w_up, exp_w_down, final_g):
    x2d = x.reshape(N_TOK, D_MODEL)
    row = lambda v: v.reshape(1, -1)

    wr0, br0 = _router_params(router_g_w[0], router_g_b[0], router_e_w[0], router_e_b[0])
    x1, h, ids, wts = _pool_layer(x2d, row(norm_mix_g[0]), pool_w[0].astype(BF16),
                                  row(pool_scale[0]), row(norm_ffn_g[0]), wr0, br0)
    yg = _moe_grouped(0, _route_plan(ids, wts), h, exp_w_gate, exp_w_up, exp_w_down)
    x2, hn = _combine_norm(x1, yg, row(norm_mix_g[1]), emit_sum=True, norm_dtype=BF16)

    lambda_init = 0.8 - 0.6 * math.exp(-0.3 * 1)
    w_qkv = jnp.concatenate([attn_wq[0], attn_wk[0], attn_wv[0]], axis=1).astype(BF16)
    q_scale = HEAD_DIM ** -0.5 * math.log2(math.e)
    qkv = _matmul(hn, w_qkv, BF16, D_MODEL, first_block_scale=q_scale)
    lam_rows = jnp.zeros((SUBLANES, HEAD_DIM), F32)
    lam_rows = lam_rows.at[0].set(lam_q1[0]).at[1].set(lam_k1[0])
    lam_rows = lam_rows.at[2].set(lam_q2[0]).at[3].set(lam_k2[0])
    att = _diff_attention(qkv, lam_rows, row(attn_sub_g[0]), lambda_init)
    wr1, br1 = _router_params(router_g_w[1], router_g_b[1], router_e_w[1], router_e_b[1])
    x3, h, ids, wts = _proj_router(x2, att, attn_wo[0].astype(BF16), row(norm_ffn_g[1]),
                                   wr1, br1)
    yg = _moe_grouped(1, _route_plan(ids, wts), h, exp_w_gate, exp_w_up, exp_w_down)
    (out,) = _combine_norm(x3, yg, row(final_g), emit_sum=False, norm_dtype=F32)
    return out.reshape(BATCH, SEQ, D_MODEL)
```

```python
import functools
import math

import jax
import jax.numpy as jnp
from jax import lax
from jax.experimental import pallas as pl
from jax.experimental.pallas import tpu as pltpu

F32 = jnp.float32
BF16 = jnp.bfloat16
I32 = jnp.int32

D_MODEL = 2048
BATCH = 8
SEQ = 2048
N_TOK = BATCH * SEQ
CHUNK = 64
POOL_WINDOWS = (2, 4, 8, 16)
POOL_CG = D_MODEL // len(POOL_WINDOWS)
POOL_HALO = 16
N_HEADS = 8
HEAD_DIM = 128
N_GROUPS = 4
EXPERTS_PER_GROUP = 8
N_EXPERTS = 32
EXPERT_FF = 512
NORM_EPS = 1e-6
SUBLN_EPS = 1e-5
NEG_INF = -1e30

SUBLANES = 8
LANES = 128
RPT = D_MODEL // LANES
RPT_PAD = RPT + 1
MXU_K = 256
ROUTER_COLS = LANES
E_COL0 = 8

TS = 256
TM = 256
N_SORT = 2 * N_TOK + N_EXPERTS * TM
N_TILES = N_SORT // TM
TQ = 512
TK = 512
MM_TM = 512
VMEM_LIMIT = 56 * 1024 * 1024


def _rms(x, g, eps):
    ms = jnp.mean(x * x, axis=-1, keepdims=True)
    return x * lax.rsqrt(ms + eps) * g


def _to_rows(ref, val, n, pitch=RPT):
    for c in range(RPT):
        ref[pl.ds(c, n, stride=pitch), :] = val[:, c * LANES:(c + 1) * LANES]


def _from_rows(ref, n, lo=0, hi=RPT, pitch=RPT):
    return jnp.concatenate(
        [ref[pl.ds(c, n, stride=pitch), :] for c in range(lo, hi)], axis=-1)


def _router_tail(x_new, gffn_ref, wr_ref, br_ref, h_ref, ids_ref, wts_ref):
    h = _rms(x_new, gffn_ref[...], NORM_EPS)
    _to_rows(h_ref, h, x_new.shape[0])
    h_hi = h.astype(BF16)
    h_lo = (h - h_hi.astype(F32)).astype(BF16)
    w_hi = wr_ref[0]
    w_lo = wr_ref[1]
    logits = (jnp.dot(h_hi, w_hi, preferred_element_type=F32)
              + jnp.dot(h_hi, w_lo, preferred_element_type=F32)
              + jnp.dot(h_lo, w_hi, preferred_element_type=F32))
    logits = logits + br_ref[...]
    lt = logits.T
    gl = lt[0:SUBLANES, :]
    io8 = lax.broadcasted_iota(I32, gl.shape, 0)
    gmax = jnp.max(gl, axis=0, keepdims=True)
    gtop = jnp.min(jnp.where(gl == gmax, io8, SUBLANES), axis=0, keepdims=True)
    p_g = 1.0 / jnp.sum(jnp.exp(gl - gmax), axis=0, keepdims=True)
    esel = lt[E_COL0:E_COL0 + EXPERTS_PER_GROUP, :]
    for gi in range(1, N_GROUPS):
        lo = E_COL0 + gi * EXPERTS_PER_GROUP
        esel = jnp.where(gtop == gi, lt[lo:lo + EXPERTS_PER_GROUP, :], esel)
    m1 = jnp.max(esel, axis=0, keepdims=True)
    i1 = jnp.min(jnp.where(esel == m1, io8, SUBLANES), axis=0, keepdims=True)
    rest = jnp.where(io8 == i1, -jnp.inf, esel)
    m2 = jnp.max(rest, axis=0, keepdims=True)
    i2 = jnp.min(jnp.where(rest == m2, io8, SUBLANES), axis=0, keepdims=True)
    r = jnp.exp(m2 - m1)
    v1 = 1.0 / (1.0 + r)
    v2 = r / (1.0 + r)
    base = gtop * EXPERTS_PER_GROUP
    ids_ref[0:1, :] = base + i1
    ids_ref[1:2, :] = base + i2
    wts_ref[0:1, :] = p_g * v1
    wts_ref[1:2, :] = p_g * v2


def _router_out_shapes():
    return (jax.ShapeDtypeStruct((N_TOK * RPT, LANES), F32),
            jax.ShapeDtypeStruct((2, N_TOK), I32),
            jax.ShapeDtypeStruct((2, N_TOK), F32))


def _router_out_specs(tok_map):
    return (pl.BlockSpec((TS * RPT, LANES), lambda *a: (tok_map(*a), 0)),
            pl.BlockSpec((2, TS), lambda *a: (0, tok_map(*a))),
            pl.BlockSpec((2, TS), lambda *a: (0, tok_map(*a))))


def _pool_kernel(x_ref, halo_ref, gmix_ref, wpool_ref, pscale_ref, gffn_ref, wr_ref, br_ref,
                 x1_ref, h_ref, ids_ref, wts_ref, hs_ref):
    s = pl.program_id(1)
    xt = x_ref[...]
    g = gmix_ref[...]
    h = _rms(xt, g, NORM_EPS)
    hh = _rms(halo_ref[...], g, NORM_EPS)
    hs_ref[0:POOL_HALO, :] = jnp.where(s > 0, hh, 0.0)
    hs_ref[POOL_HALO:, :] = h
    pos = s * TS + lax.broadcasted_iota(I32, (TS, 1), 0)
    outs = []
    for gi, w in enumerate(POOL_WINDOWS):
        c0 = gi * POOL_CG
        he = hs_ref[:, c0:c0 + POOL_CG]
        acc = he
        sh = 1
        while sh < w:
            acc = acc + pltpu.roll(acc, sh, axis=0)
            sh *= 2
        inv_cnt = 1.0 / jnp.minimum(pos + 1, w).astype(F32)
        mix = acc[POOL_HALO:, :] * inv_cnt - he[POOL_HALO:, :]
        outs.append(jnp.dot(mix.astype(BF16), wpool_ref[gi], preferred_element_type=F32))
    x1 = xt + jnp.concatenate(outs, axis=-1) * pscale_ref[...]
    x1_ref[...] = x1
    _router_tail(x1, gffn_ref, wr_ref, br_ref, h_ref, ids_ref, wts_ref)


def _pool_layer(x2d, gmix, wpool_bf, pscale, gffn, wr2, br):
    n_s = SEQ // TS
    tok_map = lambda b, s: b * n_s + s
    halo_blocks = TS // POOL_HALO
    const2 = lambda b, s: (0, 0)
    return pl.pallas_call(
        _pool_kernel,
        grid=(BATCH, n_s),
        in_specs=[
            pl.BlockSpec((TS, D_MODEL), lambda b, s: (tok_map(b, s), 0)),
            pl.BlockSpec((POOL_HALO, D_MODEL),
                         lambda b, s: (jnp.maximum(tok_map(b, s) * halo_blocks - 1, 0), 0)),
            pl.BlockSpec((1, D_MODEL), const2),
            pl.BlockSpec((len(POOL_WINDOWS), POOL_CG, POOL_CG), lambda b, s: (0, 0, 0)),
            pl.BlockSpec((1, D_MODEL), const2),
            pl.BlockSpec((1, D_MODEL), const2),
            pl.BlockSpec((2, D_MODEL, ROUTER_COLS), lambda b, s: (0, 0, 0)),
            pl.BlockSpec((1, ROUTER_COLS), const2),
        ],
        out_specs=(pl.BlockSpec((TS, D_MODEL), lambda b, s: (tok_map(b, s), 0)),)
        + _router_out_specs(tok_map),
        out_shape=(jax.ShapeDtypeStruct((N_TOK, D_MODEL), F32),) + _router_out_shapes(),
        scratch_shapes=[pltpu.VMEM((TS + POOL_HALO, D_MODEL), F32)],
        compiler_params=pltpu.CompilerParams(
            dimension_semantics=("arbitrary", "arbitrary"), vmem_limit_bytes=VMEM_LIMIT),
        name="pool_mixer_router",
    )(x2d, x2d, gmix, wpool_bf, pscale, gffn, wr2, br)


def _proj_router_kernel(x_ref, a_ref, wo_ref, gffn_ref, wr_ref, br_ref,
                        xo_ref, h_ref, ids_ref, wts_ref):
    x_new = x_ref[...] + jnp.dot(a_ref[...], wo_ref[...], preferred_element_type=F32)
    xo_ref[...] = x_new
    _router_tail(x_new, gffn_ref, wr_ref, br_ref, h_ref, ids_ref, wts_ref)


def _proj_router(x2d, att, wo_bf, gffn, wr2, br):
    tok_map = lambda i: i
    const2 = lambda i: (0, 0)
    return pl.pallas_call(
        _proj_router_kernel,
        grid=(N_TOK // TS,),
        in_specs=[
            pl.BlockSpec((TS, D_MODEL), lambda i: (i, 0)),
            pl.BlockSpec((TS, D_MODEL), lambda i: (i, 0)),
            pl.BlockSpec((D_MODEL, D_MODEL), const2),
            pl.BlockSpec((1, D_MODEL), const2),
            pl.BlockSpec((2, D_MODEL, ROUTER_COLS), lambda i: (0, 0, 0)),
            pl.BlockSpec((1, ROUTER_COLS), const2),
        ],
        out_specs=(pl.BlockSpec((TS, D_MODEL), lambda i: (i, 0)),) + _router_out_specs(tok_map),
        out_shape=(jax.ShapeDtypeStruct((N_TOK, D_MODEL), F32),) + _router_out_shapes(),
        compiler_params=pltpu.CompilerParams(
            dimension_semantics=("arbitrary",), vmem_limit_bytes=VMEM_LIMIT),
        name="proj_router",
    )(x2d, att, wo_bf, gffn, wr2, br)


def _moe_kernel(te_ref, nv_ref, src0_ref, srcn_ref, h_ref, wg_ref, wu_ref, wd_ref,
                gate_ref, o_ref, xbuf, wg_bf, wu_bf, wd_bf, sem_g):
    i = pl.program_id(0)
    nv = nv_ref[0]
    slot = i % 2
    other = 1 - slot
    k_chunks = D_MODEL // MXU_K
    gathers_per_chunk = TM // k_chunks

    def gather_start(idx_ref, s, r):
        t = idx_ref[0, 0, r]
        pltpu.make_async_copy(h_ref.at[pl.ds(pl.multiple_of(t * RPT, RPT), RPT), :],
                              xbuf.at[s, pl.ds(r * RPT_PAD, RPT), :],
                              sem_g.at[s]).start(priority=r % 2)

    def gather_wait(s):
        pltpu.make_async_copy(h_ref.at[pl.ds(0, TM * RPT), :],
                              xbuf.at[s, pl.ds(0, TM * RPT), :], sem_g.at[s]).wait()

    @pl.when(i == 0)
    def _():
        for r in range(TM):
            gather_start(src0_ref, 0, r)

    @pl.when(i < nv)
    def _():
        e = te_ref[i]
        e_prev = te_ref[jnp.maximum(i - 1, 0)]

        @pl.when((i == 0) | (e != e_prev))
        def _():
            wg_bf[...] = wg_ref[0, 0].astype(BF16)
            wu_bf[...] = wu_ref[0, 0].astype(BF16)
            wd_bf[...] = wd_ref[0, 0].astype(BF16)

        gather_wait(slot)
        g = jnp.zeros((TM, EXPERT_FF), F32)
        u = jnp.zeros((TM, EXPERT_FF), F32)
        per_pass = MXU_K // LANES
        for c in range(k_chunks):
            for r in range(c * gathers_per_chunk, (c + 1) * gathers_per_chunk):
                gather_start(srcn_ref, other, r)
            xk = _from_rows(xbuf.at[slot], TM, c * per_pass, (c + 1) * per_pass,
                            pitch=RPT_PAD).astype(BF16)
            g = g + jnp.dot(xk, wg_bf[c * MXU_K:(c + 1) * MXU_K, :], preferred_element_type=F32)
            u = u + jnp.dot(xk, wu_bf[c * MXU_K:(c + 1) * MXU_K, :], preferred_element_type=F32)
        hid = (g / (1.0 + jnp.exp(-g)) * u).astype(BF16)
        y = jnp.dot(hid, wd_bf[...], preferred_element_type=F32)
        o_ref[...] = y * gate_ref[...]

    @pl.when(i >= nv)
    def _():
        o_ref[...] = jnp.zeros(o_ref.shape, o_ref.dtype)

    @pl.when(i == nv - 1)
    def _():
        gather_wait(other)


def _moe_grouped(layer, plan, h, w_gate, w_up, w_down):
    tile_expert, n_valid, src_tok, gate_col = plan
    last = lambda nv: nv[0] - 1
    idx_blk = lambda f: pl.BlockSpec((1, 1, TM), f, memory_space=pltpu.SMEM)
    w_spec = lambda shape: pl.BlockSpec(
        (1, 1) + shape, lambda i, te, nv: (layer, te[jnp.minimum(i, last(nv))], 0, 0))
    grid_spec = pltpu.PrefetchScalarGridSpec(
        num_scalar_prefetch=2,
        grid=(N_TILES,),
        in_specs=[
            idx_blk(lambda i, te, nv: (0, 0, 0)),
            idx_blk(lambda i, te, nv: (jnp.minimum(i + 1, last(nv)), 0, 0)),
            pl.BlockSpec(memory_space=pl.ANY),
            w_spec((D_MODEL, EXPERT_FF)),
            w_spec((D_MODEL, EXPERT_FF)),
            w_spec((EXPERT_FF, D_MODEL)),
            pl.BlockSpec((TM, 1), lambda i, te, nv: (jnp.minimum(i, last(nv)), 0)),
        ],
        out_specs=pl.BlockSpec((TM, D_MODEL), lambda i, te, nv: (i, 0)),
        scratch_shapes=[pltpu.VMEM((2, TM * RPT_PAD, LANES), F32),
                        pltpu.VMEM((D_MODEL, EXPERT_FF), BF16),
                        pltpu.VMEM((D_MODEL, EXPERT_FF), BF16),
                        pltpu.VMEM((EXPERT_FF, D_MODEL), BF16),
                        pltpu.SemaphoreType.DMA((2,))],
    )
    idx3 = lambda v: v.reshape(N_TILES, 1, TM)
    return pl.pallas_call(
        _moe_kernel,
        grid_spec=grid_spec,
        out_shape=jax.ShapeDtypeStruct((N_SORT, D_MODEL), F32),
        compiler_params=pltpu.CompilerParams(
            dimension_semantics=("arbitrary",), vmem_limit_bytes=VMEM_LIMIT),
        name="moe_grouped",
    )(tile_expert, n_valid, idx3(src_tok), idx3(src_tok), h, w_gate, w_up, w_down, gate_col)


def _route_plan(ids, wts):
    e = ids.reshape(-1)
    w = wts.reshape(-1)
    onehot = (e[:, None] == jnp.arange(N_EXPERTS, dtype=I32)[None, :]).astype(I32)
    csum = jnp.cumsum(onehot, axis=0)
    rank = jnp.sum(onehot * csum, axis=1) - 1
    counts = csum[-1]
    padded = (counts + TM - 1) // TM * TM
    ends = jnp.cumsum(padded)
    pos = (ends - padded)[e] + rank
    pair = jnp.full((N_SORT,), -1, I32).at[pos].set(jnp.arange(2 * N_TOK, dtype=I32))
    is_pad = pair < 0
    src_tok = jnp.where(is_pad, 0, pair % N_TOK)
    gate = jnp.where(is_pad, 0.0, w[jnp.maximum(pair, 0)])
    n_valid = (ends[-1] // TM).astype(I32)
    tile_start = jnp.minimum(jnp.arange(N_TILES, dtype=I32), n_valid - 1) * TM
    n_done = jnp.sum((ends[None, :] <= tile_start[:, None]).astype(I32), axis=1)
    tile_expert = jnp.minimum(n_done, N_EXPERTS - 1)
    plan = (tile_expert, n_valid.reshape(1), src_tok, gate.reshape(N_SORT, 1))
    return plan, pos.reshape(2, N_TOK)


def _combine_norm_kernel(p0f_ref, p1f_ref, p0n_ref, p1n_ref, x_ref, ys_ref, g_ref, *rest,
                         emit_sum):
    out_refs, (ybuf, sem) = rest[:-2], rest[-2:]
    i = pl.program_id(0)
    n_blk = pl.num_programs(0)
    slot = i % 2
    other = 1 - slot

    def fetch(idx_refs, s):
        for k in range(2):
            for r in range(TS):
                pltpu.make_async_copy(ys_ref.at[pl.ds(idx_refs[k][0, 0, r], 1), :],
                                      ybuf.at[s, k, pl.ds(r, 1), :],
                                      sem.at[s]).start(priority=r % 2)

    @pl.when(i == 0)
    def _():
        fetch((p0f_ref, p1f_ref), 0)

    for k in range(2):
        pltpu.make_async_copy(ys_ref.at[pl.ds(0, TS), :], ybuf.at[slot, k], sem.at[slot]).wait()

    @pl.when(i + 1 < n_blk)
    def _():
        fetch((p0n_ref, p1n_ref), other)

    x_new = x_ref[...] + ybuf[slot, 0] + ybuf[slot, 1]
    normed = _rms(x_new, g_ref[...], NORM_EPS)
    if emit_sum:
        out_refs[0][...] = x_new
        out_refs[1][...] = normed.astype(out_refs[1].dtype)
    else:
        out_refs[0][...] = normed.astype(out_refs[0].dtype)


def _combine_norm(x2d, ys, pos, g, *, emit_sum, norm_dtype):
    n_blk = N_TOK // TS
    tok_spec = pl.BlockSpec((TS, D_MODEL), lambda i: (i, 0))
    out_specs = [tok_spec]
    out_shape = [jax.ShapeDtypeStruct((N_TOK, D_MODEL), norm_dtype)]
    if emit_sum:
        out_specs = [tok_spec, tok_spec]
        out_shape = [jax.ShapeDtypeStruct((N_TOK, D_MODEL), F32)] + out_shape
    idx_blk = lambda f: pl.BlockSpec((1, 1, TS), f, memory_space=pltpu.SMEM)
    first = lambda i: (0, 0, 0)
    nxt = lambda i: (jnp.minimum(i + 1, n_blk - 1), 0, 0)
    pos3 = [pos[k].reshape(n_blk, 1, TS) for k in range(2)]
    return pl.pallas_call(
        functools.partial(_combine_norm_kernel, emit_sum=emit_sum),
        grid=(n_blk,),
        in_specs=[
            idx_blk(first), idx_blk(first), idx_blk(nxt), idx_blk(nxt),
            tok_spec,
            pl.BlockSpec(memory_space=pl.ANY),
            pl.BlockSpec((1, D_MODEL), lambda i: (0, 0)),
        ],
        out_specs=tuple(out_specs),
        out_shape=tuple(out_shape),
        scratch_shapes=[pltpu.VMEM((2, 2, TS, D_MODEL), F32),
                        pltpu.SemaphoreType.DMA((2,))],
        compiler_params=pltpu.CompilerParams(
            dimension_semantics=("arbitrary",), vmem_limit_bytes=VMEM_LIMIT),
        name="combine_norm",
    )(pos3[0], pos3[1], pos3[0], pos3[1], x2d, ys, g)


def _mm_kernel(a_ref, b_ref, o_ref, *, first_block_scale):
    acc = jnp.dot(a_ref[...], b_ref[...], preferred_element_type=F32)
    if first_block_scale is not None:
        acc = acc * jnp.where(pl.program_id(0) == 0, first_block_scale, 1.0)
    o_ref[...] = acc.astype(o_ref.dtype)


def _matmul(a, b, out_dtype, tn, first_block_scale=None):
    m, k = a.shape
    n = b.shape[1]
    return pl.pallas_call(
        functools.partial(_mm_kernel, first_block_scale=first_block_scale),
        grid=(n // tn, m // MM_TM),
        in_specs=[pl.BlockSpec((MM_TM, k), lambda j, i: (i, 0)),
                  pl.BlockSpec((k, tn), lambda j, i: (0, j))],
        out_specs=pl.BlockSpec((MM_TM, tn), lambda j, i: (i, j)),
        out_shape=jax.ShapeDtypeStruct((m, n), out_dtype),
        compiler_params=pltpu.CompilerParams(
            dimension_semantics=("arbitrary", "arbitrary"), vmem_limit_bytes=VMEM_LIMIT),
        name="dense_matmul",
    )(a, b)


def _attn_kernel(lam_ref, q_ref, k_ref, v_ref, subg_ref, o_ref, *scratch, lambda_init):
    m_ref, l_ref, acc_ref, s_ref, p_ref, a_ref = (scratch[2 * n:2 * n + 2] for n in range(6))
    lane_rep = lambda a, width: jnp.concatenate([a] * (width // LANES), axis=1)
    lv = lam_ref[...]
    lam = (jnp.exp(jnp.sum(lv[0:1] * lv[1:2], axis=-1, keepdims=True))
           - jnp.exp(jnp.sum(lv[2:3] * lv[3:4], axis=-1, keepdims=True)) + lambda_init)
    width = 2 * HEAD_DIM
    sub_g = subg_ref[...] * (1.0 - lambda_init)

    def query_tile(qi, carry):
        q0 = pl.multiple_of(qi * TQ, TQ)
        q = q_ref[pl.ds(q0, TQ), :]
        qs = (q[:, :HEAD_DIM], q[:, HEAD_DIM:])
        for c in range(2):
            m_ref[c][...] = jnp.full(m_ref[c].shape, NEG_INF, F32)
            l_ref[c][...] = jnp.zeros(l_ref[c].shape, F32)
            acc_ref[c][...] = jnp.zeros(acc_ref[c].shape, F32)

        def tile_update(j, diagonal):
            k0 = pl.multiple_of(j * TK, TK)
            kt = k_ref[pl.ds(k0, TK), :]
            vt = v_ref[pl.ds(k0, TK), :]
            for c in range(2):
                kc = kt[:, c * HEAD_DIM:(c + 1) * HEAD_DIM]
                s_ref[c][...] = lax.dot_general(qs[c], kc, (((1,), (1,)), ((), ())),
                                                preferred_element_type=F32)
            for c in range(2):
                for rb in range(TQ // CHUNK):
                    rows = pl.ds(rb * CHUNK, CHUNK)
                    ncol = TK
                    if diagonal:
                        seen = (rb + 1) * CHUNK
                        ncol = min(TK, -(-seen // LANES) * LANES)
                    s = s_ref[c][rows, :ncol]
                    if diagonal and seen < ncol:
                        col = lax.broadcasted_iota(I32, (CHUNK, ncol), 1)
                        s = jnp.where(col < seen, s, NEG_INF)
                    m_old = m_ref[c][rows, :]
                    m_new = jnp.maximum(m_old, jnp.max(s, axis=-1, keepdims=True))
                    alpha = jnp.exp2(m_old - m_new)
                    p = jnp.exp2(s - lane_rep(m_new, ncol))
                    l_ref[c][rows, :] = (alpha * l_ref[c][rows, :]
                                         + jnp.sum(p, axis=-1, keepdims=True))
                    m_ref[c][rows, :] = m_new
                    a_ref[c][rows, :] = alpha
                    p_ref[c][rows, :ncol] = p.astype(BF16)
                    if ncol < TK:
                        p_ref[c][rows, ncol:] = jnp.zeros((CHUNK, TK - ncol), BF16)
                acc_ref[c][...] = (lane_rep(a_ref[c][...], width) * acc_ref[c][...]
                                   + jnp.dot(p_ref[c][...], vt, preferred_element_type=F32))

        def full_body(j, c):
            tile_update(j, False)
            return c

        lax.fori_loop(0, qi, full_body, 0)
        tile_update(qi, True)

        for rb in range(TQ // CHUNK):
            rows = pl.ds(rb * CHUNK, CHUNK)
            o = (acc_ref[0][rows, :] * lane_rep(1.0 / l_ref[0][rows, :], width)
                 - acc_ref[1][rows, :] * lane_rep(lam * (1.0 / l_ref[1][rows, :]), width))
            o_ref[pl.ds(pl.multiple_of(q0 + rb * CHUNK, CHUNK), CHUNK), :] = _rms(
                o, sub_g, SUBLN_EPS).astype(o_ref.dtype)
        return carry

    lax.fori_loop(0, SEQ // TQ, query_tile, 0)


def _diff_attention(qkv, lam_rows, sub_g, lambda_init):
    assert TQ == TK
    head_w = 2 * HEAD_DIM
    return pl.pallas_call(
        functools.partial(_attn_kernel, lambda_init=lambda_init),
        grid=(BATCH, N_HEADS),
        in_specs=[
            pl.BlockSpec((SUBLANES, HEAD_DIM), lambda b, h: (0, 0)),
            pl.BlockSpec((SEQ, head_w), lambda b, h: (b, h)),
            pl.BlockSpec((SEQ, head_w), lambda b, h: (b, N_HEADS + h)),
            pl.BlockSpec((SEQ, head_w), lambda b, h: (b, 2 * N_HEADS + h)),
            pl.BlockSpec((1, head_w), lambda b, h: (0, 0)),
        ],
        out_specs=pl.BlockSpec((SEQ, head_w), lambda b, h: (b, h)),
        out_shape=jax.ShapeDtypeStruct((N_TOK, D_MODEL), BF16),
        scratch_shapes=(
            [pltpu.VMEM((TQ, LANES), F32)] * 2
            + [pltpu.VMEM((TQ, LANES), F32)] * 2
            + [pltpu.VMEM((TQ, head_w), F32)] * 2
            + [pltpu.VMEM((TQ, TK), F32)] * 2
            + [pltpu.VMEM((TQ, TK), BF16)] * 2
            + [pltpu.VMEM((TQ, LANES), F32)] * 2),
        compiler_params=pltpu.CompilerParams(
            dimension_semantics=("arbitrary", "arbitrary"),
            vmem_limit_bytes=VMEM_LIMIT),
        name="diff_attention",
    )(lam_rows, qkv, qkv, qkv, sub_g)


def _router_params(wr_g, br_g, wr_e, br_e):
    w = jnp.zeros((D_MODEL, ROUTER_COLS), F32)
    w = w.at[:, 0:N_GROUPS].set(wr_g).at[:, E_COL0:E_COL0 + N_EXPERTS].set(wr_e)
    b = jnp.full((1, ROUTER_COLS), NEG_INF, F32)
    b = b.at[0, 0:N_GROUPS].set(br_g).at[0, E_COL0:E_COL0 + N_EXPERTS].set(br_e)
    w_hi = w.astype(BF16)
    w_lo = (w - w_hi.astype(F32)).astype(BF16)
    return jnp.stack([w_hi, w_lo]), b


def kernel(x, norm_mix_g, norm_ffn_g, pool_w, pool_scale, attn_wq, attn_wk, attn_wv, attn_wo,
           lam_q1, lam_k1, lam_q2, lam_k2, attn_sub_g, router_g_w, router_g_b, router_e_w,
           router_e_b, exp_w_gate, exp_w_up, exp_w_down, final_g):
    x2d = x.reshape(N_TOK, D_MODEL)
    row = lambda v: v.reshape(1, -1)

    wr0, br0 = _router_params(router_g_w[0], router_g_b[0], router_e_w[0], router_e_b[0])
    x1, h, ids, wts = _pool_layer(x2d, row(norm_mix_g[0]), pool_w[0].astype(BF16),
                                  row(pool_scale[0]), row(norm_ffn_g[0]), wr0, br0)
    plan, pos = _route_plan(ids, wts)
    ys = _moe_grouped(0, plan, h, exp_w_gate, exp_w_up, exp_w_down)
    x2, hn = _combine_norm(x1, ys, pos, row(norm_mix_g[1]), emit_sum=True, norm_dtype=BF16)

    lambda_init = 0.8 - 0.6 * math.exp(-0.3 * 1)
    w_qkv = jnp.concatenate([attn_wq[0], attn_wk[0], attn_wv[0]], axis=1).astype(BF16)
    q_scale = HEAD_DIM ** -0.5 * math.log2(math.e)
    qkv = _matmul(hn, w_qkv, BF16, D_MODEL, first_block_scale=q_scale)
    lam_rows = jnp.zeros((SUBLANES, HEAD_DIM), F32)
    lam_rows = lam_rows.at[0].set(lam_q1[0]).at[1].set(lam_k1[0])
    lam_rows = lam_rows.at[2].set(lam_q2[0]).at[3].set(lam_k2[0])
    att = _diff_attention(qkv, lam_rows, row(attn_sub_g[0]), lambda_init)
    wr1, br1 = _router_params(router_g_w[1], router_g_b[1], router_e_w[1], router_e_b[1])
    x3, h, ids, wts = _proj_router(x2, att, attn_wo[0].astype(BF16), row(norm_ffn_g[1]),
                                   wr1, br1)
    plan, pos = _route_plan(ids, wts)
    ys = _moe_grouped(1, plan, h, exp_w_gate, exp_w_up, exp_w_down)
    (out,) = _combine_norm(x3, ys, pos, row(final_g), emit_sum=False, norm_dtype=F32)
    return out.reshape(BATCH, SEQ, D_MODEL)
```

```python
import functools
import math

import jax
import jax.numpy as jnp
from jax import lax
from jax.experimental import pallas as pl
from jax.experimental.pallas import tpu as pltpu

F32 = jnp.float32
BF16 = jnp.bfloat16
I32 = jnp.int32

D_MODEL = 2048
BATCH = 8
SEQ = 2048
N_TOK = BATCH * SEQ
CHUNK = 64
POOL_WINDOWS = (2, 4, 8, 16)
POOL_CG = D_MODEL // len(POOL_WINDOWS)
POOL_HALO = 16
N_HEADS = 8
HEAD_DIM = 128
N_GROUPS = 4
EXPERTS_PER_GROUP = 8
N_EXPERTS = 32
EXPERT_FF = 512
NORM_EPS = 1e-6
SUBLN_EPS = 1e-5
NEG_INF = -1e30

SUBLANES = 8
LANES = 128
RPT = D_MODEL // LANES
RPT_PAD = RPT + 1
MXU_K = 256
ROUTER_COLS = LANES
E_COL0 = 8

TS = 256
TM = 256
N_SORT = 2 * N_TOK + N_EXPERTS * TM
N_TILES = N_SORT // TM
TQ = 512
TK = 512
MM_TM = 512
VMEM_LIMIT = 56 * 1024 * 1024


def _rms(x, g, eps):
    ms = jnp.mean(x * x, axis=-1, keepdims=True)
    return x * lax.rsqrt(ms + eps) * g


def _to_rows(ref, val, n, pitch=RPT):
    for c in range(RPT):
        ref[pl.ds(c, n, stride=pitch), :] = val[:, c * LANES:(c + 1) * LANES]


def _from_rows(ref, n, lo=0, hi=RPT, pitch=RPT):
    return jnp.concatenate(
        [ref[pl.ds(c, n, stride=pitch), :] for c in range(lo, hi)], axis=-1)


def _router_tail(x_new, gffn_ref, wr_ref, br_ref, h_ref, ids_ref, wts_ref):
    h = _rms(x_new, gffn_ref[...], NORM_EPS)
    h_ref[...] = h
    h_hi = h.astype(BF16)
    h_lo = (h - h_hi.astype(F32)).astype(BF16)
    w_hi = wr_ref[0]
    w_lo = wr_ref[1]
    logits = (jnp.dot(h_hi, w_hi, preferred_element_type=F32)
              + jnp.dot(h_hi, w_lo, preferred_element_type=F32)
              + jnp.dot(h_lo, w_hi, preferred_element_type=F32))
    logits = logits + br_ref[...]
    lt = logits.T
    gl = lt[0:SUBLANES, :]
    io8 = lax.broadcasted_iota(I32, gl.shape, 0)
    gmax = jnp.max(gl, axis=0, keepdims=True)
    gtop = jnp.min(jnp.where(gl == gmax, io8, SUBLANES), axis=0, keepdims=True)
    p_g = 1.0 / jnp.sum(jnp.exp(gl - gmax), axis=0, keepdims=True)
    esel = lt[E_COL0:E_COL0 + EXPERTS_PER_GROUP, :]
    for gi in range(1, N_GROUPS):
        lo = E_COL0 + gi * EXPERTS_PER_GROUP
        esel = jnp.where(gtop == gi, lt[lo:lo + EXPERTS_PER_GROUP, :], esel)
    m1 = jnp.max(esel, axis=0, keepdims=True)
    i1 = jnp.min(jnp.where(esel == m1, io8, SUBLANES), axis=0, keepdims=True)
    rest = jnp.where(io8 == i1, -jnp.inf, esel)
    m2 = jnp.max(rest, axis=0, keepdims=True)
    i2 = jnp.min(jnp.where(rest == m2, io8, SUBLANES), axis=0, keepdims=True)
    r = jnp.exp(m2 - m1)
    v1 = 1.0 / (1.0 + r)
    v2 = r / (1.0 + r)
    base = gtop * EXPERTS_PER_GROUP
    ids_ref[0:1, :] = base + i1
    ids_ref[1:2, :] = base + i2
    wts_ref[0:1, :] = p_g * v1
    wts_ref[1:2, :] = p_g * v2


def _router_out_shapes():
    return (jax.ShapeDtypeStruct((N_TOK, D_MODEL), F32),
            jax.ShapeDtypeStruct((2, N_TOK), I32),
            jax.ShapeDtypeStruct((2, N_TOK), F32))


def _router_out_specs(tok_map):
    return (pl.BlockSpec((TS, D_MODEL), lambda *a: (tok_map(*a), 0)),
            pl.BlockSpec((2, TS), lambda *a: (0, tok_map(*a))),
            pl.BlockSpec((2, TS), lambda *a: (0, tok_map(*a))))


def _pool_kernel(x_ref, halo_ref, gmix_ref, wpool_ref, pscale_ref, gffn_ref, wr_ref, br_ref,
                 x1_ref, h_ref, ids_ref, wts_ref, hs_ref):
    s = pl.program_id(1)
    xt = x_ref[...]
    g = gmix_ref[...]
    h = _rms(xt, g, NORM_EPS)
    hh = _rms(halo_ref[...], g, NORM_EPS)
    hs_ref[0:POOL_HALO, :] = jnp.where(s > 0, hh, 0.0)
    hs_ref[POOL_HALO:, :] = h
    pos = s * TS + lax.broadcasted_iota(I32, (TS, 1), 0)
    outs = []
    for gi, w in enumerate(POOL_WINDOWS):
        c0 = gi * POOL_CG
        he = hs_ref[:, c0:c0 + POOL_CG]
        acc = he
        sh = 1
        while sh < w:
            acc = acc + pltpu.roll(acc, sh, axis=0)
            sh *= 2
        inv_cnt = 1.0 / jnp.minimum(pos + 1, w).astype(F32)
        mix = acc[POOL_HALO:, :] * inv_cnt - he[POOL_HALO:, :]
        outs.append(jnp.dot(mix.astype(BF16), wpool_ref[gi], preferred_element_type=F32))
    x1 = xt + jnp.concatenate(outs, axis=-1) * pscale_ref[...]
    x1_ref[...] = x1
    _router_tail(x1, gffn_ref, wr_ref, br_ref, h_ref, ids_ref, wts_ref)


def _pool_layer(x2d, gmix, wpool_bf, pscale, gffn, wr2, br):
    n_s = SEQ // TS
    tok_map = lambda b, s: b * n_s + s
    halo_blocks = TS // POOL_HALO
    const2 = lambda b, s: (0, 0)
    return pl.pallas_call(
        _pool_kernel,
        grid=(BATCH, n_s),
        in_specs=[
            pl.BlockSpec((TS, D_MODEL), lambda b, s: (tok_map(b, s), 0)),
            pl.BlockSpec((POOL_HALO, D_MODEL),
                         lambda b, s: (jnp.maximum(tok_map(b, s) * halo_blocks - 1, 0), 0)),
            pl.BlockSpec((1, D_MODEL), const2),
            pl.BlockSpec((len(POOL_WINDOWS), POOL_CG, POOL_CG), lambda b, s: (0, 0, 0)),
            pl.BlockSpec((1, D_MODEL), const2),
            pl.BlockSpec((1, D_MODEL), const2),
            pl.BlockSpec((2, D_MODEL, ROUTER_COLS), lambda b, s: (0, 0, 0)),
            pl.BlockSpec((1, ROUTER_COLS), const2),
        ],
        out_specs=(pl.BlockSpec((TS, D_MODEL), lambda b, s: (tok_map(b, s), 0)),)
        + _router_out_specs(tok_map),
        out_shape=(jax.ShapeDtypeStruct((N_TOK, D_MODEL), F32),) + _router_out_shapes(),
        scratch_shapes=[pltpu.VMEM((TS + POOL_HALO, D_MODEL), F32)],
        compiler_params=pltpu.CompilerParams(
            dimension_semantics=("arbitrary", "arbitrary"), vmem_limit_bytes=VMEM_LIMIT),
        name="pool_mixer_router",
    )(x2d, x2d, gmix, wpool_bf, pscale, gffn, wr2, br)


def _proj_router_kernel(x_ref, a_ref, wo_ref, gffn_ref, wr_ref, br_ref,
                        xo_ref, h_ref, ids_ref, wts_ref):
    x_new = x_ref[...] + jnp.dot(a_ref[...], wo_ref[...], preferred_element_type=F32)
    xo_ref[...] = x_new
    _router_tail(x_new, gffn_ref, wr_ref, br_ref, h_ref, ids_ref, wts_ref)


def _proj_router(x2d, att, wo_bf, gffn, wr2, br):
    tok_map = lambda i: i
    const2 = lambda i: (0, 0)
    return pl.pallas_call(
        _proj_router_kernel,
        grid=(N_TOK // TS,),
        in_specs=[
            pl.BlockSpec((TS, D_MODEL), lambda i: (i, 0)),
            pl.BlockSpec((TS, D_MODEL), lambda i: (i, 0)),
            pl.BlockSpec((D_MODEL, D_MODEL), const2),
            pl.BlockSpec((1, D_MODEL), const2),
            pl.BlockSpec((2, D_MODEL, ROUTER_COLS), lambda i: (0, 0, 0)),
            pl.BlockSpec((1, ROUTER_COLS), const2),
        ],
        out_specs=(pl.BlockSpec((TS, D_MODEL), lambda i: (i, 0)),) + _router_out_specs(tok_map),
        out_shape=(jax.ShapeDtypeStruct((N_TOK, D_MODEL), F32),) + _router_out_shapes(),
        compiler_params=pltpu.CompilerParams(
            dimension_semantics=("arbitrary",), vmem_limit_bytes=VMEM_LIMIT),
        name="proj_router",
    )(x2d, att, wo_bf, gffn, wr2, br)


def _dispatch_kernel(pad0_ref, padn_ref, nv_ref, p0_ref, p1_ref, h_ref, xs_ref,
                     hbuf, zrow, ztile, sem, sem_z):
    i = pl.program_id(0)
    n_blk = pl.num_programs(0)
    slot = i % 2

    def zero_row(p):
        return pltpu.make_async_copy(zrow.at[pl.ds(0, 1), :], xs_ref.at[pl.ds(p, 1), :], sem_z)

    def zero_tile(t):
        return pltpu.make_async_copy(
            ztile, xs_ref.at[pl.ds(pl.multiple_of(t * TM, TM), TM), :], sem_z)

    def for_each_unowned(row_fn, tile_fn):
        def per_expert(e, c):
            start = pad0_ref[e]

            def per_row(j, c2):
                row_fn(start + j)
                return c2
            return lax.fori_loop(0, padn_ref[e], per_row, c)
        lax.fori_loop(0, N_EXPERTS, per_expert, 0)

        def per_tile(t, c):
            tile_fn(t)
            return c
        lax.fori_loop(nv_ref[0], N_TILES, per_tile, 0)

    @pl.when(i == 0)
    def _():
        zrow[...] = jnp.zeros(zrow.shape, zrow.dtype)
        ztile[...] = jnp.zeros(ztile.shape, ztile.dtype)
        for_each_unowned(lambda p: zero_row(p).start(), lambda t: zero_tile(t).start())

    def send_wait(s):
        for _ in range(2):
            pltpu.make_async_copy(hbuf.at[s], xs_ref.at[pl.ds(0, TS), :], sem.at[s]).wait()

    @pl.when(i >= 2)
    def _():
        send_wait(slot)

    hbuf[slot] = h_ref[...]
    for k, p_ref in enumerate((p0_ref, p1_ref)):
        for r in range(TS):
            pltpu.make_async_copy(hbuf.at[slot, pl.ds(r, 1), :],
                                  xs_ref.at[pl.ds(p_ref[0, 0, r], 1), :],
                                  sem.at[slot]).start(priority=(r + k) % 2)

    @pl.when(i == n_blk - 1)
    def _():
        send_wait(1 - slot)
        send_wait(slot)
        for_each_unowned(lambda p: zero_row(p).wait(), lambda t: zero_tile(t).wait())


def _dispatch(h, pos, pad_start, pad_len, n_valid):
    n_blk = N_TOK // TS
    idx_blk = lambda: pl.BlockSpec((1, 1, TS), lambda i, *_: (i, 0, 0), memory_space=pltpu.SMEM)
    pos3 = [pos[k].reshape(n_blk, 1, TS) for k in range(2)]
    grid_spec = pltpu.PrefetchScalarGridSpec(
        num_scalar_prefetch=3,
        grid=(n_blk,),
        in_specs=[idx_blk(), idx_blk(),
                  pl.BlockSpec((TS, D_MODEL), lambda i, *_: (i, 0))],
        out_specs=pl.BlockSpec(memory_space=pl.ANY),
        scratch_shapes=[pltpu.VMEM((2, TS, D_MODEL), F32),
                        pltpu.VMEM((SUBLANES, D_MODEL), F32),
                        pltpu.VMEM((TM, D_MODEL), F32),
                        pltpu.SemaphoreType.DMA((2,)),
                        pltpu.SemaphoreType.DMA(())],
    )
    return pl.pallas_call(
        _dispatch_kernel,
        grid_spec=grid_spec,
        out_shape=jax.ShapeDtypeStruct((N_SORT, D_MODEL), F32),
        compiler_params=pltpu.CompilerParams(
            dimension_semantics=("arbitrary",), vmem_limit_bytes=VMEM_LIMIT),
        name="dispatch_rows",
    )(pad_start, pad_len, n_valid, pos3[0], pos3[1], h)


def _moe_kernel(te_ref, nv_ref, x_ref, wg_ref, wu_ref, wd_ref, o_ref, wg_bf, wu_bf, wd_bf):
    i = pl.program_id(0)
    nv = nv_ref[0]

    @pl.when(i < nv)
    def _():
        e = te_ref[i]
        e_prev = te_ref[jnp.maximum(i - 1, 0)]

        @pl.when((i == 0) | (e != e_prev))
        def _():
            wg_bf[...] = wg_ref[0, 0].astype(BF16)
            wu_bf[...] = wu_ref[0, 0].astype(BF16)
            wd_bf[...] = wd_ref[0, 0].astype(BF16)

        x = x_ref[...].astype(BF16)
        g = jnp.dot(x, wg_bf[...], preferred_element_type=F32)
        u = jnp.dot(x, wu_bf[...], preferred_element_type=F32)
        hid = (g / (1.0 + jnp.exp(-g)) * u).astype(BF16)
        o_ref[...] = jnp.dot(hid, wd_bf[...], preferred_element_type=F32)

    @pl.when(i >= nv)
    def _():
        o_ref[...] = jnp.zeros(o_ref.shape, o_ref.dtype)


def _moe_grouped(layer, tile_expert, n_valid, xs, w_gate, w_up, w_down):
    last = lambda nv: nv[0] - 1
    w_spec = lambda shape: pl.BlockSpec(
        (1, 1) + shape, lambda i, te, nv: (layer, te[jnp.minimum(i, last(nv))], 0, 0))
    grid_spec = pltpu.PrefetchScalarGridSpec(
        num_scalar_prefetch=2,
        grid=(N_TILES,),
        in_specs=[
            pl.BlockSpec((TM, D_MODEL), lambda i, te, nv: (jnp.minimum(i, last(nv)), 0)),
            w_spec((D_MODEL, EXPERT_FF)),
            w_spec((D_MODEL, EXPERT_FF)),
            w_spec((EXPERT_FF, D_MODEL)),
        ],
        out_specs=pl.BlockSpec((TM, D_MODEL), lambda i, te, nv: (i, 0)),
        scratch_shapes=[pltpu.VMEM((D_MODEL, EXPERT_FF), BF16),
                        pltpu.VMEM((D_MODEL, EXPERT_FF), BF16),
                        pltpu.VMEM((EXPERT_FF, D_MODEL), BF16)],
    )
    return pl.pallas_call(
        _moe_kernel,
        grid_spec=grid_spec,
        out_shape=jax.ShapeDtypeStruct((N_SORT, D_MODEL), F32),
        compiler_params=pltpu.CompilerParams(
            dimension_semantics=("arbitrary",), vmem_limit_bytes=VMEM_LIMIT),
        name="moe_grouped",
    )(tile_expert, n_valid, xs, w_gate, w_up, w_down)


def _route_plan(ids):
    e = ids.reshape(-1)
    onehot = (e[:, None] == jnp.arange(N_EXPERTS, dtype=I32)[None, :]).astype(I32)
    csum = jnp.cumsum(onehot, axis=0)
    rank = jnp.sum(onehot * csum, axis=1) - 1
    counts = csum[-1]
    padded = (counts + TM - 1) // TM * TM
    ends = jnp.cumsum(padded)
    offs = ends - padded
    pos = (offs[e] + rank).reshape(2, N_TOK)
    n_valid = (ends[-1] // TM).astype(I32)
    tile_start = jnp.minimum(jnp.arange(N_TILES, dtype=I32), n_valid - 1) * TM
    n_done = jnp.sum((ends[None, :] <= tile_start[:, None]).astype(I32), axis=1)
    tile_expert = jnp.minimum(n_done, N_EXPERTS - 1)
    return pos, offs + counts, padded - counts, tile_expert, n_valid.reshape(1)


def _moe_layer(layer, h, ids, w_gate, w_up, w_down):
    pos, pad_start, pad_len, tile_expert, n_valid = _route_plan(ids)
    xs = _dispatch(h, pos, pad_start, pad_len, n_valid)
    ys = _moe_grouped(layer, tile_expert, n_valid, xs, w_gate, w_up, w_down)
    return ys, pos


def _combine_norm_kernel(p0f_ref, p1f_ref, p0n_ref, p1n_ref, x_ref, w_ref, ys_ref, g_ref, *rest,
                         emit_sum):
    out_refs, (ybuf, sem) = rest[:-2], rest[-2:]
    i = pl.program_id(0)
    n_blk = pl.num_programs(0)
    slot = i % 2
    other = 1 - slot

    def fetch(idx_refs, s):
        for k in range(2):
            for r in range(TS):
                pltpu.make_async_copy(ys_ref.at[pl.ds(idx_refs[k][0, 0, r], 1), :],
                                      ybuf.at[s, k, pl.ds(r, 1), :],
                                      sem.at[s]).start(priority=r % 2)

    @pl.when(i == 0)
    def _():
        fetch((p0f_ref, p1f_ref), 0)

    for k in range(2):
        pltpu.make_async_copy(ys_ref.at[pl.ds(0, TS), :], ybuf.at[slot, k], sem.at[slot]).wait()

    @pl.when(i + 1 < n_blk)
    def _():
        fetch((p0n_ref, p1n_ref), other)

    w = w_ref[...]
    eye = (lax.broadcasted_iota(I32, (TS, TS), 0) == lax.broadcasted_iota(I32, (TS, TS), 1))
    x_new = x_ref[...]
    for k in range(2):
        w_col = jnp.sum(jnp.where(eye, w[k:k + 1, :], 0.0), axis=1, keepdims=True)
        x_new = x_new + w_col * ybuf[slot, k]
    normed = _rms(x_new, g_ref[...], NORM_EPS)
    if emit_sum:
        out_refs[0][...] = x_new
        out_refs[1][...] = normed.astype(out_refs[1].dtype)
    else:
        out_refs[0][...] = normed.astype(out_refs[0].dtype)


def _combine_norm(x2d, ys, pos, wts, g, *, emit_sum, norm_dtype):
    n_blk = N_TOK // TS
    tok_spec = pl.BlockSpec((TS, D_MODEL), lambda i: (i, 0))
    out_specs = [tok_spec]
    out_shape = [jax.ShapeDtypeStruct((N_TOK, D_MODEL), norm_dtype)]
    if emit_sum:
        out_specs = [tok_spec, tok_spec]
        out_shape = [jax.ShapeDtypeStruct((N_TOK, D_MODEL), F32)] + out_shape
    idx_blk = lambda f: pl.BlockSpec((1, 1, TS), f, memory_space=pltpu.SMEM)
    first = lambda i: (0, 0, 0)
    nxt = lambda i: (jnp.minimum(i + 1, n_blk - 1), 0, 0)
    pos3 = [pos[k].reshape(n_blk, 1, TS) for k in range(2)]
    return pl.pallas_call(
        functools.partial(_combine_norm_kernel, emit_sum=emit_sum),
        grid=(n_blk,),
        in_specs=[
            idx_blk(first), idx_blk(first), idx_blk(nxt), idx_blk(nxt),
            tok_spec,
            pl.BlockSpec((2, TS), lambda i: (0, i)),
            pl.BlockSpec(memory_space=pl.ANY),
            pl.BlockSpec((1, D_MODEL), lambda i: (0, 0)),
        ],
        out_specs=tuple(out_specs),
        out_shape=tuple(out_shape),
        scratch_shapes=[pltpu.VMEM((2, 2, TS, D_MODEL), F32),
                        pltpu.SemaphoreType.DMA((2,))],
        compiler_params=pltpu.CompilerParams(
            dimension_semantics=("arbitrary",), vmem_limit_bytes=VMEM_LIMIT),
        name="combine_norm",
    )(pos3[0], pos3[1], pos3[0], pos3[1], x2d, wts, ys, g)


def _mm_kernel(a_ref, b_ref, o_ref, *, first_block_scale):
    acc = jnp.dot(a_ref[...], b_ref[...], preferred_element_type=F32)
    if first_block_scale is not None:
        acc = acc * jnp.where(pl.program_id(0) == 0, first_block_scale, 1.0)
    o_ref[...] = acc.astype(o_ref.dtype)


def _matmul(a, b, out_dtype, tn, first_block_scale=None):
    m, k = a.shape
    n = b.shape[1]
    return pl.pallas_call(
        functools.partial(_mm_kernel, first_block_scale=first_block_scale),
        grid=(n // tn, m // MM_TM),
        in_specs=[pl.BlockSpec((MM_TM, k), lambda j, i: (i, 0)),
                  pl.BlockSpec((k, tn), lambda j, i: (0, j))],
        out_specs=pl.BlockSpec((MM_TM, tn), lambda j, i: (i, j)),
        out_shape=jax.ShapeDtypeStruct((m, n), out_dtype),
        compiler_params=pltpu.CompilerParams(
            dimension_semantics=("arbitrary", "arbitrary"), vmem_limit_bytes=VMEM_LIMIT),
        name="dense_matmul",
    )(a, b)


def _attn_kernel(lam_ref, q_ref, k_ref, v_ref, subg_ref, o_ref, *scratch, lambda_init):
    m_ref, l_ref, acc_ref, s_ref, p_ref, a_ref = (scratch[2 * n:2 * n + 2] for n in range(6))
    lane_rep = lambda a, width: jnp.concatenate([a] * (width // LANES), axis=1)
    lv = lam_ref[...]
    lam = (jnp.exp(jnp.sum(lv[0:1] * lv[1:2], axis=-1, keepdims=True))
           - jnp.exp(jnp.sum(lv[2:3] * lv[3:4], axis=-1, keepdims=True)) + lambda_init)
    width = 2 * HEAD_DIM
    sub_g = subg_ref[...] * (1.0 - lambda_init)

    def query_tile(qi, carry):
        q0 = pl.multiple_of(qi * TQ, TQ)
        q = q_ref[pl.ds(q0, TQ), :]
        qs = (q[:, :HEAD_DIM], q[:, HEAD_DIM:])
        for c in range(2):
            m_ref[c][...] = jnp.full(m_ref[c].shape, NEG_INF, F32)
            l_ref[c][...] = jnp.zeros(l_ref[c].shape, F32)
            acc_ref[c][...] = jnp.zeros(acc_ref[c].shape, F32)

        def tile_update(j, diagonal):
            k0 = pl.multiple_of(j * TK, TK)
            kt = k_ref[pl.ds(k0, TK), :]
            vt = v_ref[pl.ds(k0, TK), :]
            for c in range(2):
                kc = kt[:, c * HEAD_DIM:(c + 1) * HEAD_DIM]
                s_ref[c][...] = lax.dot_general(qs[c], kc, (((1,), (1,)), ((), ())),
                                                preferred_element_type=F32)
            for c in range(2):
                for rb in range(TQ // CHUNK):
                    rows = pl.ds(rb * CHUNK, CHUNK)
                    ncol = TK
                    if diagonal:
                        seen = (rb + 1) * CHUNK
                        ncol = min(TK, -(-seen // LANES) * LANES)
                    s = s_ref[c][rows, :ncol]
                    if diagonal and seen < ncol:
                        col = lax.broadcasted_iota(I32, (CHUNK, ncol), 1)
                        s = jnp.where(col < seen, s, NEG_INF)
                    m_old = m_ref[c][rows, :]
                    m_new = jnp.maximum(m_old, jnp.max(s, axis=-1, keepdims=True))
                    alpha = jnp.exp2(m_old - m_new)
                    p = jnp.exp2(s - lane_rep(m_new, ncol))
                    l_ref[c][rows, :] = (alpha * l_ref[c][rows, :]
                                         + jnp.sum(p, axis=-1, keepdims=True))
                    m_ref[c][rows, :] = m_new
                    a_ref[c][rows, :] = alpha
                    p_ref[c][rows, :ncol] = p.astype(BF16)
                    if ncol < TK:
                        p_ref[c][rows, ncol:] = jnp.zeros((CHUNK, TK - ncol), BF16)
                acc_ref[c][...] = (lane_rep(a_ref[c][...], width) * acc_ref[c][...]
                                   + jnp.dot(p_ref[c][...], vt, preferred_element_type=F32))

        def full_body(j, c):
            tile_update(j, False)
            return c

        lax.fori_loop(0, qi, full_body, 0)
        tile_update(qi, True)

        for rb in range(TQ // CHUNK):
            rows = pl.ds(rb * CHUNK, CHUNK)
            o = (acc_ref[0][rows, :] * lane_rep(1.0 / l_ref[0][rows, :], width)
                 - acc_ref[1][rows, :] * lane_rep(lam * (1.0 / l_ref[1][rows, :]), width))
            o_ref[pl.ds(pl.multiple_of(q0 + rb * CHUNK, CHUNK), CHUNK), :] = _rms(
                o, sub_g, SUBLN_EPS).astype(o_ref.dtype)
        return carry

    lax.fori_loop(0, SEQ // TQ, query_tile, 0)


def _diff_attention(qkv, lam_rows, sub_g, lambda_init):
    assert TQ == TK
    head_w = 2 * HEAD_DIM
    return pl.pallas_call(
        functools.partial(_attn_kernel, lambda_init=lambda_init),
        grid=(BATCH, N_HEADS),
        in_specs=[
            pl.BlockSpec((SUBLANES, HEAD_DIM), lambda b, h: (0, 0)),
            pl.BlockSpec((SEQ, head_w), lambda b, h: (b, h)),
            pl.BlockSpec((SEQ, head_w), lambda b, h: (b, N_HEADS + h)),
            pl.BlockSpec((SEQ, head_w), lambda b, h: (b, 2 * N_HEADS + h)),
            pl.BlockSpec((1, head_w), lambda b, h: (0, 0)),
        ],
        out_specs=pl.BlockSpec((SEQ, head_w), lambda b, h: (b, h)),
        out_shape=jax.ShapeDtypeStruct((N_TOK, D_MODEL), BF16),
        scratch_shapes=(
            [pltpu.VMEM((TQ, LANES), F32)] * 2
            + [pltpu.VMEM((TQ, LANES), F32)] * 2
            + [pltpu.VMEM((TQ, head_w), F32)] * 2
            + [pltpu.VMEM((TQ, TK), F32)] * 2
            + [pltpu.VMEM((TQ, TK), BF16)] * 2
            + [pltpu.VMEM((TQ, LANES), F32)] * 2),
        compiler_params=pltpu.CompilerParams(
            dimension_semantics=("arbitrary", "arbitrary"),
            vmem_limit_bytes=VMEM_LIMIT),
        name="diff_attention",
    )(lam_rows, qkv, qkv, qkv, sub_g)


def _router_params(wr_g, br_g, wr_e, br_e):
    w = jnp.zeros((D_MODEL, ROUTER_COLS), F32)
    w = w.at[:, 0:N_GROUPS].set(wr_g).at[:, E_COL0:E_COL0 + N_EXPERTS].set(wr_e)
    b = jnp.full((1, ROUTER_COLS), NEG_INF, F32)
    b = b.at[0, 0:N_GROUPS].set(br_g).at[0, E_COL0:E_COL0 + N_EXPERTS].set(br_e)
    w_hi = w.astype(BF16)
    w_lo = (w - w_hi.astype(F32)).astype(BF16)
    return jnp.stack([w_hi, w_lo]), b


def kernel(x, norm_mix_g, norm_ffn_g, pool_w, pool_scale, attn_wq, attn_wk, attn_wv, attn_wo,
           lam_q1, lam_k1, lam_q2, lam_k2, attn_sub_g, router_g_w, router_g_b, router_e_w,
           router_e_b, exp_w_gate, exp_w_up, exp_w_down, final_g):
    x2d = x.reshape(N_TOK, D_MODEL)
    row = lambda v: v.reshape(1, -1)

    wr0, br0 = _router_params(router_g_w[0], router_g_b[0], router_e_w[0], router_e_b[0])
    x1, h, ids, wts = _pool_layer(x2d, row(norm_mix_g[0]), pool_w[0].astype(BF16),
                                  row(pool_scale[0]), row(norm_ffn_g[0]), wr0, br0)
    ys, pos = _moe_layer(0, h, ids, exp_w_gate, exp_w_up, exp_w_down)
    x2, hn = _combine_norm(x1, ys, pos, wts, row(norm_mix_g[1]), emit_sum=True, norm_dtype=BF16)

    lambda_init = 0.8 - 0.6 * math.exp(-0.3 * 1)
    w_qkv = jnp.concatenate([attn_wq[0], attn_wk[0], attn_wv[0]], axis=1).astype(BF16)
    q_scale = HEAD_DIM ** -0.5 * math.log2(math.e)
    qkv = _matmul(hn, w_qkv, BF16, D_MODEL, first_block_scale=q_scale)
    lam_rows = jnp.zeros((SUBLANES, HEAD_DIM), F32)
    lam_rows = lam_rows.at[0].set(lam_q1[0]).at[1].set(lam_k1[0])
    lam_rows = lam_rows.at[2].set(lam_q2[0]).at[3].set(lam_k2[0])
    att = _diff_attention(qkv, lam_rows, row(attn_sub_g[0]), lambda_init)
    wr1, br1 = _router_params(router_g_w[1], router_g_b[1], router_e_w[1], router_e_b[1])
    x3, h, ids, wts = _proj_router(x2, att, attn_wo[0].astype(BF16), row(norm_ffn_g[1]),
                                   wr1, br1)
    ys, pos = _moe_layer(1, h, ids, exp_w_gate, exp_w_up, exp_w_down)
    (out,) = _combine_norm(x3, ys, pos, wts, row(final_g), emit_sum=False, norm_dtype=F32)
    return out.reshape(BATCH, SEQ, D_MODEL)
```

```python
import functools
import math

import jax
import jax.numpy as jnp
from jax import lax
from jax.experimental import pallas as pl
from jax.experimental.pallas import tpu as pltpu

F32 = jnp.float32
BF16 = jnp.bfloat16
I32 = jnp.int32

D_MODEL = 2048
BATCH = 8
SEQ = 2048
N_TOK = BATCH * SEQ
CHUNK = 64
POOL_WINDOWS = (2, 4, 8, 16)
POOL_CG = D_MODEL // len(POOL_WINDOWS)
POOL_HALO = 16
N_HEADS = 8
HEAD_DIM = 128
N_GROUPS = 4
EXPERTS_PER_GROUP = 8
N_EXPERTS = 32
EXPERT_FF = 512
NORM_EPS = 1e-6
SUBLN_EPS = 1e-5
NEG_INF = -1e30

SUBLANES = 8
LANES = 128
RPT = D_MODEL // LANES
RPT_PAD = RPT + 1
MXU_K = 256
ROUTER_COLS = LANES
E_COL0 = 8

TS = 256
TM = 256
N_SORT = 2 * N_TOK + N_EXPERTS * TM
N_TILES = N_SORT // TM
TQ = 512
TK = 512
MM_TM = 512
VMEM_LIMIT = 56 * 1024 * 1024


def _rms(x, g, eps):
    ms = jnp.mean(x * x, axis=-1, keepdims=True)
    return x * lax.rsqrt(ms + eps) * g


def _to_rows(ref, val, n, pitch=RPT):
    for c in range(RPT):
        ref[pl.ds(c, n, stride=pitch), :] = val[:, c * LANES:(c + 1) * LANES]


def _from_rows(ref, n, lo=0, hi=RPT, pitch=RPT):
    return jnp.concatenate(
        [ref[pl.ds(c, n, stride=pitch), :] for c in range(lo, hi)], axis=-1)


def _router_tail(x_new, gffn_ref, wr_ref, br_ref, h_ref, ids_ref, wts_ref):
    h = _rms(x_new, gffn_ref[...], NORM_EPS)
    h_ref[...] = h
    h_hi = h.astype(BF16)
    h_lo = (h - h_hi.astype(F32)).astype(BF16)
    w_hi = wr_ref[0]
    w_lo = wr_ref[1]
    logits = (jnp.dot(h_hi, w_hi, preferred_element_type=F32)
              + jnp.dot(h_hi, w_lo, preferred_element_type=F32)
              + jnp.dot(h_lo, w_hi, preferred_element_type=F32))
    logits = logits + br_ref[...]
    lt = logits.T
    gl = lt[0:SUBLANES, :]
    io8 = lax.broadcasted_iota(I32, gl.shape, 0)
    gmax = jnp.max(gl, axis=0, keepdims=True)
    gtop = jnp.min(jnp.where(gl == gmax, io8, SUBLANES), axis=0, keepdims=True)
    p_g = 1.0 / jnp.sum(jnp.exp(gl - gmax), axis=0, keepdims=True)
    esel = lt[E_COL0:E_COL0 + EXPERTS_PER_GROUP, :]
    for gi in range(1, N_GROUPS):
        lo = E_COL0 + gi * EXPERTS_PER_GROUP
        esel = jnp.where(gtop == gi, lt[lo:lo + EXPERTS_PER_GROUP, :], esel)
    m1 = jnp.max(esel, axis=0, keepdims=True)
    i1 = jnp.min(jnp.where(esel == m1, io8, SUBLANES), axis=0, keepdims=True)
    rest = jnp.where(io8 == i1, -jnp.inf, esel)
    m2 = jnp.max(rest, axis=0, keepdims=True)
    i2 = jnp.min(jnp.where(rest == m2, io8, SUBLANES), axis=0, keepdims=True)
    r = jnp.exp(m2 - m1)
    v1 = 1.0 / (1.0 + r)
    v2 = r / (1.0 + r)
    base = gtop * EXPERTS_PER_GROUP
    ids_ref[0:1, :] = base + i1
    ids_ref[1:2, :] = base + i2
    wts_ref[0:1, :] = p_g * v1
    wts_ref[1:2, :] = p_g * v2


def _router_out_shapes():
    return (jax.ShapeDtypeStruct((N_TOK, D_MODEL), F32),
            jax.ShapeDtypeStruct((2, N_TOK), I32),
            jax.ShapeDtypeStruct((2, N_TOK), F32))


def _router_out_specs(tok_map):
    return (pl.BlockSpec((TS, D_MODEL), lambda *a: (tok_map(*a), 0)),
            pl.BlockSpec((2, TS), lambda *a: (0, tok_map(*a))),
            pl.BlockSpec((2, TS), lambda *a: (0, tok_map(*a))))


def _pool_kernel(x_ref, halo_ref, gmix_ref, wpool_ref, pscale_ref, gffn_ref, wr_ref, br_ref,
                 x1_ref, h_ref, ids_ref, wts_ref, hs_ref):
    s = pl.program_id(1)
    xt = x_ref[...]
    g = gmix_ref[...]
    h = _rms(xt, g, NORM_EPS)
    hh = _rms(halo_ref[...], g, NORM_EPS)
    hs_ref[0:POOL_HALO, :] = jnp.where(s > 0, hh, 0.0)
    hs_ref[POOL_HALO:, :] = h
    pos = s * TS + lax.broadcasted_iota(I32, (TS, 1), 0)
    outs = []
    for gi, w in enumerate(POOL_WINDOWS):
        c0 = gi * POOL_CG
        he = hs_ref[:, c0:c0 + POOL_CG]
        acc = he
        sh = 1
        while sh < w:
            acc = acc + pltpu.roll(acc, sh, axis=0)
            sh *= 2
        inv_cnt = 1.0 / jnp.minimum(pos + 1, w).astype(F32)
        mix = acc[POOL_HALO:, :] * inv_cnt - he[POOL_HALO:, :]
        outs.append(jnp.dot(mix.astype(BF16), wpool_ref[gi], preferred_element_type=F32))
    x1 = xt + jnp.concatenate(outs, axis=-1) * pscale_ref[...]
    x1_ref[...] = x1
    _router_tail(x1, gffn_ref, wr_ref, br_ref, h_ref, ids_ref, wts_ref)


def _pool_layer(x2d, gmix, wpool_bf, pscale, gffn, wr2, br):
    n_s = SEQ // TS
    tok_map = lambda b, s: b * n_s + s
    halo_blocks = TS // POOL_HALO
    const2 = lambda b, s: (0, 0)
    return pl.pallas_call(
        _pool_kernel,
        grid=(BATCH, n_s),
        in_specs=[
            pl.BlockSpec((TS, D_MODEL), lambda b, s: (tok_map(b, s), 0)),
            pl.BlockSpec((POOL_HALO, D_MODEL),
                         lambda b, s: (jnp.maximum(tok_map(b, s) * halo_blocks - 1, 0), 0)),
            pl.BlockSpec((1, D_MODEL), const2),
            pl.BlockSpec((len(POOL_WINDOWS), POOL_CG, POOL_CG), lambda b, s: (0, 0, 0)),
            pl.BlockSpec((1, D_MODEL), const2),
            pl.BlockSpec((1, D_MODEL), const2),
            pl.BlockSpec((2, D_MODEL, ROUTER_COLS), lambda b, s: (0, 0, 0)),
            pl.BlockSpec((1, ROUTER_COLS), const2),
        ],
        out_specs=(pl.BlockSpec((TS, D_MODEL), lambda b, s: (tok_map(b, s), 0)),)
        + _router_out_specs(tok_map),
        out_shape=(jax.ShapeDtypeStruct((N_TOK, D_MODEL), F32),) + _router_out_shapes(),
        scratch_shapes=[pltpu.VMEM((TS + POOL_HALO, D_MODEL), F32)],
        compiler_params=pltpu.CompilerParams(
            dimension_semantics=("arbitrary", "arbitrary"), vmem_limit_bytes=VMEM_LIMIT),
        name="pool_mixer_router",
    )(x2d, x2d, gmix, wpool_bf, pscale, gffn, wr2, br)


def _proj_router_kernel(x_ref, a_ref, wo_ref, gffn_ref, wr_ref, br_ref,
                        xo_ref, h_ref, ids_ref, wts_ref):
    x_new = x_ref[...] + jnp.dot(a_ref[...], wo_ref[...], preferred_element_type=F32)
    xo_ref[...] = x_new
    _router_tail(x_new, gffn_ref, wr_ref, br_ref, h_ref, ids_ref, wts_ref)


def _proj_router(x2d, att, wo_bf, gffn, wr2, br):
    tok_map = lambda i: i
    const2 = lambda i: (0, 0)
    return pl.pallas_call(
        _proj_router_kernel,
        grid=(N_TOK // TS,),
        in_specs=[
            pl.BlockSpec((TS, D_MODEL), lambda i: (i, 0)),
            pl.BlockSpec((TS, D_MODEL), lambda i: (i, 0)),
            pl.BlockSpec((D_MODEL, D_MODEL), const2),
            pl.BlockSpec((1, D_MODEL), const2),
            pl.BlockSpec((2, D_MODEL, ROUTER_COLS), lambda i: (0, 0, 0)),
            pl.BlockSpec((1, ROUTER_COLS), const2),
        ],
        out_specs=(pl.BlockSpec((TS, D_MODEL), lambda i: (i, 0)),) + _router_out_specs(tok_map),
        out_shape=(jax.ShapeDtypeStruct((N_TOK, D_MODEL), F32),) + _router_out_shapes(),
        compiler_params=pltpu.CompilerParams(
            dimension_semantics=("arbitrary",), vmem_limit_bytes=VMEM_LIMIT),
        name="proj_router",
    )(x2d, att, wo_bf, gffn, wr2, br)


def _dispatch_kernel(pad0_ref, padn_ref, nv_ref, p0_ref, p1_ref, h_ref, xs_ref,
                     hbuf, zrow, ztile, sem, sem_z):
    i = pl.program_id(0)
    n_blk = pl.num_programs(0)
    slot = i % 2

    def zero_row(p):
        return pltpu.make_async_copy(zrow.at[pl.ds(0, 1), :], xs_ref.at[pl.ds(p, 1), :], sem_z)

    def zero_tile(t):
        return pltpu.make_async_copy(
            ztile, xs_ref.at[pl.ds(pl.multiple_of(t * TM, TM), TM), :], sem_z)

    def for_each_unowned(row_fn, tile_fn):
        def per_expert(e, c):
            start = pad0_ref[e]

            def per_row(j, c2):
                row_fn(start + j)
                return c2
            return lax.fori_loop(0, padn_ref[e], per_row, c)
        lax.fori_loop(0, N_EXPERTS, per_expert, 0)

        def per_tile(t, c):
            tile_fn(t)
            return c
        lax.fori_loop(nv_ref[0], N_TILES, per_tile, 0)

    @pl.when(i == 0)
    def _():
        zrow[...] = jnp.zeros(zrow.shape, zrow.dtype)
        ztile[...] = jnp.zeros(ztile.shape, ztile.dtype)
        for_each_unowned(lambda p: zero_row(p).start(), lambda t: zero_tile(t).start())

    def send_wait(s):
        for _ in range(2):
            pltpu.make_async_copy(hbuf.at[s], xs_ref.at[pl.ds(0, TS), :], sem.at[s]).wait()

    @pl.when(i >= 2)
    def _():
        send_wait(slot)

    hbuf[slot] = h_ref[...]
    for k, p_ref in enumerate((p0_ref, p1_ref)):
        for r in range(TS):
            pltpu.make_async_copy(hbuf.at[slot, pl.ds(r, 1), :],
                                  xs_ref.at[pl.ds(p_ref[0, 0, r], 1), :],
                                  sem.at[slot]).start(priority=(r + k) % 2)

    @pl.when(i == n_blk - 1)
    def _():
        send_wait(1 - slot)
        send_wait(slot)
        for_each_unowned(lambda p: zero_row(p).wait(), lambda t: zero_tile(t).wait())


def _dispatch(h, pos, pad_start, pad_len, n_valid):
    n_blk = N_TOK // TS
    idx_blk = lambda: pl.BlockSpec((1, 1, TS), lambda i, *_: (i, 0, 0), memory_space=pltpu.SMEM)
    pos3 = [pos[k].reshape(n_blk, 1, TS) for k in range(2)]
    grid_spec = pltpu.PrefetchScalarGridSpec(
        num_scalar_prefetch=3,
        grid=(n_blk,),
        in_specs=[idx_blk(), idx_blk(),
                  pl.BlockSpec((TS, D_MODEL), lambda i, *_: (i, 0))],
        out_specs=pl.BlockSpec(memory_space=pl.ANY),
        scratch_shapes=[pltpu.VMEM((2, TS, D_MODEL), F32),
                        pltpu.VMEM((SUBLANES, D_MODEL), F32),
                        pltpu.VMEM((TM, D_MODEL), F32),
                        pltpu.SemaphoreType.DMA((2,)),
                        pltpu.SemaphoreType.DMA(())],
    )
    return pl.pallas_call(
        _dispatch_kernel,
        grid_spec=grid_spec,
        out_shape=jax.ShapeDtypeStruct((N_SORT, D_MODEL), F32),
        compiler_params=pltpu.CompilerParams(
            dimension_semantics=("arbitrary",), vmem_limit_bytes=VMEM_LIMIT),
        name="dispatch_rows",
    )(pad_start, pad_len, n_valid, pos3[0], pos3[1], h)


def _moe_kernel(te_ref, nv_ref, first_ref, ord_ref, nxt_ref, x_ref, wg_hbm, wu_hbm, wd_hbm, o_ref,
                wg_f32, wu_f32, wd_f32, wg_bf, wu_bf, wd_bf, sem_w, *, layer):
    i = pl.program_id(0)
    nv = nv_ref[0]

    def weight_copies(e, s):
        return [pltpu.make_async_copy(hbm.at[layer, e], buf.at[s], sem_w.at[s])
                for hbm, buf in ((wg_hbm, wg_f32), (wu_hbm, wu_f32), (wd_hbm, wd_f32))]

    @pl.when(i == 0)
    def _():
        for cp in weight_copies(te_ref[0], 0):
            cp.start()

    @pl.when((i < nv) & (first_ref[i] > 0))
    def _():
        e = te_ref[i]
        s = ord_ref[i] % 2
        for cp in weight_copies(e, s):
            cp.wait()
        wg_bf[...] = wg_f32[s].astype(BF16)
        wu_bf[...] = wu_f32[s].astype(BF16)
        wd_bf[...] = wd_f32[s].astype(BF16)
        e_next = nxt_ref[i]

        @pl.when(e_next != e)
        def _():
            for cp in weight_copies(e_next, 1 - s):
                cp.start()

    @pl.when(i < nv)
    def _():
        x = x_ref[...].astype(BF16)
        g = jnp.dot(x, wg_bf[...], preferred_element_type=F32)
        u = jnp.dot(x, wu_bf[...], preferred_element_type=F32)
        hid = (g / (1.0 + jnp.exp(-g)) * u).astype(BF16)
        o_ref[...] = jnp.dot(hid, wd_bf[...], preferred_element_type=F32)

    @pl.when(i >= nv)
    def _():
        o_ref[...] = jnp.zeros(o_ref.shape, o_ref.dtype)


def _moe_grouped(layer, tiles, xs, w_gate, w_up, w_down):
    tile_expert, n_valid, first, ordinal, next_expert = tiles
    grid_spec = pltpu.PrefetchScalarGridSpec(
        num_scalar_prefetch=5,
        grid=(N_TILES,),
        in_specs=[
            pl.BlockSpec((TM, D_MODEL), lambda i, te, nv, *_: (jnp.minimum(i, nv[0] - 1), 0)),
            pl.BlockSpec(memory_space=pl.ANY),
            pl.BlockSpec(memory_space=pl.ANY),
            pl.BlockSpec(memory_space=pl.ANY),
        ],
        out_specs=pl.BlockSpec((TM, D_MODEL), lambda i, *_: (i, 0)),
        scratch_shapes=[pltpu.VMEM((2, D_MODEL, EXPERT_FF), F32),
                        pltpu.VMEM((2, D_MODEL, EXPERT_FF), F32),
                        pltpu.VMEM((2, EXPERT_FF, D_MODEL), F32),
                        pltpu.VMEM((D_MODEL, EXPERT_FF), BF16),
                        pltpu.VMEM((D_MODEL, EXPERT_FF), BF16),
                        pltpu.VMEM((EXPERT_FF, D_MODEL), BF16),
                        pltpu.SemaphoreType.DMA((2,))],
    )
    return pl.pallas_call(
        functools.partial(_moe_kernel, layer=layer),
        grid_spec=grid_spec,
        out_shape=jax.ShapeDtypeStruct((N_SORT, D_MODEL), F32),
        compiler_params=pltpu.CompilerParams(
            dimension_semantics=("arbitrary",), vmem_limit_bytes=VMEM_LIMIT),
        name="moe_grouped",
    )(tile_expert, n_valid, first, ordinal, next_expert, xs, w_gate, w_up, w_down)


def _route_plan(ids):
    e = ids.reshape(-1)
    onehot = (e[:, None] == jnp.arange(N_EXPERTS, dtype=I32)[None, :]).astype(I32)
    csum = jnp.cumsum(onehot, axis=0)
    rank = jnp.sum(onehot * csum, axis=1) - 1
    counts = csum[-1]
    padded = (counts + TM - 1) // TM * TM
    ends = jnp.cumsum(padded)
    offs = ends - padded
    pos = (offs[e] + rank).reshape(2, N_TOK)
    n_valid = (ends[-1] // TM).astype(I32)
    tile = jnp.arange(N_TILES, dtype=I32)
    tile_start = jnp.minimum(tile, n_valid - 1) * TM
    n_done = jnp.sum((ends[None, :] <= tile_start[:, None]).astype(I32), axis=1)
    tile_expert = jnp.minimum(n_done, N_EXPERTS - 1)
    prev_expert = jnp.concatenate([jnp.full((1,), -1, I32), tile_expert[:-1]])
    first = ((tile_expert != prev_expert) & (tile < n_valid)).astype(I32)
    ordinal = jnp.cumsum(first) - 1
    expert = jnp.arange(N_EXPERTS, dtype=I32)
    used_id = jnp.where(counts > 0, expert, N_EXPERTS)
    later = expert[None, :] > expert[:, None]
    next_used = jnp.min(jnp.where(later, used_id[None, :], N_EXPERTS), axis=1)
    next_used = jnp.where(next_used < N_EXPERTS, next_used, expert)
    next_expert = jnp.sum(jnp.where(tile_expert[:, None] == expert[None, :],
                                    next_used[None, :], 0), axis=1)
    tiles = (tile_expert, n_valid.reshape(1), first, ordinal, next_expert)
    return pos, offs + counts, padded - counts, tiles


def _moe_layer(layer, h, ids, w_gate, w_up, w_down):
    pos, pad_start, pad_len, tiles = _route_plan(ids)
    xs = _dispatch(h, pos, pad_start, pad_len, tiles[1])
    ys = _moe_grouped(layer, tiles, xs, w_gate, w_up, w_down)
    return ys, pos


def _combine_norm_kernel(p0f_ref, p1f_ref, p0n_ref, p1n_ref, x_ref, w_ref, ys_ref, g_ref, *rest,
                         emit_sum):
    out_refs, (ybuf, sem) = rest[:-2], rest[-2:]
    i = pl.program_id(0)
    n_blk = pl.num_programs(0)
    slot = i % 2
    other = 1 - slot

    def fetch(idx_refs, s):
        for k in range(2):
            for r in range(TS):
                pltpu.make_async_copy(ys_ref.at[pl.ds(idx_refs[k][0, 0, r], 1), :],
                                      ybuf.at[s, k, pl.ds(r, 1), :],
                                      sem.at[s]).start(priority=r % 2)

    @pl.when(i == 0)
    def _():
        fetch((p0f_ref, p1f_ref), 0)

    for k in range(2):
        pltpu.make_async_copy(ys_ref.at[pl.ds(0, TS), :], ybuf.at[slot, k], sem.at[slot]).wait()

    @pl.when(i + 1 < n_blk)
    def _():
        fetch((p0n_ref, p1n_ref), other)

    w = w_ref[...]
    eye = (lax.broadcasted_iota(I32, (TS, TS), 0) == lax.broadcasted_iota(I32, (TS, TS), 1))
    x_new = x_ref[...]
    for k in range(2):
        w_col = jnp.sum(jnp.where(eye, w[k:k + 1, :], 0.0), axis=1, keepdims=True)
        x_new = x_new + w_col * ybuf[slot, k]
    normed = _rms(x_new, g_ref[...], NORM_EPS)
    if emit_sum:
        out_refs[0][...] = x_new
        out_refs[1][...] = normed.astype(out_refs[1].dtype)
    else:
        out_refs[0][...] = normed.astype(out_refs[0].dtype)


def _combine_norm(x2d, ys, pos, wts, g, *, emit_sum, norm_dtype):
    n_blk = N_TOK // TS
    tok_spec = pl.BlockSpec((TS, D_MODEL), lambda i: (i, 0))
    out_specs = [tok_spec]
    out_shape = [jax.ShapeDtypeStruct((N_TOK, D_MODEL), norm_dtype)]
    if emit_sum:
        out_specs = [tok_spec, tok_spec]
        out_shape = [jax.ShapeDtypeStruct((N_TOK, D_MODEL), F32)] + out_shape
    idx_blk = lambda f: pl.BlockSpec((1, 1, TS), f, memory_space=pltpu.SMEM)
    first = lambda i: (0, 0, 0)
    nxt = lambda i: (jnp.minimum(i + 1, n_blk - 1), 0, 0)
    pos3 = [pos[k].reshape(n_blk, 1, TS) for k in range(2)]
    return pl.pallas_call(
        functools.partial(_combine_norm_kernel, emit_sum=emit_sum),
        grid=(n_blk,),
        in_specs=[
            idx_blk(first), idx_blk(first), idx_blk(nxt), idx_blk(nxt),
            tok_spec,
            pl.BlockSpec((2, TS), lambda i: (0, i)),
            pl.BlockSpec(memory_space=pl.ANY),
            pl.BlockSpec((1, D_MODEL), lambda i: (0, 0)),
        ],
        out_specs=tuple(out_specs),
        out_shape=tuple(out_shape),
        scratch_shapes=[pltpu.VMEM((2, 2, TS, D_MODEL), F32),
                        pltpu.SemaphoreType.DMA((2,))],
        compiler_params=pltpu.CompilerParams(
            dimension_semantics=("arbitrary",), vmem_limit_bytes=VMEM_LIMIT),
        name="combine_norm",
    )(pos3[0], pos3[1], pos3[0], pos3[1], x2d, wts, ys, g)


def _mm_kernel(a_ref, b_ref, o_ref, *, first_block_scale):
    acc = jnp.dot(a_ref[...], b_ref[...], preferred_element_type=F32)
    if first_block_scale is not None:
        acc = acc * jnp.where(pl.program_id(0) == 0, first_block_scale, 1.0)
    o_ref[...] = acc.astype(o_ref.dtype)


def _matmul(a, b, out_dtype, tn, first_block_scale=None):
    m, k = a.shape
    n = b.shape[1]
    return pl.pallas_call(
        functools.partial(_mm_kernel, first_block_scale=first_block_scale),
        grid=(n // tn, m // MM_TM),
        in_specs=[pl.BlockSpec((MM_TM, k), lambda j, i: (i, 0)),
                  pl.BlockSpec((k, tn), lambda j, i: (0, j))],
        out_specs=pl.BlockSpec((MM_TM, tn), lambda j, i: (i, j)),
        out_shape=jax.ShapeDtypeStruct((m, n), out_dtype),
        compiler_params=pltpu.CompilerParams(
            dimension_semantics=("arbitrary", "arbitrary"), vmem_limit_bytes=VMEM_LIMIT),
        name="dense_matmul",
    )(a, b)


def _attn_kernel(lam_ref, q_ref, k_ref, v_ref, subg_ref, o_ref, *scratch, lambda_init):
    m_ref, l_ref, acc_ref, s_ref, p_ref, a_ref = (scratch[2 * n:2 * n + 2] for n in range(6))
    lane_rep = lambda a, width: jnp.concatenate([a] * (width // LANES), axis=1)
    lv = lam_ref[...]
    lam = (jnp.exp(jnp.sum(lv[0:1] * lv[1:2], axis=-1, keepdims=True))
           - jnp.exp(jnp.sum(lv[2:3] * lv[3:4], axis=-1, keepdims=True)) + lambda_init)
    width = 2 * HEAD_DIM
    sub_g = subg_ref[...] * (1.0 - lambda_init)

    def query_tile(qi, carry):
        q0 = pl.multiple_of(qi * TQ, TQ)
        q = q_ref[pl.ds(q0, TQ), :]
        qs = (q[:, :HEAD_DIM], q[:, HEAD_DIM:])
        for c in range(2):
            m_ref[c][...] = jnp.full(m_ref[c].shape, NEG_INF, F32)
            l_ref[c][...] = jnp.zeros(l_ref[c].shape, F32)
            acc_ref[c][...] = jnp.zeros(acc_ref[c].shape, F32)

        def tile_update(j, diagonal):
            k0 = pl.multiple_of(j * TK, TK)
            kt = k_ref[pl.ds(k0, TK), :]
            vt = v_ref[pl.ds(k0, TK), :]
            for c in range(2):
                kc = kt[:, c * HEAD_DIM:(c + 1) * HEAD_DIM]
                s_ref[c][...] = lax.dot_general(qs[c], kc, (((1,), (1,)), ((), ())),
                                                preferred_element_type=F32)
            for c in range(2):
                for rb in range(TQ // CHUNK):
                    rows = pl.ds(rb * CHUNK, CHUNK)
                    ncol = TK
                    if diagonal:
                        seen = (rb + 1) * CHUNK
                        ncol = min(TK, -(-seen // LANES) * LANES)
                    s = s_ref[c][rows, :ncol]
                    if diagonal and seen < ncol:
                        col = lax.broadcasted_iota(I32, (CHUNK, ncol), 1)
                        s = jnp.where(col < seen, s, NEG_INF)
                    m_old = m_ref[c][rows, :]
                    m_new = jnp.maximum(m_old, jnp.max(s, axis=-1, keepdims=True))
                    alpha = jnp.exp2(m_old - m_new)
                    p = jnp.exp2(s - lane_rep(m_new, ncol))
                    l_ref[c][rows, :] = (alpha * l_ref[c][rows, :]
                                         + jnp.sum(p, axis=-1, keepdims=True))
                    m_ref[c][rows, :] = m_new
                    a_ref[c][rows, :] = alpha
                    p_ref[c][rows, :ncol] = p.astype(BF16)
                    if ncol < TK:
                        p_ref[c][rows, ncol:] = jnp.zeros((CHUNK, TK - ncol), BF16)
                acc_ref[c][...] = (lane_rep(a_ref[c][...], width) * acc_ref[c][...]
                                   + jnp.dot(p_ref[c][...], vt, preferred_element_type=F32))

        def full_body(j, c):
            tile_update(j, False)
            return c

        lax.fori_loop(0, qi, full_body, 0)
        tile_update(qi, True)

        for rb in range(TQ // CHUNK):
            rows = pl.ds(rb * CHUNK, CHUNK)
            o = (acc_ref[0][rows, :] * lane_rep(1.0 / l_ref[0][rows, :], width)
                 - acc_ref[1][rows, :] * lane_rep(lam * (1.0 / l_ref[1][rows, :]), width))
            o_ref[pl.ds(pl.multiple_of(q0 + rb * CHUNK, CHUNK), CHUNK), :] = _rms(
                o, sub_g, SUBLN_EPS).astype(o_ref.dtype)
        return carry

    lax.fori_loop(0, SEQ // TQ, query_tile, 0)


def _diff_attention(qkv, lam_rows, sub_g, lambda_init):
    assert TQ == TK
    head_w = 2 * HEAD_DIM
    return pl.pallas_call(
        functools.partial(_attn_kernel, lambda_init=lambda_init),
        grid=(BATCH, N_HEADS),
        in_specs=[
            pl.BlockSpec((SUBLANES, HEAD_DIM), lambda b, h: (0, 0)),
            pl.BlockSpec((SEQ, head_w), lambda b, h: (b, h)),
            pl.BlockSpec((SEQ, head_w), lambda b, h: (b, N_HEADS + h)),
            pl.BlockSpec((SEQ, head_w), lambda b, h: (b, 2 * N_HEADS + h)),
            pl.BlockSpec((1, head_w), lambda b, h: (0, 0)),
        ],
        out_specs=pl.BlockSpec((SEQ, head_w), lambda b, h: (b, h)),
        out_shape=jax.ShapeDtypeStruct((N_TOK, D_MODEL), BF16),
        scratch_shapes=(
            [pltpu.VMEM((TQ, LANES), F32)] * 2
            + [pltpu.VMEM((TQ, LANES), F32)] * 2
            + [pltpu.VMEM((TQ, head_w), F32)] * 2
            + [pltpu.VMEM((TQ, TK), F32)] * 2
            + [pltpu.VMEM((TQ, TK), BF16)] * 2
            + [pltpu.VMEM((TQ, LANES), F32)] * 2),
        compiler_params=pltpu.CompilerParams(
            dimension_semantics=("arbitrary", "arbitrary"),
            vmem_limit_bytes=VMEM_LIMIT),
        name="diff_attention",
    )(lam_rows, qkv, qkv, qkv, sub_g)


def _router_params(wr_g, br_g, wr_e, br_e):
    w = jnp.zeros((D_MODEL, ROUTER_COLS), F32)
    w = w.at[:, 0:N_GROUPS].set(wr_g).at[:, E_COL0:E_COL0 + N_EXPERTS].set(wr_e)
    b = jnp.full((1, ROUTER_COLS), NEG_INF, F32)
    b = b.at[0, 0:N_GROUPS].set(br_g).at[0, E_COL0:E_COL0 + N_EXPERTS].set(br_e)
    w_hi = w.astype(BF16)
    w_lo = (w - w_hi.astype(F32)).astype(BF16)
    return jnp.stack([w_hi, w_lo]), b


def kernel(x, norm_mix_g, norm_ffn_g, pool_w, pool_scale, attn_wq, attn_wk, attn_wv, attn_wo,
           lam_q1, lam_k1, lam_q2, lam_k2, attn_sub_g, router_g_w, router_g_b, router_e_w,
           router_e_b, exp_w_gate, exp_w_up, exp_w_down, final_g):
    x2d = x.reshape(N_TOK, D_MODEL)
    row = lambda v: v.reshape(1, -1)

    wr0, br0 = _router_params(router_g_w[0], router_g_b[0], router_e_w[0], router_e_b[0])
    x1, h, ids, wts = _pool_layer(x2d, row(norm_mix_g[0]), pool_w[0].astype(BF16),
                                  row(pool_scale[0]), row(norm_ffn_g[0]), wr0, br0)
    ys, pos = _moe_layer(0, h, ids, exp_w_gate, exp_w_up, exp_w_down)
    x2, hn = _combine_norm(x1, ys, pos, wts, row(norm_mix_g[1]), emit_sum=True, norm_dtype=BF16)

    lambda_init = 0.8 - 0.6 * math.exp(-0.3 * 1)
    w_qkv = jnp.concatenate([attn_wq[0], attn_wk[0], attn_wv[0]], axis=1).astype(BF16)
    q_scale = HEAD_DIM ** -0.5 * math.log2(math.e)
    qkv = _matmul(hn, w_qkv, BF16, D_MODEL, first_block_scale=q_scale)
    lam_rows = jnp.zeros((SUBLANES, HEAD_DIM), F32)
    lam_rows = lam_rows.at[0].set(lam_q1[0]).at[1].set(lam_k1[0])
    lam_rows = lam_rows.at[2].set(lam_q2[0]).at[3].set(lam_k2[0])
    att = _diff_attention(qkv, lam_rows, row(attn_sub_g[0]), lambda_init)
    wr1, br1 = _router_params(router_g_w[1], router_g_b[1], router_e_w[1], router_e_b[1])
    x3, h, ids, wts = _proj_router(x2, att, attn_wo[0].astype(BF16), row(norm_ffn_g[1]),
                                   wr1, br1)
    ys, pos = _moe_layer(1, h, ids, exp_w_gate, exp_w_up, exp_w_down)
    (out,) = _combine_norm(x3, ys, pos, wts, row(final_g), emit_sum=False, norm_dtype=F32)
    return out.reshape(BATCH, SEQ, D_MODEL)
```

```python
import functools
import math

import jax
import jax.numpy as jnp
from jax import lax
from jax.experimental import pallas as pl
from jax.experimental.pallas import tpu as pltpu

F32 = jnp.float32
BF16 = jnp.bfloat16
I32 = jnp.int32

D_MODEL = 2048
BATCH = 8
SEQ = 2048
N_TOK = BATCH * SEQ
CHUNK = 64
POOL_WINDOWS = (2, 4, 8, 16)
POOL_CG = D_MODEL // len(POOL_WINDOWS)
POOL_HALO = 16
N_HEADS = 8
HEAD_DIM = 128
N_GROUPS = 4
EXPERTS_PER_GROUP = 8
N_EXPERTS = 32
EXPERT_FF = 512
NORM_EPS = 1e-6
SUBLN_EPS = 1e-5
NEG_INF = -1e30

SUBLANES = 8
LANES = 128
RPT = D_MODEL // LANES
RPT_PAD = RPT + 1
MXU_K = 256
ROUTER_COLS = LANES
E_COL0 = 8

TS = 256
TM = 256
N_SORT = 2 * N_TOK + N_EXPERTS * TM
N_TILES = N_SORT // TM
TQ = 512
TK = 512
MM_TM = 512
RANK_BLOCK = 2048
VMEM_LIMIT = 56 * 1024 * 1024


def _rms(x, g, eps):
    ms = jnp.mean(x * x, axis=-1, keepdims=True)
    return x * lax.rsqrt(ms + eps) * g


def _to_rows(ref, val, n, pitch=RPT):
    for c in range(RPT):
        ref[pl.ds(c, n, stride=pitch), :] = val[:, c * LANES:(c + 1) * LANES]


def _from_rows(ref, n, lo=0, hi=RPT, pitch=RPT):
    return jnp.concatenate(
        [ref[pl.ds(c, n, stride=pitch), :] for c in range(lo, hi)], axis=-1)


def _router_tail(x_new, gffn_ref, wr_ref, br_ref, h_ref, ids_ref, wts_ref):
    h = _rms(x_new, gffn_ref[...], NORM_EPS)
    h_ref[...] = h
    h_hi = h.astype(BF16)
    h_lo = (h - h_hi.astype(F32)).astype(BF16)
    hi_terms = jnp.dot(h_hi, wr_ref[...], preferred_element_type=F32)
    logits = (hi_terms[:, :ROUTER_COLS] + hi_terms[:, ROUTER_COLS:]
              + jnp.dot(h_lo, wr_ref[:, :ROUTER_COLS], preferred_element_type=F32))
    logits = logits + br_ref[...]
    lt = logits.T
    gl = lt[0:SUBLANES, :]
    io8 = lax.broadcasted_iota(I32, gl.shape, 0)
    gmax = jnp.max(gl, axis=0, keepdims=True)
    gtop = jnp.min(jnp.where(gl == gmax, io8, SUBLANES), axis=0, keepdims=True)
    p_g = 1.0 / jnp.sum(jnp.exp(gl - gmax), axis=0, keepdims=True)
    esel = lt[E_COL0:E_COL0 + EXPERTS_PER_GROUP, :]
    for gi in range(1, N_GROUPS):
        lo = E_COL0 + gi * EXPERTS_PER_GROUP
        esel = jnp.where(gtop == gi, lt[lo:lo + EXPERTS_PER_GROUP, :], esel)
    m1 = jnp.max(esel, axis=0, keepdims=True)
    i1 = jnp.min(jnp.where(esel == m1, io8, SUBLANES), axis=0, keepdims=True)
    rest = jnp.where(io8 == i1, -jnp.inf, esel)
    m2 = jnp.max(rest, axis=0, keepdims=True)
    i2 = jnp.min(jnp.where(rest == m2, io8, SUBLANES), axis=0, keepdims=True)
    r = jnp.exp(m2 - m1)
    v1 = 1.0 / (1.0 + r)
    v2 = r / (1.0 + r)
    base = gtop * EXPERTS_PER_GROUP
    ids_ref[0:1, :] = base + i1
    ids_ref[1:2, :] = base + i2
    wts_ref[0:1, :] = p_g * v1
    wts_ref[1:2, :] = p_g * v2


def _router_out_shapes():
    return (jax.ShapeDtypeStruct((N_TOK, D_MODEL), F32),
            jax.ShapeDtypeStruct((2, N_TOK), I32),
            jax.ShapeDtypeStruct((2, N_TOK), F32))


def _router_out_specs(tok_map):
    return (pl.BlockSpec((TS, D_MODEL), lambda *a: (tok_map(*a), 0)),
            pl.BlockSpec((2, TS), lambda *a: (0, tok_map(*a))),
            pl.BlockSpec((2, TS), lambda *a: (0, tok_map(*a))))


def _pool_kernel(x_ref, halo_ref, gmix_ref, wpool_ref, pscale_ref, gffn_ref, wr_ref, br_ref,
                 x1_ref, h_ref, ids_ref, wts_ref, hs_ref):
    s = pl.program_id(1)
    xt = x_ref[...]
    g = gmix_ref[...]
    h = _rms(xt, g, NORM_EPS)
    hh = _rms(halo_ref[...], g, NORM_EPS)
    hs_ref[0:POOL_HALO, :] = jnp.where(s > 0, hh, 0.0)
    hs_ref[POOL_HALO:, :] = h
    pos = s * TS + lax.broadcasted_iota(I32, (TS, 1), 0)
    outs = []
    for gi, w in enumerate(POOL_WINDOWS):
        c0 = gi * POOL_CG
        he = hs_ref[:, c0:c0 + POOL_CG]
        acc = he
        sh = 1
        while sh < w:
            acc = acc + pltpu.roll(acc, sh, axis=0)
            sh *= 2
        inv_cnt = 1.0 / jnp.minimum(pos + 1, w).astype(F32)
        mix = acc[POOL_HALO:, :] * inv_cnt - he[POOL_HALO:, :]
        outs.append(jnp.dot(mix.astype(BF16), wpool_ref[gi], preferred_element_type=F32))
    x1 = xt + jnp.concatenate(outs, axis=-1) * pscale_ref[...]
    x1_ref[...] = x1
    _router_tail(x1, gffn_ref, wr_ref, br_ref, h_ref, ids_ref, wts_ref)


def _pool_layer(x2d, gmix, wpool_bf, pscale, gffn, wr2, br):
    n_s = SEQ // TS
    tok_map = lambda b, s: b * n_s + s
    halo_blocks = TS // POOL_HALO
    const2 = lambda b, s: (0, 0)
    return pl.pallas_call(
        _pool_kernel,
        grid=(BATCH, n_s),
        in_specs=[
            pl.BlockSpec((TS, D_MODEL), lambda b, s: (tok_map(b, s), 0)),
            pl.BlockSpec((POOL_HALO, D_MODEL),
                         lambda b, s: (jnp.maximum(tok_map(b, s) * halo_blocks - 1, 0), 0)),
            pl.BlockSpec((1, D_MODEL), const2),
            pl.BlockSpec((len(POOL_WINDOWS), POOL_CG, POOL_CG), lambda b, s: (0, 0, 0)),
            pl.BlockSpec((1, D_MODEL), const2),
            pl.BlockSpec((1, D_MODEL), const2),
            pl.BlockSpec((D_MODEL, 2 * ROUTER_COLS), const2),
            pl.BlockSpec((1, ROUTER_COLS), const2),
        ],
        out_specs=(pl.BlockSpec((TS, D_MODEL), lambda b, s: (tok_map(b, s), 0)),)
        + _router_out_specs(tok_map),
        out_shape=(jax.ShapeDtypeStruct((N_TOK, D_MODEL), F32),) + _router_out_shapes(),
        scratch_shapes=[pltpu.VMEM((TS + POOL_HALO, D_MODEL), F32)],
        compiler_params=pltpu.CompilerParams(
            dimension_semantics=("arbitrary", "arbitrary"), vmem_limit_bytes=VMEM_LIMIT),
        name="pool_mixer_router",
    )(x2d, x2d, gmix, wpool_bf, pscale, gffn, wr2, br)


def _proj_router_kernel(x_ref, a_ref, wo_ref, gffn_ref, wr_ref, br_ref,
                        xo_ref, h_ref, ids_ref, wts_ref):
    x_new = x_ref[...] + jnp.dot(a_ref[...], wo_ref[...], preferred_element_type=F32)
    xo_ref[...] = x_new
    _router_tail(x_new, gffn_ref, wr_ref, br_ref, h_ref, ids_ref, wts_ref)


def _proj_router(x2d, att, wo_bf, gffn, wr2, br):
    tok_map = lambda i: i
    const2 = lambda i: (0, 0)
    return pl.pallas_call(
        _proj_router_kernel,
        grid=(N_TOK // TS,),
        in_specs=[
            pl.BlockSpec((TS, D_MODEL), lambda i: (i, 0)),
            pl.BlockSpec((TS, D_MODEL), lambda i: (i, 0)),
            pl.BlockSpec((D_MODEL, D_MODEL), const2),
            pl.BlockSpec((1, D_MODEL), const2),
            pl.BlockSpec((D_MODEL, 2 * ROUTER_COLS), const2),
            pl.BlockSpec((1, ROUTER_COLS), const2),
        ],
        out_specs=(pl.BlockSpec((TS, D_MODEL), lambda i: (i, 0)),) + _router_out_specs(tok_map),
        out_shape=(jax.ShapeDtypeStruct((N_TOK, D_MODEL), F32),) + _router_out_shapes(),
        compiler_params=pltpu.CompilerParams(
            dimension_semantics=("arbitrary",), vmem_limit_bytes=VMEM_LIMIT),
        name="proj_router",
    )(x2d, att, wo_bf, gffn, wr2, br)


def _dispatch_kernel(pad0_ref, padn_ref, nv_ref, p0_ref, p1_ref, h_ref, xs_ref,
                     hbuf, zrow, ztile, sem, sem_z):
    i = pl.program_id(0)
    n_blk = pl.num_programs(0)
    slot = i % 2

    def zero_row(p):
        return pltpu.make_async_copy(zrow.at[pl.ds(0, 1), :], xs_ref.at[pl.ds(p, 1), :], sem_z)

    def zero_tile(t):
        return pltpu.make_async_copy(
            ztile, xs_ref.at[pl.ds(pl.multiple_of(t * TM, TM), TM), :], sem_z)

    def for_each_unowned(row_fn, tile_fn):
        def per_expert(e, c):
            start = pad0_ref[e]

            def per_row(j, c2):
                row_fn(start + j)
                return c2
            return lax.fori_loop(0, padn_ref[e], per_row, c)
        lax.fori_loop(0, N_EXPERTS, per_expert, 0)

        def per_tile(t, c):
            tile_fn(t)
            return c
        lax.fori_loop(nv_ref[0], N_TILES, per_tile, 0)

    @pl.when(i == 0)
    def _():
        zrow[...] = jnp.zeros(zrow.shape, zrow.dtype)
        ztile[...] = jnp.zeros(ztile.shape, ztile.dtype)
        for_each_unowned(lambda p: zero_row(p).start(), lambda t: zero_tile(t).start())

    def send_wait(s):
        for _ in range(2):
            pltpu.make_async_copy(hbuf.at[s], xs_ref.at[pl.ds(0, TS), :], sem.at[s]).wait()

    @pl.when(i >= 2)
    def _():
        send_wait(slot)

    hbuf[slot] = h_ref[...]
    for k, p_ref in enumerate((p0_ref, p1_ref)):
        for r in range(TS):
            pltpu.make_async_copy(hbuf.at[slot, pl.ds(r, 1), :],
                                  xs_ref.at[pl.ds(p_ref[0, 0, r], 1), :],
                                  sem.at[slot]).start(priority=(r + k) % 2)

    @pl.when(i == n_blk - 1)
    def _():
        send_wait(1 - slot)
        send_wait(slot)
        for_each_unowned(lambda p: zero_row(p).wait(), lambda t: zero_tile(t).wait())


def _dispatch(h, pos, pad_start, pad_len, n_valid):
    n_blk = N_TOK // TS
    idx_blk = lambda: pl.BlockSpec((1, 1, TS), lambda i, *_: (i, 0, 0), memory_space=pltpu.SMEM)
    pos3 = [pos[k].reshape(n_blk, 1, TS) for k in range(2)]
    grid_spec = pltpu.PrefetchScalarGridSpec(
        num_scalar_prefetch=3,
        grid=(n_blk,),
        in_specs=[idx_blk(), idx_blk(),
                  pl.BlockSpec((TS, D_MODEL), lambda i, *_: (i, 0))],
        out_specs=pl.BlockSpec(memory_space=pl.ANY),
        scratch_shapes=[pltpu.VMEM((2, TS, D_MODEL), F32),
                        pltpu.VMEM((SUBLANES, D_MODEL), F32),
                        pltpu.VMEM((TM, D_MODEL), F32),
                        pltpu.SemaphoreType.DMA((2,)),
                        pltpu.SemaphoreType.DMA(())],
    )
    return pl.pallas_call(
        _dispatch_kernel,
        grid_spec=grid_spec,
        out_shape=jax.ShapeDtypeStruct((N_SORT, D_MODEL), F32),
        compiler_params=pltpu.CompilerParams(
            dimension_semantics=("arbitrary",), vmem_limit_bytes=VMEM_LIMIT),
        name="dispatch_rows",
    )(pad_start, pad_len, n_valid, pos3[0], pos3[1], h)


def _moe_kernel(te_ref, nv_ref, first_ref, ord_ref, nxt_ref, x_ref, wg_hbm, wu_hbm, wd_hbm, o_ref,
                wg_f32, wu_f32, wd_f32, wg_bf, wu_bf, wd_bf, sem_w, *, layer):
    i = pl.program_id(0)
    nv = nv_ref[0]

    def weight_copies(e, s):
        return [pltpu.make_async_copy(hbm.at[layer, e], buf.at[s], sem_w.at[s])
                for hbm, buf in ((wg_hbm, wg_f32), (wu_hbm, wu_f32), (wd_hbm, wd_f32))]

    @pl.when(i == 0)
    def _():
        for cp in weight_copies(te_ref[0], 0):
            cp.start()

    @pl.when((i < nv) & (first_ref[i] > 0))
    def _():
        e = te_ref[i]
        s = ord_ref[i] % 2
        for cp in weight_copies(e, s):
            cp.wait()
        wg_bf[...] = wg_f32[s].astype(BF16)
        wu_bf[...] = wu_f32[s].astype(BF16)
        wd_bf[...] = wd_f32[s].astype(BF16)
        e_next = nxt_ref[i]

        @pl.when(e_next != e)
        def _():
            for cp in weight_copies(e_next, 1 - s):
                cp.start()

    @pl.when(i < nv)
    def _():
        x = x_ref[...].astype(BF16)
        g = jnp.dot(x, wg_bf[...], preferred_element_type=F32)
        u = jnp.dot(x, wu_bf[...], preferred_element_type=F32)
        hid = (g / (1.0 + jnp.exp(-g)) * u).astype(BF16)
        o_ref[...] = jnp.dot(hid, wd_bf[...], preferred_element_type=F32)

    @pl.when(i >= nv)
    def _():
        o_ref[...] = jnp.zeros(o_ref.shape, o_ref.dtype)


def _moe_grouped(layer, tiles, xs, w_gate, w_up, w_down):
    tile_expert, n_valid, first, ordinal, next_expert = tiles
    grid_spec = pltpu.PrefetchScalarGridSpec(
        num_scalar_prefetch=5,
        grid=(N_TILES,),
        in_specs=[
            pl.BlockSpec((TM, D_MODEL), lambda i, te, nv, *_: (jnp.minimum(i, nv[0] - 1), 0)),
            pl.BlockSpec(memory_space=pl.ANY),
            pl.BlockSpec(memory_space=pl.ANY),
            pl.BlockSpec(memory_space=pl.ANY),
        ],
        out_specs=pl.BlockSpec((TM, D_MODEL), lambda i, *_: (i, 0)),
        scratch_shapes=[pltpu.VMEM((2, D_MODEL, EXPERT_FF), F32),
                        pltpu.VMEM((2, D_MODEL, EXPERT_FF), F32),
                        pltpu.VMEM((2, EXPERT_FF, D_MODEL), F32),
                        pltpu.VMEM((D_MODEL, EXPERT_FF), BF16),
                        pltpu.VMEM((D_MODEL, EXPERT_FF), BF16),
                        pltpu.VMEM((EXPERT_FF, D_MODEL), BF16),
                        pltpu.SemaphoreType.DMA((2,))],
    )
    return pl.pallas_call(
        functools.partial(_moe_kernel, layer=layer),
        grid_spec=grid_spec,
        out_shape=jax.ShapeDtypeStruct((N_SORT, D_MODEL), F32),
        compiler_params=pltpu.CompilerParams(
            dimension_semantics=("arbitrary",), vmem_limit_bytes=VMEM_LIMIT),
        name="moe_grouped",
    )(tile_expert, n_valid, first, ordinal, next_expert, xs, w_gate, w_up, w_down)


def _rank_kernel(e_ref, tri_ref, rank_ref, cnt_ref, carry):
    b = pl.program_id(0)

    @pl.when(b == 0)
    def _():
        carry[...] = jnp.zeros(carry.shape, carry.dtype)

    e = e_ref[...]
    is_e = lax.broadcasted_iota(I32, (N_EXPERTS, RANK_BLOCK), 0) == e
    onehot = jnp.where(is_e, 1.0, 0.0).astype(BF16)
    prefix = jnp.dot(onehot, tri_ref[...], preferred_element_type=F32)
    before = carry[...]
    seen = prefix + before[:, 0:1]
    rank = jnp.sum(jnp.where(is_e, seen, 0.0), axis=0, keepdims=True) - 1.0
    rank_ref[...] = rank.astype(I32)
    total = before + prefix[:, RANK_BLOCK - 1:RANK_BLOCK]
    carry[...] = total
    cnt_ref[...] = total.astype(I32)


def _rank_in_expert(e_flat):
    n = e_flat.shape[1]
    tri = jnp.triu(jnp.ones((RANK_BLOCK, RANK_BLOCK), BF16))
    rank, cnt = pl.pallas_call(
        _rank_kernel,
        grid=(n // RANK_BLOCK,),
        in_specs=[pl.BlockSpec((1, RANK_BLOCK), lambda b: (0, b)),
                  pl.BlockSpec((RANK_BLOCK, RANK_BLOCK), lambda b: (0, 0))],
        out_specs=(pl.BlockSpec((1, RANK_BLOCK), lambda b: (0, b)),
                   pl.BlockSpec((N_EXPERTS, LANES), lambda b: (0, 0))),
        out_shape=(jax.ShapeDtypeStruct((1, n), I32),
                   jax.ShapeDtypeStruct((N_EXPERTS, LANES), I32)),
        scratch_shapes=[pltpu.VMEM((N_EXPERTS, LANES), F32)],
        compiler_params=pltpu.CompilerParams(
            dimension_semantics=("arbitrary",), vmem_limit_bytes=VMEM_LIMIT),
        name="rank_in_expert",
    )(e_flat, tri)
    return rank, cnt[:, 0]


def _route_plan(ids):
    e = ids.reshape(-1)
    rank, counts = _rank_in_expert(ids.reshape(1, -1))
    padded = (counts + TM - 1) // TM * TM
    ends = jnp.cumsum(padded)
    offs = ends - padded
    is_e = e[:, None] == jnp.arange(N_EXPERTS, dtype=I32)[None, :]
    pos = jnp.sum(jnp.where(is_e, offs[None, :], 0), axis=1) + rank.reshape(-1)
    pos = pos.reshape(2, N_TOK)
    n_valid = (ends[-1] // TM).astype(I32)
    tile = jnp.arange(N_TILES, dtype=I32)
    tile_start = jnp.minimum(tile, n_valid - 1) * TM
    n_done = jnp.sum((ends[None, :] <= tile_start[:, None]).astype(I32), axis=1)
    tile_expert = jnp.minimum(n_done, N_EXPERTS - 1)
    prev_expert = jnp.concatenate([jnp.full((1,), -1, I32), tile_expert[:-1]])
    first = ((tile_expert != prev_expert) & (tile < n_valid)).astype(I32)
    ordinal = jnp.cumsum(first) - 1
    expert = jnp.arange(N_EXPERTS, dtype=I32)
    used_id = jnp.where(counts > 0, expert, N_EXPERTS)
    later = expert[None, :] > expert[:, None]
    next_used = jnp.min(jnp.where(later, used_id[None, :], N_EXPERTS), axis=1)
    next_used = jnp.where(next_used < N_EXPERTS, next_used, expert)
    next_expert = jnp.sum(jnp.where(tile_expert[:, None] == expert[None, :],
                                    next_used[None, :], 0), axis=1)
    tiles = (tile_expert, n_valid.reshape(1), first, ordinal, next_expert)
    return pos, offs + counts, padded - counts, tiles


def _moe_layer(layer, h, ids, w_gate, w_up, w_down):
    pos, pad_start, pad_len, tiles = _route_plan(ids)
    xs = _dispatch(h, pos, pad_start, pad_len, tiles[1])
    ys = _moe_grouped(layer, tiles, xs, w_gate, w_up, w_down)
    return ys, pos


def _combine_norm_kernel(p0f_ref, p1f_ref, p0n_ref, p1n_ref, x_ref, w_ref, ys_ref, g_ref, *rest,
                         emit_sum):
    out_refs, (ybuf, sem) = rest[:-2], rest[-2:]
    i = pl.program_id(0)
    n_blk = pl.num_programs(0)
    slot = i % 2
    other = 1 - slot

    def fetch(idx_refs, s):
        for k in range(2):
            for r in range(TS):
                pltpu.make_async_copy(ys_ref.at[pl.ds(idx_refs[k][0, 0, r], 1), :],
                                      ybuf.at[s, k, pl.ds(r, 1), :],
                                      sem.at[s]).start(priority=r % 2)

    @pl.when(i == 0)
    def _():
        fetch((p0f_ref, p1f_ref), 0)

    for k in range(2):
        pltpu.make_async_copy(ys_ref.at[pl.ds(0, TS), :], ybuf.at[slot, k], sem.at[slot]).wait()

    @pl.when(i + 1 < n_blk)
    def _():
        fetch((p0n_ref, p1n_ref), other)

    w = w_ref[...]
    eye = (lax.broadcasted_iota(I32, (TS, TS), 0) == lax.broadcasted_iota(I32, (TS, TS), 1))
    x_new = x_ref[...]
    for k in range(2):
        w_col = jnp.sum(jnp.where(eye, w[k:k + 1, :], 0.0), axis=1, keepdims=True)
        x_new = x_new + w_col * ybuf[slot, k]
    normed = _rms(x_new, g_ref[...], NORM_EPS)
    if emit_sum:
        out_refs[0][...] = x_new
        out_refs[1][...] = normed.astype(out_refs[1].dtype)
    else:
        out_refs[0][...] = normed.astype(out_refs[0].dtype)


def _combine_norm(x2d, ys, pos, wts, g, *, emit_sum, norm_dtype):
    n_blk = N_TOK // TS
    tok_spec = pl.BlockSpec((TS, D_MODEL), lambda i: (i, 0))
    out_specs = [tok_spec]
    out_shape = [jax.ShapeDtypeStruct((N_TOK, D_MODEL), norm_dtype)]
    if emit_sum:
        out_specs = [tok_spec, tok_spec]
        out_shape = [jax.ShapeDtypeStruct((N_TOK, D_MODEL), F32)] + out_shape
    idx_blk = lambda f: pl.BlockSpec((1, 1, TS), f, memory_space=pltpu.SMEM)
    first = lambda i: (0, 0, 0)
    nxt = lambda i: (jnp.minimum(i + 1, n_blk - 1), 0, 0)
    pos3 = [pos[k].reshape(n_blk, 1, TS) for k in range(2)]
    return pl.pallas_call(
        functools.partial(_combine_norm_kernel, emit_sum=emit_sum),
        grid=(n_blk,),
        in_specs=[
            idx_blk(first), idx_blk(first), idx_blk(nxt), idx_blk(nxt),
            tok_spec,
            pl.BlockSpec((2, TS), lambda i: (0, i)),
            pl.BlockSpec(memory_space=pl.ANY),
            pl.BlockSpec((1, D_MODEL), lambda i: (0, 0)),
        ],
        out_specs=tuple(out_specs),
        out_shape=tuple(out_shape),
        scratch_shapes=[pltpu.VMEM((2, 2, TS, D_MODEL), F32),
                        pltpu.SemaphoreType.DMA((2,))],
        compiler_params=pltpu.CompilerParams(
            dimension_semantics=("arbitrary",), vmem_limit_bytes=VMEM_LIMIT),
        name="combine_norm",
    )(pos3[0], pos3[1], pos3[0], pos3[1], x2d, wts, ys, g)


def _mm_kernel(a_ref, b_ref, o_ref, *, first_block_scale):
    acc = jnp.dot(a_ref[...], b_ref[...], preferred_element_type=F32)
    if first_block_scale is not None:
        acc = acc * jnp.where(pl.program_id(0) == 0, first_block_scale, 1.0)
    o_ref[...] = acc.astype(o_ref.dtype)


def _matmul(a, b, out_dtype, tn, first_block_scale=None):
    m, k = a.shape
    n = b.shape[1]
    return pl.pallas_call(
        functools.partial(_mm_kernel, first_block_scale=first_block_scale),
        grid=(n // tn, m // MM_TM),
        in_specs=[pl.BlockSpec((MM_TM, k), lambda j, i: (i, 0)),
                  pl.BlockSpec((k, tn), lambda j, i: (0, j))],
        out_specs=pl.BlockSpec((MM_TM, tn), lambda j, i: (i, j)),
        out_shape=jax.ShapeDtypeStruct((m, n), out_dtype),
        compiler_params=pltpu.CompilerParams(
            dimension_semantics=("arbitrary", "arbitrary"), vmem_limit_bytes=VMEM_LIMIT),
        name="dense_matmul",
    )(a, b)


def _attn_kernel(lam_ref, q_ref, k_ref, v_ref, subg_ref, o_ref, *scratch, lambda_init):
    m_ref, l_ref, acc_ref, s_ref, p_ref, a_ref = (scratch[2 * n:2 * n + 2] for n in range(6))
    lane_rep = lambda a, width: jnp.concatenate([a] * (width // LANES), axis=1)
    lv = lam_ref[...]
    lam = (jnp.exp(jnp.sum(lv[0:1] * lv[1:2], axis=-1, keepdims=True))
           - jnp.exp(jnp.sum(lv[2:3] * lv[3:4], axis=-1, keepdims=True)) + lambda_init)
    width = 2 * HEAD_DIM
    sub_g = subg_ref[...] * (1.0 - lambda_init)

    def query_tile(qi, carry):
        q0 = pl.multiple_of(qi * TQ, TQ)
        q = q_ref[pl.ds(q0, TQ), :]
        qs = (q[:, :HEAD_DIM], q[:, HEAD_DIM:])
        for c in range(2):
            m_ref[c][...] = jnp.full(m_ref[c].shape, NEG_INF, F32)
            l_ref[c][...] = jnp.zeros(l_ref[c].shape, F32)
            acc_ref[c][...] = jnp.zeros(acc_ref[c].shape, F32)

        def tile_update(j, diagonal):
            k0 = pl.multiple_of(j * TK, TK)
            kt = k_ref[pl.ds(k0, TK), :]
            vt = v_ref[pl.ds(k0, TK), :]
            for c in range(2):
                kc = kt[:, c * HEAD_DIM:(c + 1) * HEAD_DIM]
                s_ref[c][...] = lax.dot_general(qs[c], kc, (((1,), (1,)), ((), ())),
                                                preferred_element_type=F32)
            for c in range(2):
                for rb in range(TQ // CHUNK):
                    rows = pl.ds(rb * CHUNK, CHUNK)
                    ncol = TK
                    if diagonal:
                        seen = (rb + 1) * CHUNK
                        ncol = min(TK, -(-seen // LANES) * LANES)
                    s = s_ref[c][rows, :ncol]
                    if diagonal and seen < ncol:
                        col = lax.broadcasted_iota(I32, (CHUNK, ncol), 1)
                        s = jnp.where(col < seen, s, NEG_INF)
                    m_old = m_ref[c][rows, :]
                    m_new = jnp.maximum(m_old, jnp.max(s, axis=-1, keepdims=True))
                    alpha = jnp.exp2(m_old - m_new)
                    p = jnp.exp2(s - lane_rep(m_new, ncol))
                    l_ref[c][rows, :] = (alpha * l_ref[c][rows, :]
                                         + jnp.sum(p, axis=-1, keepdims=True))
                    m_ref[c][rows, :] = m_new
                    a_ref[c][rows, :] = alpha
                    p_ref[c][rows, :ncol] = p.astype(BF16)
                    if ncol < TK:
                        p_ref[c][rows, ncol:] = jnp.zeros((CHUNK, TK - ncol), BF16)
                acc_ref[c][...] = (lane_rep(a_ref[c][...], width) * acc_ref[c][...]
                                   + jnp.dot(p_ref[c][...], vt, preferred_element_type=F32))

        def full_body(j, c):
            tile_update(j, False)
            return c

        lax.fori_loop(0, qi, full_body, 0)
        tile_update(qi, True)

        for rb in range(TQ // CHUNK):
            rows = pl.ds(rb * CHUNK, CHUNK)
            o = (acc_ref[0][rows, :] * lane_rep(1.0 / l_ref[0][rows, :], width)
                 - acc_ref[1][rows, :] * lane_rep(lam * (1.0 / l_ref[1][rows, :]), width))
            o_ref[pl.ds(pl.multiple_of(q0 + rb * CHUNK, CHUNK), CHUNK), :] = _rms(
                o, sub_g, SUBLN_EPS).astype(o_ref.dtype)
        return carry

    lax.fori_loop(0, SEQ // TQ, query_tile, 0)


def _diff_attention(qkv, lam_rows, sub_g, lambda_init):
    assert TQ == TK
    head_w = 2 * HEAD_DIM
    return pl.pallas_call(
        functools.partial(_attn_kernel, lambda_init=lambda_init),
        grid=(BATCH, N_HEADS),
        in_specs=[
            pl.BlockSpec((SUBLANES, HEAD_DIM), lambda b, h: (0, 0)),
            pl.BlockSpec((SEQ, head_w), lambda b, h: (b, h)),
            pl.BlockSpec((SEQ, head_w), lambda b, h: (b, N_HEADS + h)),
            pl.BlockSpec((SEQ, head_w), lambda b, h: (b, 2 * N_HEADS + h)),
            pl.BlockSpec((1, head_w), lambda b, h: (0, 0)),
        ],
        out_specs=pl.BlockSpec((SEQ, head_w), lambda b, h: (b, h)),
        out_shape=jax.ShapeDtypeStruct((N_TOK, D_MODEL), BF16),
        scratch_shapes=(
            [pltpu.VMEM((TQ, LANES), F32)] * 2
            + [pltpu.VMEM((TQ, LANES), F32)] * 2
            + [pltpu.VMEM((TQ, head_w), F32)] * 2
            + [pltpu.VMEM((TQ, TK), F32)] * 2
            + [pltpu.VMEM((TQ, TK), BF16)] * 2
            + [pltpu.VMEM((TQ, LANES), F32)] * 2),
        compiler_params=pltpu.CompilerParams(
            dimension_semantics=("arbitrary", "arbitrary"),
            vmem_limit_bytes=VMEM_LIMIT),
        name="diff_attention",
    )(lam_rows, qkv, qkv, qkv, sub_g)


def _router_params(wr_g, br_g, wr_e, br_e):
    w = jnp.zeros((D_MODEL, ROUTER_COLS), F32)
    w = w.at[:, 0:N_GROUPS].set(wr_g).at[:, E_COL0:E_COL0 + N_EXPERTS].set(wr_e)
    b = jnp.full((1, ROUTER_COLS), NEG_INF, F32)
    b = b.at[0, 0:N_GROUPS].set(br_g).at[0, E_COL0:E_COL0 + N_EXPERTS].set(br_e)
    w_hi = w.astype(BF16)
    w_lo = (w - w_hi.astype(F32)).astype(BF16)
    return jnp.concatenate([w_hi, w_lo], axis=1), b


def kernel(x, norm_mix_g, norm_ffn_g, pool_w, pool_scale, attn_wq, attn_wk, attn_wv, attn_wo,
           lam_q1, lam_k1, lam_q2, lam_k2, attn_sub_g, router_g_w, router_g_b, router_e_w,
           router_e_b, exp_w_gate, exp_w_up, exp_w_down, final_g):
    x2d = x.reshape(N_TOK, D_MODEL)
    row = lambda v: v.reshape(1, -1)

    wr0, br0 = _router_params(router_g_w[0], router_g_b[0], router_e_w[0], router_e_b[0])
    x1, h, ids, wts = _pool_layer(x2d, row(norm_mix_g[0]), pool_w[0].astype(BF16),
                                  row(pool_scale[0]), row(norm_ffn_g[0]), wr0, br0)
    ys, pos = _moe_layer(0, h, ids, exp_w_gate, exp_w_up, exp_w_down)
    x2, hn = _combine_norm(x1, ys, pos, wts, row(norm_mix_g[1]), emit_sum=True, norm_dtype=BF16)

    lambda_init = 0.8 - 0.6 * math.exp(-0.3 * 1)
    w_qkv = jnp.concatenate([attn_wq[0], attn_wk[0], attn_wv[0]], axis=1).astype(BF16)
    q_scale = HEAD_DIM ** -0.5 * math.log2(math.e)
    qkv = _matmul(hn, w_qkv, BF16, D_MODEL, first_block_scale=q_scale)
    lam_rows = jnp.zeros((SUBLANES, HEAD_DIM), F32)
    lam_rows = lam_rows.at[0].set(lam_q1[0]).at[1].set(lam_k1[0])
    lam_rows = lam_rows.at[2].set(lam_q2[0]).at[3].set(lam_k2[0])
    att = _diff_attention(qkv, lam_rows, row(attn_sub_g[0]), lambda_init)
    wr1, br1 = _router_params(router_g_w[1], router_g_b[1], router_e_w[1], router_e_b[1])
    x3, h, ids, wts = _proj_router(x2, att, attn_wo[0].astype(BF16), row(norm_ffn_g[1]),
                                   wr1, br1)
    ys, pos = _moe_layer(1, h, ids, exp_w_gate, exp_w_up, exp_w_down)
    (out,) = _combine_norm(x3, ys, pos, wts, row(final_g), emit_sum=False, norm_dtype=F32)
    return out.reshape(BATCH, SEQ, D_MODEL)
```

```python
import functools
import math

import jax
import jax.numpy as jnp
from jax import lax
from jax.experimental import pallas as pl
from jax.experimental.pallas import tpu as pltpu

F32 = jnp.float32
BF16 = jnp.bfloat16
I32 = jnp.int32

D_MODEL = 2048
BATCH = 8
SEQ = 2048
N_TOK = BATCH * SEQ
CHUNK = 64
POOL_WINDOWS = (2, 4, 8, 16)
POOL_CG = D_MODEL // len(POOL_WINDOWS)
POOL_HALO = 16
N_HEADS = 8
HEAD_DIM = 128
N_GROUPS = 4
EXPERTS_PER_GROUP = 8
N_EXPERTS = 32
EXPERT_FF = 512
NORM_EPS = 1e-6
SUBLN_EPS = 1e-5
NEG_INF = -1e30

SUBLANES = 8
LANES = 128
RPT = D_MODEL // LANES
RPT_PAD = RPT + 1
MXU_K = 256
ROUTER_COLS = LANES
E_COL0 = 8

TS = 256
TD = 512
TM = 256
N_SORT = 2 * N_TOK + N_EXPERTS * TM
N_TILES = N_SORT // TM
TQ = 512
TK = 512
MM_TM = 512
RANK_BLOCK = 2048
VMEM_LIMIT = 56 * 1024 * 1024


def _rms(x, g, eps):
    ms = jnp.mean(x * x, axis=-1, keepdims=True)
    return x * lax.rsqrt(ms + eps) * g


def _to_rows(ref, val, n, pitch=RPT):
    for c in range(RPT):
        ref[pl.ds(c, n, stride=pitch), :] = val[:, c * LANES:(c + 1) * LANES]


def _from_rows(ref, n, lo=0, hi=RPT, pitch=RPT):
    return jnp.concatenate(
        [ref[pl.ds(c, n, stride=pitch), :] for c in range(lo, hi)], axis=-1)


def _router_tail(x_new, gffn_ref, wr_ref, br_ref, h_ref, ids_ref, wts_ref):
    h = _rms(x_new, gffn_ref[...], NORM_EPS)
    h_ref[...] = h
    h_hi = h.astype(BF16)
    h_lo = (h - h_hi.astype(F32)).astype(BF16)
    hi_terms = jnp.dot(h_hi, wr_ref[...], preferred_element_type=F32)
    logits = (hi_terms[:, :ROUTER_COLS] + hi_terms[:, ROUTER_COLS:]
              + jnp.dot(h_lo, wr_ref[:, :ROUTER_COLS], preferred_element_type=F32))
    logits = logits + br_ref[...]
    lt = logits.T
    gl = lt[0:SUBLANES, :]
    io8 = lax.broadcasted_iota(I32, gl.shape, 0)
    gmax = jnp.max(gl, axis=0, keepdims=True)
    gtop = jnp.min(jnp.where(gl == gmax, io8, SUBLANES), axis=0, keepdims=True)
    p_g = 1.0 / jnp.sum(jnp.exp(gl - gmax), axis=0, keepdims=True)
    esel = lt[E_COL0:E_COL0 + EXPERTS_PER_GROUP, :]
    for gi in range(1, N_GROUPS):
        lo = E_COL0 + gi * EXPERTS_PER_GROUP
        esel = jnp.where(gtop == gi, lt[lo:lo + EXPERTS_PER_GROUP, :], esel)
    m1 = jnp.max(esel, axis=0, keepdims=True)
    i1 = jnp.min(jnp.where(esel == m1, io8, SUBLANES), axis=0, keepdims=True)
    rest = jnp.where(io8 == i1, -jnp.inf, esel)
    m2 = jnp.max(rest, axis=0, keepdims=True)
    i2 = jnp.min(jnp.where(rest == m2, io8, SUBLANES), axis=0, keepdims=True)
    r = jnp.exp(m2 - m1)
    v1 = 1.0 / (1.0 + r)
    v2 = r / (1.0 + r)
    base = gtop * EXPERTS_PER_GROUP
    ids_ref[0:1, :] = base + i1
    ids_ref[1:2, :] = base + i2
    wts_ref[0:1, :] = p_g * v1
    wts_ref[1:2, :] = p_g * v2


def _router_out_shapes():
    return (jax.ShapeDtypeStruct((N_TOK, D_MODEL), F32),
            jax.ShapeDtypeStruct((2, N_TOK), I32),
            jax.ShapeDtypeStruct((2, N_TOK), F32))


def _router_out_specs(tok_map):
    return (pl.BlockSpec((TS, D_MODEL), lambda *a: (tok_map(*a), 0)),
            pl.BlockSpec((2, TS), lambda *a: (0, tok_map(*a))),
            pl.BlockSpec((2, TS), lambda *a: (0, tok_map(*a))))


def _pool_kernel(x_ref, halo_ref, gmix_ref, wpool_ref, pscale_ref, gffn_ref, wr_ref, br_ref,
                 x1_ref, h_ref, ids_ref, wts_ref, hs_ref):
    s = pl.program_id(1)
    xt = x_ref[...]
    g = gmix_ref[...]
    h = _rms(xt, g, NORM_EPS)
    hh = _rms(halo_ref[...], g, NORM_EPS)
    hs_ref[0:POOL_HALO, :] = jnp.where(s > 0, hh, 0.0)
    hs_ref[POOL_HALO:, :] = h
    pos = s * TS + lax.broadcasted_iota(I32, (TS, 1), 0)
    outs = []
    for gi, w in enumerate(POOL_WINDOWS):
        c0 = gi * POOL_CG
        he = hs_ref[:, c0:c0 + POOL_CG]
        acc = he
        sh = 1
        while sh < w:
            acc = acc + pltpu.roll(acc, sh, axis=0)
            sh *= 2
        inv_cnt = 1.0 / jnp.minimum(pos + 1, w).astype(F32)
        mix = acc[POOL_HALO:, :] * inv_cnt - he[POOL_HALO:, :]
        outs.append(jnp.dot(mix.astype(BF16), wpool_ref[gi], preferred_element_type=F32))
    x1 = xt + jnp.concatenate(outs, axis=-1) * pscale_ref[...]
    x1_ref[...] = x1
    _router_tail(x1, gffn_ref, wr_ref, br_ref, h_ref, ids_ref, wts_ref)


def _pool_layer(x2d, gmix, wpool_bf, pscale, gffn, wr2, br):
    n_s = SEQ // TS
    tok_map = lambda b, s: b * n_s + s
    halo_blocks = TS // POOL_HALO
    const2 = lambda b, s: (0, 0)
    return pl.pallas_call(
        _pool_kernel,
        grid=(BATCH, n_s),
        in_specs=[
            pl.BlockSpec((TS, D_MODEL), lambda b, s: (tok_map(b, s), 0)),
            pl.BlockSpec((POOL_HALO, D_MODEL),
                         lambda b, s: (jnp.maximum(tok_map(b, s) * halo_blocks - 1, 0), 0)),
            pl.BlockSpec((1, D_MODEL), const2),
            pl.BlockSpec((len(POOL_WINDOWS), POOL_CG, POOL_CG), lambda b, s: (0, 0, 0)),
            pl.BlockSpec((1, D_MODEL), const2),
            pl.BlockSpec((1, D_MODEL), const2),
            pl.BlockSpec((D_MODEL, 2 * ROUTER_COLS), const2),
            pl.BlockSpec((1, ROUTER_COLS), const2),
        ],
        out_specs=(pl.BlockSpec((TS, D_MODEL), lambda b, s: (tok_map(b, s), 0)),)
        + _router_out_specs(tok_map),
        out_shape=(jax.ShapeDtypeStruct((N_TOK, D_MODEL), F32),) + _router_out_shapes(),
        scratch_shapes=[pltpu.VMEM((TS + POOL_HALO, D_MODEL), F32)],
        compiler_params=pltpu.CompilerParams(
            dimension_semantics=("arbitrary", "arbitrary"), vmem_limit_bytes=VMEM_LIMIT),
        name="pool_mixer_router",
    )(x2d, x2d, gmix, wpool_bf, pscale, gffn, wr2, br)


def _proj_router_kernel(x_ref, a_ref, wo_ref, gffn_ref, wr_ref, br_ref,
                        xo_ref, h_ref, ids_ref, wts_ref):
    x_new = x_ref[...] + jnp.dot(a_ref[...], wo_ref[...], preferred_element_type=F32)
    xo_ref[...] = x_new
    _router_tail(x_new, gffn_ref, wr_ref, br_ref, h_ref, ids_ref, wts_ref)


def _proj_router(x2d, att, wo_bf, gffn, wr2, br):
    tok_map = lambda i: i
    const2 = lambda i: (0, 0)
    return pl.pallas_call(
        _proj_router_kernel,
        grid=(N_TOK // TS,),
        in_specs=[
            pl.BlockSpec((TS, D_MODEL), lambda i: (i, 0)),
            pl.BlockSpec((TS, D_MODEL), lambda i: (i, 0)),
            pl.BlockSpec((D_MODEL, D_MODEL), const2),
            pl.BlockSpec((1, D_MODEL), const2),
            pl.BlockSpec((D_MODEL, 2 * ROUTER_COLS), const2),
            pl.BlockSpec((1, ROUTER_COLS), const2),
        ],
        out_specs=(pl.BlockSpec((TS, D_MODEL), lambda i: (i, 0)),) + _router_out_specs(tok_map),
        out_shape=(jax.ShapeDtypeStruct((N_TOK, D_MODEL), F32),) + _router_out_shapes(),
        compiler_params=pltpu.CompilerParams(
            dimension_semantics=("arbitrary",), vmem_limit_bytes=VMEM_LIMIT),
        name="proj_router",
    )(x2d, att, wo_bf, gffn, wr2, br)


def _dispatch_kernel(pad0_ref, padn_ref, nv_ref, p0_ref, p1_ref, h_ref, xs_ref,
                     hbuf, zrow, ztile, sem, sem_z):
    i = pl.program_id(0)
    n_blk = pl.num_programs(0)
    slot = i % 2

    def zero_row(p):
        return pltpu.make_async_copy(zrow.at[pl.ds(0, 1), :], xs_ref.at[pl.ds(p, 1), :], sem_z)

    def zero_rows8(p):
        return pltpu.make_async_copy(
            zrow, xs_ref.at[pl.ds(pl.multiple_of(p, SUBLANES), SUBLANES), :], sem_z)

    def zero_tile(t):
        return pltpu.make_async_copy(
            ztile, xs_ref.at[pl.ds(pl.multiple_of(t * TM, TM), TM), :], sem_z)

    def for_each_unowned(row_fn, rows8_fn, tile_fn):
        def per_expert(e, c):
            start = pad0_ref[e]
            n_single = (-start) % SUBLANES
            n_rows8 = (padn_ref[e] - n_single) // SUBLANES

            def per_row(j, c2):
                row_fn(start + j)
                return c2

            def per_rows8(j, c2):
                rows8_fn(start + n_single + j * SUBLANES)
                return c2
            lax.fori_loop(0, n_single, per_row, c)
            return lax.fori_loop(0, n_rows8, per_rows8, c)
        lax.fori_loop(0, N_EXPERTS, per_expert, 0)

        def per_tile(t, c):
            tile_fn(t)
            return c
        lax.fori_loop(nv_ref[0], N_TILES, per_tile, 0)

    @pl.when(i == 0)
    def _():
        zrow[...] = jnp.zeros(zrow.shape, zrow.dtype)
        ztile[...] = jnp.zeros(ztile.shape, ztile.dtype)
        for_each_unowned(lambda p: zero_row(p).start(), lambda p: zero_rows8(p).start(),
                         lambda t: zero_tile(t).start())

    def send_wait(s):
        for _ in range(2):
            pltpu.make_async_copy(hbuf.at[s], xs_ref.at[pl.ds(0, TD), :], sem.at[s]).wait()

    @pl.when(i >= 2)
    def _():
        send_wait(slot)

    hbuf[slot] = h_ref[...]
    for k, p_ref in enumerate((p0_ref, p1_ref)):
        for r in range(TD):
            pltpu.make_async_copy(hbuf.at[slot, pl.ds(r, 1), :],
                                  xs_ref.at[pl.ds(p_ref[0, 0, r], 1), :],
                                  sem.at[slot]).start(priority=(r + k) % 2)

    @pl.when(i == n_blk - 1)
    def _():
        send_wait(1 - slot)
        send_wait(slot)
        for_each_unowned(lambda p: zero_row(p).wait(), lambda p: zero_rows8(p).wait(),
                         lambda t: zero_tile(t).wait())


def _dispatch(h, pos, pad_start, pad_len, n_valid):
    n_blk = N_TOK // TD
    idx_blk = lambda: pl.BlockSpec((1, 1, TD), lambda i, *_: (i, 0, 0), memory_space=pltpu.SMEM)
    pos3 = [pos[k].reshape(n_blk, 1, TD) for k in range(2)]
    grid_spec = pltpu.PrefetchScalarGridSpec(
        num_scalar_prefetch=3,
        grid=(n_blk,),
        in_specs=[idx_blk(), idx_blk(),
                  pl.BlockSpec((TD, D_MODEL), lambda i, *_: (i, 0))],
        out_specs=pl.BlockSpec(memory_space=pl.ANY),
        scratch_shapes=[pltpu.VMEM((2, TD, D_MODEL), F32),
                        pltpu.VMEM((SUBLANES, D_MODEL), F32),
                        pltpu.VMEM((TM, D_MODEL), F32),
                        pltpu.SemaphoreType.DMA((2,)),
                        pltpu.SemaphoreType.DMA(())],
    )
    return pl.pallas_call(
        _dispatch_kernel,
        grid_spec=grid_spec,
        out_shape=jax.ShapeDtypeStruct((N_SORT, D_MODEL), F32),
        compiler_params=pltpu.CompilerParams(
            dimension_semantics=("arbitrary",), vmem_limit_bytes=VMEM_LIMIT),
        name="dispatch_rows",
    )(pad_start, pad_len, n_valid, pos3[0], pos3[1], h)


def _moe_kernel(te_ref, nv_ref, first_ref, ord_ref, nxt_ref, x_ref, wg_hbm, wu_hbm, wd_hbm, o_ref,
                wg_f32, wu_f32, wd_f32, wg_bf, wu_bf, wd_bf, sem_w, *, layer):
    i = pl.program_id(0)
    nv = nv_ref[0]

    def weight_copies(e, s):
        return [pltpu.make_async_copy(hbm.at[layer, e], buf.at[s], sem_w.at[s])
                for hbm, buf in ((wg_hbm, wg_f32), (wu_hbm, wu_f32), (wd_hbm, wd_f32))]

    @pl.when(i == 0)
    def _():
        for cp in weight_copies(te_ref[0], 0):
            cp.start()

    @pl.when((i < nv) & (first_ref[i] > 0))
    def _():
        e = te_ref[i]
        s = ord_ref[i] % 2
        for cp in weight_copies(e, s):
            cp.wait()
        wg_bf[...] = wg_f32[s].astype(BF16)
        wu_bf[...] = wu_f32[s].astype(BF16)
        wd_bf[...] = wd_f32[s].astype(BF16)
        e_next = nxt_ref[i]

        @pl.when(e_next != e)
        def _():
            for cp in weight_copies(e_next, 1 - s):
                cp.start()

    @pl.when(i < nv)
    def _():
        x = x_ref[...].astype(BF16)
        g = jnp.dot(x, wg_bf[...], preferred_element_type=F32)
        u = jnp.dot(x, wu_bf[...], preferred_element_type=F32)
        hid = (g / (1.0 + jnp.exp(-g)) * u).astype(BF16)
        o_ref[...] = jnp.dot(hid, wd_bf[...], preferred_element_type=F32)

    @pl.when(i >= nv)
    def _():
        o_ref[...] = jnp.zeros(o_ref.shape, o_ref.dtype)


def _moe_grouped(layer, tiles, xs, w_gate, w_up, w_down):
    tile_expert, n_valid, first, ordinal, next_expert = tiles
    grid_spec = pltpu.PrefetchScalarGridSpec(
        num_scalar_prefetch=5,
        grid=(N_TILES,),
        in_specs=[
            pl.BlockSpec((TM, D_MODEL), lambda i, te, nv, *_: (jnp.minimum(i, nv[0] - 1), 0)),
            pl.BlockSpec(memory_space=pl.ANY),
            pl.BlockSpec(memory_space=pl.ANY),
            pl.BlockSpec(memory_space=pl.ANY),
        ],
        out_specs=pl.BlockSpec((TM, D_MODEL), lambda i, *_: (i, 0)),
        scratch_shapes=[pltpu.VMEM((2, D_MODEL, EXPERT_FF), F32),
                        pltpu.VMEM((2, D_MODEL, EXPERT_FF), F32),
                        pltpu.VMEM((2, EXPERT_FF, D_MODEL), F32),
                        pltpu.VMEM((D_MODEL, EXPERT_FF), BF16),
                        pltpu.VMEM((D_MODEL, EXPERT_FF), BF16),
                        pltpu.VMEM((EXPERT_FF, D_MODEL), BF16),
                        pltpu.SemaphoreType.DMA((2,))],
    )
    return pl.pallas_call(
        functools.partial(_moe_kernel, layer=layer),
        grid_spec=grid_spec,
        out_shape=jax.ShapeDtypeStruct((N_SORT, D_MODEL), F32),
        compiler_params=pltpu.CompilerParams(
            dimension_semantics=("arbitrary",), vmem_limit_bytes=VMEM_LIMIT),
        name="moe_grouped",
    )(tile_expert, n_valid, first, ordinal, next_expert, xs, w_gate, w_up, w_down)


def _rank_kernel(e_ref, tri_ref, rank_ref, cnt_ref, carry):
    b = pl.program_id(0)

    @pl.when(b == 0)
    def _():
        carry[...] = jnp.zeros(carry.shape, carry.dtype)

    e = e_ref[...]
    is_e = lax.broadcasted_iota(I32, (N_EXPERTS, RANK_BLOCK), 0) == e
    onehot = jnp.where(is_e, 1.0, 0.0).astype(BF16)
    prefix = jnp.dot(onehot, tri_ref[...], preferred_element_type=F32)
    before = carry[...]
    seen = prefix + before[:, 0:1]
    rank = jnp.sum(jnp.where(is_e, seen, 0.0), axis=0, keepdims=True) - 1.0
    rank_ref[...] = rank.astype(I32)
    total = before + prefix[:, RANK_BLOCK - 1:RANK_BLOCK]
    carry[...] = total
    cnt_ref[...] = total.astype(I32)


def _rank_in_expert(e_flat):
    n = e_flat.shape[1]
    tri = jnp.triu(jnp.ones((RANK_BLOCK, RANK_BLOCK), BF16))
    rank, cnt = pl.pallas_call(
        _rank_kernel,
        grid=(n // RANK_BLOCK,),
        in_specs=[pl.BlockSpec((1, RANK_BLOCK), lambda b: (0, b)),
                  pl.BlockSpec((RANK_BLOCK, RANK_BLOCK), lambda b: (0, 0))],
        out_specs=(pl.BlockSpec((1, RANK_BLOCK), lambda b: (0, b)),
                   pl.BlockSpec((N_EXPERTS, LANES), lambda b: (0, 0))),
        out_shape=(jax.ShapeDtypeStruct((1, n), I32),
                   jax.ShapeDtypeStruct((N_EXPERTS, LANES), I32)),
        scratch_shapes=[pltpu.VMEM((N_EXPERTS, LANES), F32)],
        compiler_params=pltpu.CompilerParams(
            dimension_semantics=("arbitrary",), vmem_limit_bytes=VMEM_LIMIT),
        name="rank_in_expert",
    )(e_flat, tri)
    return rank, cnt[:, 0]


def _route_plan(ids):
    e = ids.reshape(-1)
    rank, counts = _rank_in_expert(ids.reshape(1, -1))
    padded = (counts + TM - 1) // TM * TM
    ends = jnp.cumsum(padded)
    offs = ends - padded
    is_e = e[:, None] == jnp.arange(N_EXPERTS, dtype=I32)[None, :]
    pos = jnp.sum(jnp.where(is_e, offs[None, :], 0), axis=1) + rank.reshape(-1)
    pos = pos.reshape(2, N_TOK)
    n_valid = (ends[-1] // TM).astype(I32)
    tile = jnp.arange(N_TILES, dtype=I32)
    tile_start = jnp.minimum(tile, n_valid - 1) * TM
    n_done = jnp.sum((ends[None, :] <= tile_start[:, None]).astype(I32), axis=1)
    tile_expert = jnp.minimum(n_done, N_EXPERTS - 1)
    prev_expert = jnp.concatenate([jnp.full((1,), -1, I32), tile_expert[:-1]])
    first = ((tile_expert != prev_expert) & (tile < n_valid)).astype(I32)
    ordinal = jnp.cumsum(first) - 1
    expert = jnp.arange(N_EXPERTS, dtype=I32)
    used_id = jnp.where(counts > 0, expert, N_EXPERTS)
    later = expert[None, :] > expert[:, None]
    next_used = jnp.min(jnp.where(later, used_id[None, :], N_EXPERTS), axis=1)
    next_used = jnp.where(next_used < N_EXPERTS, next_used, expert)
    next_expert = jnp.sum(jnp.where(tile_expert[:, None] == expert[None, :],
                                    next_used[None, :], 0), axis=1)
    tiles = (tile_expert, n_valid.reshape(1), first, ordinal, next_expert)
    return pos, offs + counts, padded - counts, tiles


def _moe_layer(layer, h, ids, w_gate, w_up, w_down):
    pos, pad_start, pad_len, tiles = _route_plan(ids)
    xs = _dispatch(h, pos, pad_start, pad_len, tiles[1])
    ys = _moe_grouped(layer, tiles, xs, w_gate, w_up, w_down)
    return ys, pos


def _combine_norm_kernel(p0f_ref, p1f_ref, p0n_ref, p1n_ref, x_ref, w_ref, ys_ref, g_ref, *rest,
                         emit_sum):
    out_refs, (ybuf, sem) = rest[:-2], rest[-2:]
    i = pl.program_id(0)
    n_blk = pl.num_programs(0)
    slot = i % 2
    other = 1 - slot

    def fetch(idx_refs, s):
        for k in range(2):
            for r in range(TD):
                pltpu.make_async_copy(ys_ref.at[pl.ds(idx_refs[k][0, 0, r], 1), :],
                                      ybuf.at[s, k, pl.ds(r, 1), :],
                                      sem.at[s]).start(priority=r % 2)

    @pl.when(i == 0)
    def _():
        fetch((p0f_ref, p1f_ref), 0)

    for k in range(2):
        pltpu.make_async_copy(ys_ref.at[pl.ds(0, TD), :], ybuf.at[slot, k], sem.at[slot]).wait()

    @pl.when(i + 1 < n_blk)
    def _():
        fetch((p0n_ref, p1n_ref), other)

    w = w_ref[...]
    eye = (lax.broadcasted_iota(I32, (TD, TD), 0) == lax.broadcasted_iota(I32, (TD, TD), 1))
    x_new = x_ref[...]
    for k in range(2):
        w_col = jnp.sum(jnp.where(eye, w[k:k + 1, :], 0.0), axis=1, keepdims=True)
        x_new = x_new + w_col * ybuf[slot, k]
    normed = _rms(x_new, g_ref[...], NORM_EPS)
    if emit_sum:
        out_refs[0][...] = x_new
        out_refs[1][...] = normed.astype(out_refs[1].dtype)
    else:
        out_refs[0][...] = normed.astype(out_refs[0].dtype)


def _combine_norm(x2d, ys, pos, wts, g, *, emit_sum, norm_dtype):
    n_blk = N_TOK // TD
    tok_spec = pl.BlockSpec((TD, D_MODEL), lambda i: (i, 0))
    out_specs = [tok_spec]
    out_shape = [jax.ShapeDtypeStruct((N_TOK, D_MODEL), norm_dtype)]
    if emit_sum:
        out_specs = [tok_spec, tok_spec]
        out_shape = [jax.ShapeDtypeStruct((N_TOK, D_MODEL), F32)] + out_shape
    idx_blk = lambda f: pl.BlockSpec((1, 1, TD), f, memory_space=pltpu.SMEM)
    first = lambda i: (0, 0, 0)
    nxt = lambda i: (jnp.minimum(i + 1, n_blk - 1), 0, 0)
    pos3 = [pos[k].reshape(n_blk, 1, TD) for k in range(2)]
    return pl.pallas_call(
        functools.partial(_combine_norm_kernel, emit_sum=emit_sum),
        grid=(n_blk,),
        in_specs=[
            idx_blk(first), idx_blk(first), idx_blk(nxt), idx_blk(nxt),
            tok_spec,
            pl.BlockSpec((2, TD), lambda i: (0, i)),
            pl.BlockSpec(memory_space=pl.ANY),
            pl.BlockSpec((1, D_MODEL), lambda i: (0, 0)),
        ],
        out_specs=tuple(out_specs),
        out_shape=tuple(out_shape),
        scratch_shapes=[pltpu.VMEM((2, 2, TD, D_MODEL), F32),
                        pltpu.SemaphoreType.DMA((2,))],
        compiler_params=pltpu.CompilerParams(
            dimension_semantics=("arbitrary",), vmem_limit_bytes=VMEM_LIMIT),
        name="combine_norm",
    )(pos3[0], pos3[1], pos3[0], pos3[1], x2d, wts, ys, g)


def _mm_kernel(a_ref, b_ref, o_ref, *, first_block_scale):
    acc = jnp.dot(a_ref[...], b_ref[...], preferred_element_type=F32)
    if first_block_scale is not None:
        acc = acc * jnp.where(pl.program_id(0) == 0, first_block_scale, 1.0)
    o_ref[...] = acc.astype(o_ref.dtype)


def _matmul(a, b, out_dtype, tn, first_block_scale=None):
    m, k = a.shape
    n = b.shape[1]
    return pl.pallas_call(
        functools.partial(_mm_kernel, first_block_scale=first_block_scale),
        grid=(n // tn, m // MM_TM),
        in_specs=[pl.BlockSpec((MM_TM, k), lambda j, i: (i, 0)),
                  pl.BlockSpec((k, tn), lambda j, i: (0, j))],
        out_specs=pl.BlockSpec((MM_TM, tn), lambda j, i: (i, j)),
        out_shape=jax.ShapeDtypeStruct((m, n), out_dtype),
        compiler_params=pltpu.CompilerParams(
            dimension_semantics=("arbitrary", "arbitrary"), vmem_limit_bytes=VMEM_LIMIT),
        name="dense_matmul",
    )(a, b)


def _attn_kernel(lam_ref, q_ref, k_ref, v_ref, subg_ref, o_ref, *scratch, lambda_init):
    m_ref, l_ref, acc_ref, s_ref, p_ref, a_ref = (scratch[2 * n:2 * n + 2] for n in range(6))
    lane_rep = lambda a, width: jnp.concatenate([a] * (width // LANES), axis=1)
    lv = lam_ref[...]
    lam = (jnp.exp(jnp.sum(lv[0:1] * lv[1:2], axis=-1, keepdims=True))
           - jnp.exp(jnp.sum(lv[2:3] * lv[3:4], axis=-1, keepdims=True)) + lambda_init)
    width = 2 * HEAD_DIM
    sub_g = subg_ref[...] * (1.0 - lambda_init)

    def query_tile(qi, carry):
        q0 = pl.multiple_of(qi * TQ, TQ)
        q = q_ref[pl.ds(q0, TQ), :]
        qs = (q[:, :HEAD_DIM], q[:, HEAD_DIM:])
        for c in range(2):
            m_ref[c][...] = jnp.full(m_ref[c].shape, NEG_INF, F32)
            l_ref[c][...] = jnp.zeros(l_ref[c].shape, F32)
            acc_ref[c][...] = jnp.zeros(acc_ref[c].shape, F32)

        def tile_update(j, diagonal):
            k0 = pl.multiple_of(j * TK, TK)
            kt = k_ref[pl.ds(k0, TK), :]
            vt = v_ref[pl.ds(k0, TK), :]
            for c in range(2):
                kc = kt[:, c * HEAD_DIM:(c + 1) * HEAD_DIM]
                s_ref[c][...] = lax.dot_general(qs[c], kc, (((1,), (1,)), ((), ())),
                                                preferred_element_type=F32)
            for c in range(2):
                for rb in range(TQ // CHUNK):
                    rows = pl.ds(rb * CHUNK, CHUNK)
                    ncol = TK
                    if diagonal:
                        seen = (rb + 1) * CHUNK
                        ncol = min(TK, -(-seen // LANES) * LANES)
                    s = s_ref[c][rows, :ncol]
                    if diagonal and seen < ncol:
                        col = lax.broadcasted_iota(I32, (CHUNK, ncol), 1)
                        s = jnp.where(col < seen, s, NEG_INF)
                    m_old = m_ref[c][rows, :]
                    m_new = jnp.maximum(m_old, jnp.max(s, axis=-1, keepdims=True))
                    alpha = jnp.exp2(m_old - m_new)
                    p = jnp.exp2(s - lane_rep(m_new, ncol))
                    l_ref[c][rows, :] = (alpha * l_ref[c][rows, :]
                                         + jnp.sum(p, axis=-1, keepdims=True))
                    m_ref[c][rows, :] = m_new
                    a_ref[c][rows, :] = alpha
                    p_ref[c][rows, :ncol] = p.astype(BF16)
                    if ncol < TK:
                        p_ref[c][rows, ncol:] = jnp.zeros((CHUNK, TK - ncol), BF16)
                acc_ref[c][...] = (lane_rep(a_ref[c][...], width) * acc_ref[c][...]
                                   + jnp.dot(p_ref[c][...], vt, preferred_element_type=F32))

        def full_body(j, c):
            tile_update(j, False)
            return c

        lax.fori_loop(0, qi, full_body, 0)
        tile_update(qi, True)

        for rb in range(TQ // CHUNK):
            rows = pl.ds(rb * CHUNK, CHUNK)
            o = (acc_ref[0][rows, :] * lane_rep(1.0 / l_ref[0][rows, :], width)
                 - acc_ref[1][rows, :] * lane_rep(lam * (1.0 / l_ref[1][rows, :]), width))
            o_ref[pl.ds(pl.multiple_of(q0 + rb * CHUNK, CHUNK), CHUNK), :] = _rms(
                o, sub_g, SUBLN_EPS).astype(o_ref.dtype)
        return carry

    lax.fori_loop(0, SEQ // TQ, query_tile, 0)


def _diff_attention(qkv, lam_rows, sub_g, lambda_init):
    assert TQ == TK
    head_w = 2 * HEAD_DIM
    return pl.pallas_call(
        functools.partial(_attn_kernel, lambda_init=lambda_init),
        grid=(BATCH, N_HEADS),
        in_specs=[
            pl.BlockSpec((SUBLANES, HEAD_DIM), lambda b, h: (0, 0)),
            pl.BlockSpec((SEQ, head_w), lambda b, h: (b, h)),
            pl.BlockSpec((SEQ, head_w), lambda b, h: (b, N_HEADS + h)),
            pl.BlockSpec((SEQ, head_w), lambda b, h: (b, 2 * N_HEADS + h)),
            pl.BlockSpec((1, head_w), lambda b, h: (0, 0)),
        ],
        out_specs=pl.BlockSpec((SEQ, head_w), lambda b, h: (b, h)),
        out_shape=jax.ShapeDtypeStruct((N_TOK, D_MODEL), BF16),
        scratch_shapes=(
            [pltpu.VMEM((TQ, LANES), F32)] * 2
            + [pltpu.VMEM((TQ, LANES), F32)] * 2
            + [pltpu.VMEM((TQ, head_w), F32)] * 2
            + [pltpu.VMEM((TQ, TK), F32)] * 2
            + [pltpu.VMEM((TQ, TK), BF16)] * 2
            + [pltpu.VMEM((TQ, LANES), F32)] * 2),
        compiler_params=pltpu.CompilerParams(
            dimension_semantics=("arbitrary", "arbitrary"),
            vmem_limit_bytes=VMEM_LIMIT),
        name="diff_attention",
    )(lam_rows, qkv, qkv, qkv, sub_g)


def _router_params(wr_g, br_g, wr_e, br_e):
    w = jnp.zeros((D_MODEL, ROUTER_COLS), F32)
    w = w.at[:, 0:N_GROUPS].set(wr_g).at[:, E_COL0:E_COL0 + N_EXPERTS].set(wr_e)
    b = jnp.full((1, ROUTER_COLS), NEG_INF, F32)
    b = b.at[0, 0:N_GROUPS].set(br_g).at[0, E_COL0:E_COL0 + N_EXPERTS].set(br_e)
    w_hi = w.astype(BF16)
    w_lo = (w - w_hi.astype(F32)).astype(BF16)
    return jnp.concatenate([w_hi, w_lo], axis=1), b


def kernel(x, norm_mix_g, norm_ffn_g, pool_w, pool_scale, attn_wq, attn_wk, attn_wv, attn_wo,
           lam_q1, lam_k1, lam_q2, lam_k2, attn_sub_g, router_g_w, router_g_b, router_e_w,
           router_e_b, exp_w_gate, exp_w_up, exp_w_down, final_g):
    x2d = x.reshape(N_TOK, D_MODEL)
    row = lambda v: v.reshape(1, -1)

    wr0, br0 = _router_params(router_g_w[0], router_g_b[0], router_e_w[0], router_e_b[0])
    x1, h, ids, wts = _pool_layer(x2d, row(norm_mix_g[0]), pool_w[0].astype(BF16),
                                  row(pool_scale[0]), row(norm_ffn_g[0]), wr0, br0)
    ys, pos = _moe_layer(0, h, ids, exp_w_gate, exp_w_up, exp_w_down)
    x2, hn = _combine_norm(x1, ys, pos, wts, row(norm_mix_g[1]), emit_sum=True, norm_dtype=BF16)

    lambda_init = 0.8 - 0.6 * math.exp(-0.3 * 1)
    w_qkv = jnp.concatenate([attn_wq[0], attn_wk[0], attn_wv[0]], axis=1).astype(BF16)
    q_scale = HEAD_DIM ** -0.5 * math.log2(math.e)
    qkv = _matmul(hn, w_qkv, BF16, D_MODEL, first_block_scale=q_scale)
    lam_rows = jnp.zeros((SUBLANES, HEAD_DIM), F32)
    lam_rows = lam_rows.at[0].set(lam_q1[0]).at[1].set(lam_k1[0])
    lam_rows = lam_rows.at[2].set(lam_q2[0]).at[3].set(lam_k2[0])
    att = _diff_attention(qkv, lam_rows, row(attn_sub_g[0]), lambda_init)
    wr1, br1 = _router_params(router_g_w[1], router_g_b[1], router_e_w[1], router_e_b[1])
    x3, h, ids, wts = _proj_router(x2, att, attn_wo[0].astype(BF16), row(norm_ffn_g[1]),
                                   wr1, br1)
    ys, pos = _moe_layer(1, h, ids, exp_w_gate, exp_w_up, exp_w_down)
    (out,) = _combine_norm(x3, ys, pos, wts, row(final_g), emit_sum=False, norm_dtype=F32)
    return out.reshape(BATCH, SEQ, D_MODEL)
```

```python
import functools
import math

import jax
import jax.numpy as jnp
from jax import lax
from jax.experimental import pallas as pl
from jax.experimental.pallas import tpu as pltpu

F32 = jnp.float32
BF16 = jnp.bfloat16
I32 = jnp.int32

D_MODEL = 2048
BATCH = 8
SEQ = 2048
N_TOK = BATCH * SEQ
CHUNK = 64
POOL_WINDOWS = (2, 4, 8, 16)
POOL_CG = D_MODEL // len(POOL_WINDOWS)
POOL_HALO = 16
N_HEADS = 8
HEAD_DIM = 128
N_GROUPS = 4
EXPERTS_PER_GROUP = 8
N_EXPERTS = 32
EXPERT_FF = 512
NORM_EPS = 1e-6
SUBLN_EPS = 1e-5
NEG_INF = -1e30

SUBLANES = 8
LANES = 128
RPT = D_MODEL // LANES
RPT_PAD = RPT + 1
MXU_K = 256
ROUTER_COLS = LANES
E_COL0 = 8

TS = 256
TD = 512
TM = 256
N_SORT = 2 * N_TOK + N_EXPERTS * TM
N_TILES = N_SORT // TM
TQ = 512
TK = 512
MM_TM = 512
RANK_BLOCK = 2048
VMEM_LIMIT = 56 * 1024 * 1024


def _rms(x, g, eps):
    ms = jnp.mean(x * x, axis=-1, keepdims=True)
    return x * lax.rsqrt(ms + eps) * g


def _to_rows(ref, val, n, pitch=RPT):
    for c in range(RPT):
        ref[pl.ds(c, n, stride=pitch), :] = val[:, c * LANES:(c + 1) * LANES]


def _from_rows(ref, n, lo=0, hi=RPT, pitch=RPT):
    return jnp.concatenate(
        [ref[pl.ds(c, n, stride=pitch), :] for c in range(lo, hi)], axis=-1)


def _router_tail(x_new, gffn_ref, wr_ref, br_ref, h_ref, ids_ref, wts_ref):
    h = _rms(x_new, gffn_ref[...], NORM_EPS)
    h_ref[...] = h
    h_hi = h.astype(BF16)
    h_lo = (h - h_hi.astype(F32)).astype(BF16)
    hi_terms = jnp.dot(h_hi, wr_ref[...], preferred_element_type=F32)
    logits = (hi_terms[:, :ROUTER_COLS] + hi_terms[:, ROUTER_COLS:]
              + jnp.dot(h_lo, wr_ref[:, :ROUTER_COLS], preferred_element_type=F32))
    logits = logits + br_ref[...]
    lt = logits.T
    gl = lt[0:SUBLANES, :]
    io8 = lax.broadcasted_iota(I32, gl.shape, 0)
    gmax = jnp.max(gl, axis=0, keepdims=True)
    gtop = jnp.min(jnp.where(gl == gmax, io8, SUBLANES), axis=0, keepdims=True)
    p_g = 1.0 / jnp.sum(jnp.exp(gl - gmax), axis=0, keepdims=True)
    esel = lt[E_COL0:E_COL0 + EXPERTS_PER_GROUP, :]
    for gi in range(1, N_GROUPS):
        lo = E_COL0 + gi * EXPERTS_PER_GROUP
        esel = jnp.where(gtop == gi, lt[lo:lo + EXPERTS_PER_GROUP, :], esel)
    m1 = jnp.max(esel, axis=0, keepdims=True)
    i1 = jnp.min(jnp.where(esel == m1, io8, SUBLANES), axis=0, keepdims=True)
    rest = jnp.where(io8 == i1, -jnp.inf, esel)
    m2 = jnp.max(rest, axis=0, keepdims=True)
    i2 = jnp.min(jnp.where(rest == m2, io8, SUBLANES), axis=0, keepdims=True)
    r = jnp.exp(m2 - m1)
    v1 = 1.0 / (1.0 + r)
    v2 = r / (1.0 + r)
    base = gtop * EXPERTS_PER_GROUP
    ids_ref[0:1, :] = base + i1
    ids_ref[1:2, :] = base + i2
    wts_ref[0:1, :] = p_g * v1
    wts_ref[1:2, :] = p_g * v2


def _router_out_shapes():
    return (jax.ShapeDtypeStruct((N_TOK, D_MODEL), F32),
            jax.ShapeDtypeStruct((2, N_TOK), I32),
            jax.ShapeDtypeStruct((2, N_TOK), F32))


def _router_out_specs(tok_map):
    return (pl.BlockSpec((TS, D_MODEL), lambda *a: (tok_map(*a), 0)),
            pl.BlockSpec((2, TS), lambda *a: (0, tok_map(*a))),
            pl.BlockSpec((2, TS), lambda *a: (0, tok_map(*a))))


def _pool_kernel(x_ref, halo_ref, gmix_ref, wpool_ref, pscale_ref, gffn_ref, wr_ref, br_ref,
                 x1_ref, h_ref, ids_ref, wts_ref, hs_ref):
    s = pl.program_id(1)
    xt = x_ref[...]
    g = gmix_ref[...]
    h = _rms(xt, g, NORM_EPS)
    hh = _rms(halo_ref[...], g, NORM_EPS)
    hs_ref[0:POOL_HALO, :] = jnp.where(s > 0, hh, 0.0)
    hs_ref[POOL_HALO:, :] = h
    pos = s * TS + lax.broadcasted_iota(I32, (TS, 1), 0)
    outs = []
    for gi, w in enumerate(POOL_WINDOWS):
        c0 = gi * POOL_CG
        he = hs_ref[:, c0:c0 + POOL_CG]
        acc = he
        sh = 1
        while sh < w:
            acc = acc + pltpu.roll(acc, sh, axis=0)
            sh *= 2
        inv_cnt = 1.0 / jnp.minimum(pos + 1, w).astype(F32)
        mix = acc[POOL_HALO:, :] * inv_cnt - he[POOL_HALO:, :]
        outs.append(jnp.dot(mix.astype(BF16), wpool_ref[gi], preferred_element_type=F32))
    x1 = xt + jnp.concatenate(outs, axis=-1) * pscale_ref[...]
    x1_ref[...] = x1
    _router_tail(x1, gffn_ref, wr_ref, br_ref, h_ref, ids_ref, wts_ref)


def _pool_layer(x2d, gmix, wpool_bf, pscale, gffn, wr2, br):
    n_s = SEQ // TS
    tok_map = lambda b, s: b * n_s + s
    halo_blocks = TS // POOL_HALO
    const2 = lambda b, s: (0, 0)
    return pl.pallas_call(
        _pool_kernel,
        grid=(BATCH, n_s),
        in_specs=[
            pl.BlockSpec((TS, D_MODEL), lambda b, s: (tok_map(b, s), 0)),
            pl.BlockSpec((POOL_HALO, D_MODEL),
                         lambda b, s: (jnp.maximum(tok_map(b, s) * halo_blocks - 1, 0), 0)),
            pl.BlockSpec((1, D_MODEL), const2),
            pl.BlockSpec((len(POOL_WINDOWS), POOL_CG, POOL_CG), lambda b, s: (0, 0, 0)),
            pl.BlockSpec((1, D_MODEL), const2),
            pl.BlockSpec((1, D_MODEL), const2),
            pl.BlockSpec((D_MODEL, 2 * ROUTER_COLS), const2),
            pl.BlockSpec((1, ROUTER_COLS), const2),
        ],
        out_specs=(pl.BlockSpec((TS, D_MODEL), lambda b, s: (tok_map(b, s), 0)),)
        + _router_out_specs(tok_map),
        out_shape=(jax.ShapeDtypeStruct((N_TOK, D_MODEL), F32),) + _router_out_shapes(),
        scratch_shapes=[pltpu.VMEM((TS + POOL_HALO, D_MODEL), F32)],
        compiler_params=pltpu.CompilerParams(
            dimension_semantics=("arbitrary", "arbitrary"), vmem_limit_bytes=VMEM_LIMIT),
        name="pool_mixer_router",
    )(x2d, x2d, gmix, wpool_bf, pscale, gffn, wr2, br)


def _proj_router_kernel(x_ref, a_ref, wo_ref, gffn_ref, wr_ref, br_ref,
                        xo_ref, h_ref, ids_ref, wts_ref):
    x_new = x_ref[...] + jnp.dot(a_ref[...], wo_ref[...], preferred_element_type=F32)
    xo_ref[...] = x_new
    _router_tail(x_new, gffn_ref, wr_ref, br_ref, h_ref, ids_ref, wts_ref)


def _proj_router(x2d, att, wo_bf, gffn, wr2, br):
    tok_map = lambda i: i
    const2 = lambda i: (0, 0)
    return pl.pallas_call(
        _proj_router_kernel,
        grid=(N_TOK // TS,),
        in_specs=[
            pl.BlockSpec((TS, D_MODEL), lambda i: (i, 0)),
            pl.BlockSpec((TS, D_MODEL), lambda i: (i, 0)),
            pl.BlockSpec((D_MODEL, D_MODEL), const2),
            pl.BlockSpec((1, D_MODEL), const2),
            pl.BlockSpec((D_MODEL, 2 * ROUTER_COLS), const2),
            pl.BlockSpec((1, ROUTER_COLS), const2),
        ],
        out_specs=(pl.BlockSpec((TS, D_MODEL), lambda i: (i, 0)),) + _router_out_specs(tok_map),
        out_shape=(jax.ShapeDtypeStruct((N_TOK, D_MODEL), F32),) + _router_out_shapes(),
        compiler_params=pltpu.CompilerParams(
            dimension_semantics=("arbitrary",), vmem_limit_bytes=VMEM_LIMIT),
        name="proj_router",
    )(x2d, att, wo_bf, gffn, wr2, br)


def _dispatch_kernel(pad0_ref, padn_ref, nv_ref, p0_ref, p1_ref, h_ref, xs_ref,
                     hbuf, zrow, ztile, sem, sem_z):
    i = pl.program_id(0)
    n_blk = pl.num_programs(0)
    slot = i % 2

    def zero_row(p):
        return pltpu.make_async_copy(zrow.at[pl.ds(0, 1), :], xs_ref.at[pl.ds(p, 1), :], sem_z)

    def zero_rows8(p):
        return pltpu.make_async_copy(
            zrow, xs_ref.at[pl.ds(pl.multiple_of(p, SUBLANES), SUBLANES), :], sem_z)

    def zero_tile(t):
        return pltpu.make_async_copy(
            ztile, xs_ref.at[pl.ds(pl.multiple_of(t * TM, TM), TM), :], sem_z)

    def for_each_unowned(row_fn, rows8_fn, tile_fn):
        def per_expert(e, c):
            start = pad0_ref[e]
            n_single = (-start) % SUBLANES
            n_rows8 = (padn_ref[e] - n_single) // SUBLANES

            def per_row(j, c2):
                row_fn(start + j)
                return c2

            def per_rows8(j, c2):
                rows8_fn(start + n_single + j * SUBLANES)
                return c2
            lax.fori_loop(0, n_single, per_row, c)
            return lax.fori_loop(0, n_rows8, per_rows8, c)
        lax.fori_loop(0, N_EXPERTS, per_expert, 0)

        def per_tile(t, c):
            tile_fn(t)
            return c
        lax.fori_loop(nv_ref[0], N_TILES, per_tile, 0)

    @pl.when(i == 0)
    def _():
        zrow[...] = jnp.zeros(zrow.shape, zrow.dtype)
        ztile[...] = jnp.zeros(ztile.shape, ztile.dtype)
        for_each_unowned(lambda p: zero_row(p).start(), lambda p: zero_rows8(p).start(),
                         lambda t: zero_tile(t).start())

    def send_wait(s):
        for _ in range(2):
            pltpu.make_async_copy(hbuf.at[s], xs_ref.at[pl.ds(0, TD), :], sem.at[s]).wait()

    @pl.when(i >= 2)
    def _():
        send_wait(slot)

    hbuf[slot] = h_ref[...]
    for k, p_ref in enumerate((p0_ref, p1_ref)):
        for r in range(TD):
            pltpu.make_async_copy(hbuf.at[slot, pl.ds(r, 1), :],
                                  xs_ref.at[pl.ds(p_ref[0, 0, r], 1), :],
                                  sem.at[slot]).start(priority=(r + k) % 2)

    @pl.when(i == n_blk - 1)
    def _():
        send_wait(1 - slot)
        send_wait(slot)
        for_each_unowned(lambda p: zero_row(p).wait(), lambda p: zero_rows8(p).wait(),
                         lambda t: zero_tile(t).wait())


def _dispatch(h, pos, pad_start, pad_len, n_valid):
    n_blk = N_TOK // TD
    idx_blk = lambda: pl.BlockSpec((1, 1, TD), lambda i, *_: (i, 0, 0), memory_space=pltpu.SMEM)
    pos3 = [pos[k].reshape(n_blk, 1, TD) for k in range(2)]
    grid_spec = pltpu.PrefetchScalarGridSpec(
        num_scalar_prefetch=3,
        grid=(n_blk,),
        in_specs=[idx_blk(), idx_blk(),
                  pl.BlockSpec((TD, D_MODEL), lambda i, *_: (i, 0))],
        out_specs=pl.BlockSpec(memory_space=pl.ANY),
        scratch_shapes=[pltpu.VMEM((2, TD, D_MODEL), F32),
                        pltpu.VMEM((SUBLANES, D_MODEL), F32),
                        pltpu.VMEM((TM, D_MODEL), F32),
                        pltpu.SemaphoreType.DMA((2,)),
                        pltpu.SemaphoreType.DMA(())],
    )
    return pl.pallas_call(
        _dispatch_kernel,
        grid_spec=grid_spec,
        out_shape=jax.ShapeDtypeStruct((N_SORT, D_MODEL), F32),
        compiler_params=pltpu.CompilerParams(
            dimension_semantics=("arbitrary",), vmem_limit_bytes=VMEM_LIMIT),
        name="dispatch_rows",
    )(pad_start, pad_len, n_valid, pos3[0], pos3[1], h)


def _moe_kernel(te_ref, nv_ref, first_ref, ord_ref, nxt_ref, x_ref, wg_hbm, wu_hbm, wd_hbm, o_ref,
                wg_f32, wu_f32, wd_f32, wg_bf, wu_bf, wd_bf, sem_w, *, layer):
    i = pl.program_id(0)
    nv = nv_ref[0]

    def weight_copies(e, s):
        return [pltpu.make_async_copy(hbm.at[layer, e], buf.at[s], sem_w.at[s])
                for hbm, buf in ((wg_hbm, wg_f32), (wu_hbm, wu_f32), (wd_hbm, wd_f32))]

    @pl.when(i == 0)
    def _():
        for cp in weight_copies(te_ref[0], 0):
            cp.start()

    @pl.when((i < nv) & (first_ref[i] > 0))
    def _():
        e = te_ref[i]
        s = ord_ref[i] % 2
        for cp in weight_copies(e, s):
            cp.wait()
        wg_bf[...] = wg_f32[s].astype(BF16)
        wu_bf[...] = wu_f32[s].astype(BF16)
        wd_bf[...] = wd_f32[s].astype(BF16)
        e_next = nxt_ref[i]

        @pl.when(e_next != e)
        def _():
            for cp in weight_copies(e_next, 1 - s):
                cp.start()

    @pl.when(i < nv)
    def _():
        x = x_ref[...].astype(BF16)
        g = jnp.dot(x, wg_bf[...], preferred_element_type=F32)
        u = jnp.dot(x, wu_bf[...], preferred_element_type=F32)
        hid = (g / (1.0 + jnp.exp(-g)) * u).astype(BF16)
        o_ref[...] = jnp.dot(hid, wd_bf[...], preferred_element_type=F32)

    @pl.when(i >= nv)
    def _():
        o_ref[...] = jnp.zeros(o_ref.shape, o_ref.dtype)


def _moe_grouped(layer, tiles, xs, w_gate, w_up, w_down):
    tile_expert, n_valid, first, ordinal, next_expert = tiles
    grid_spec = pltpu.PrefetchScalarGridSpec(
        num_scalar_prefetch=5,
        grid=(N_TILES,),
        in_specs=[
            pl.BlockSpec((TM, D_MODEL), lambda i, te, nv, *_: (jnp.minimum(i, nv[0] - 1), 0)),
            pl.BlockSpec(memory_space=pl.ANY),
            pl.BlockSpec(memory_space=pl.ANY),
            pl.BlockSpec(memory_space=pl.ANY),
        ],
        out_specs=pl.BlockSpec((TM, D_MODEL), lambda i, *_: (i, 0)),
        scratch_shapes=[pltpu.VMEM((2, D_MODEL, EXPERT_FF), F32),
                        pltpu.VMEM((2, D_MODEL, EXPERT_FF), F32),
                        pltpu.VMEM((2, EXPERT_FF, D_MODEL), F32),
                        pltpu.VMEM((D_MODEL, EXPERT_FF), BF16),
                        pltpu.VMEM((D_MODEL, EXPERT_FF), BF16),
                        pltpu.VMEM((EXPERT_FF, D_MODEL), BF16),
                        pltpu.SemaphoreType.DMA((2,))],
    )
    return pl.pallas_call(
        functools.partial(_moe_kernel, layer=layer),
        grid_spec=grid_spec,
        out_shape=jax.ShapeDtypeStruct((N_SORT, D_MODEL), F32),
        compiler_params=pltpu.CompilerParams(
            dimension_semantics=("arbitrary",), vmem_limit_bytes=VMEM_LIMIT),
        name="moe_grouped",
    )(tile_expert, n_valid, first, ordinal, next_expert, xs, w_gate, w_up, w_down)


def _rank_kernel(e_ref, tri_ref, rank_ref, cnt_ref, carry):
    b = pl.program_id(0)

    @pl.when(b == 0)
    def _():
        carry[...] = jnp.zeros(carry.shape, carry.dtype)

    e = e_ref[...]
    is_e = lax.broadcasted_iota(I32, (N_EXPERTS, RANK_BLOCK), 0) == e
    onehot = jnp.where(is_e, 1.0, 0.0).astype(BF16)
    prefix = jnp.dot(onehot, tri_ref[...], preferred_element_type=F32)
    before = carry[...]
    seen = prefix + before[:, 0:1]
    rank = jnp.sum(jnp.where(is_e, seen, 0.0), axis=0, keepdims=True) - 1.0
    rank_ref[...] = rank.astype(I32)
    total = before + prefix[:, RANK_BLOCK - 1:RANK_BLOCK]
    carry[...] = total
    cnt_ref[...] = total.astype(I32)


def _rank_in_expert(e_flat):
    n = e_flat.shape[1]
    tri = jnp.triu(jnp.ones((RANK_BLOCK, RANK_BLOCK), BF16))
    rank, cnt = pl.pallas_call(
        _rank_kernel,
        grid=(n // RANK_BLOCK,),
        in_specs=[pl.BlockSpec((1, RANK_BLOCK), lambda b: (0, b)),
                  pl.BlockSpec((RANK_BLOCK, RANK_BLOCK), lambda b: (0, 0))],
        out_specs=(pl.BlockSpec((1, RANK_BLOCK), lambda b: (0, b)),
                   pl.BlockSpec((N_EXPERTS, LANES), lambda b: (0, 0))),
        out_shape=(jax.ShapeDtypeStruct((1, n), I32),
                   jax.ShapeDtypeStruct((N_EXPERTS, LANES), I32)),
        scratch_shapes=[pltpu.VMEM((N_EXPERTS, LANES), F32)],
        compiler_params=pltpu.CompilerParams(
            dimension_semantics=("arbitrary",), vmem_limit_bytes=VMEM_LIMIT),
        name="rank_in_expert",
    )(e_flat, tri)
    return rank, cnt[:, 0]


def _route_plan(ids):
    e = ids.reshape(-1)
    rank, counts = _rank_in_expert(ids.reshape(1, -1))
    padded = (counts + TM - 1) // TM * TM
    ends = jnp.cumsum(padded)
    offs = ends - padded
    is_e = e[:, None] == jnp.arange(N_EXPERTS, dtype=I32)[None, :]
    pos = jnp.sum(jnp.where(is_e, offs[None, :], 0), axis=1) + rank.reshape(-1)
    pos = pos.reshape(2, N_TOK)
    n_valid = (ends[-1] // TM).astype(I32)
    tile = jnp.arange(N_TILES, dtype=I32)
    tile_start = jnp.minimum(tile, n_valid - 1) * TM
    n_done = jnp.sum((ends[None, :] <= tile_start[:, None]).astype(I32), axis=1)
    tile_expert = jnp.minimum(n_done, N_EXPERTS - 1)
    prev_expert = jnp.concatenate([jnp.full((1,), -1, I32), tile_expert[:-1]])
    first = ((tile_expert != prev_expert) & (tile < n_valid)).astype(I32)
    ordinal = jnp.cumsum(first) - 1
    expert = jnp.arange(N_EXPERTS, dtype=I32)
    used_id = jnp.where(counts > 0, expert, N_EXPERTS)
    later = expert[None, :] > expert[:, None]
    next_used = jnp.min(jnp.where(later, used_id[None, :], N_EXPERTS), axis=1)
    next_used = jnp.where(next_used < N_EXPERTS, next_used, expert)
    next_expert = jnp.sum(jnp.where(tile_expert[:, None] == expert[None, :],
                                    next_used[None, :], 0), axis=1)
    tiles = (tile_expert, n_valid.reshape(1), first, ordinal, next_expert)
    return pos, offs + counts, padded - counts, tiles


def _moe_layer(layer, h, ids, w_gate, w_up, w_down):
    pos, pad_start, pad_len, tiles = _route_plan(ids)
    xs = _dispatch(h, pos, pad_start, pad_len, tiles[1])
    ys = _moe_grouped(layer, tiles, xs, w_gate, w_up, w_down)
    return ys, pos


def _combine_norm_kernel(p0f_ref, p1f_ref, p0n_ref, p1n_ref, x_ref, w_ref, ys_ref, g_ref, *rest,
                         emit_sum):
    out_refs, (ybuf, sem) = rest[:-2], rest[-2:]
    i = pl.program_id(0)
    n_blk = pl.num_programs(0)
    slot = i % 2
    other = 1 - slot

    def fetch(idx_refs, s):
        for k in range(2):
            for r in range(TD):
                pltpu.make_async_copy(ys_ref.at[pl.ds(idx_refs[k][0, 0, r], 1), :],
                                      ybuf.at[s, k, pl.ds(r, 1), :],
                                      sem.at[s]).start(priority=r % 2)

    @pl.when(i == 0)
    def _():
        fetch((p0f_ref, p1f_ref), 0)

    for k in range(2):
        pltpu.make_async_copy(ys_ref.at[pl.ds(0, TD), :], ybuf.at[slot, k], sem.at[slot]).wait()

    @pl.when(i + 1 < n_blk)
    def _():
        fetch((p0n_ref, p1n_ref), other)

    w = w_ref[...]
    eye = (lax.broadcasted_iota(I32, (TD, TD), 0) == lax.broadcasted_iota(I32, (TD, TD), 1))
    x_new = x_ref[...]
    for k in range(2):
        w_col = jnp.sum(jnp.where(eye, w[k:k + 1, :], 0.0), axis=1, keepdims=True)
        x_new = x_new + w_col * ybuf[slot, k]
    normed = _rms(x_new, g_ref[...], NORM_EPS)
    if emit_sum:
        out_refs[0][...] = x_new
        out_refs[1][...] = normed.astype(out_refs[1].dtype)
    else:
        out_refs[0][...] = normed.astype(out_refs[0].dtype)


def _combine_norm(x2d, ys, pos, wts, g, *, emit_sum, norm_dtype):
    n_blk = N_TOK // TD
    tok_spec = pl.BlockSpec((TD, D_MODEL), lambda i: (i, 0))
    out_specs = [tok_spec]
    out_shape = [jax.ShapeDtypeStruct((N_TOK, D_MODEL), norm_dtype)]
    if emit_sum:
        out_specs = [tok_spec, tok_spec]
        out_shape = [jax.ShapeDtypeStruct((N_TOK, D_MODEL), F32)] + out_shape
    idx_blk = lambda f: pl.BlockSpec((1, 1, TD), f, memory_space=pltpu.SMEM)
    first = lambda i: (0, 0, 0)
    nxt = lambda i: (jnp.minimum(i + 1, n_blk - 1), 0, 0)
    pos3 = [pos[k].reshape(n_blk, 1, TD) for k in range(2)]
    return pl.pallas_call(
        functools.partial(_combine_norm_kernel, emit_sum=emit_sum),
        grid=(n_blk,),
        in_specs=[
            idx_blk(first), idx_blk(first), idx_blk(nxt), idx_blk(nxt),
            tok_spec,
            pl.BlockSpec((2, TD), lambda i: (0, i)),
            pl.BlockSpec(memory_space=pl.ANY),
            pl.BlockSpec((1, D_MODEL), lambda i: (0, 0)),
        ],
        out_specs=tuple(out_specs),
        out_shape=tuple(out_shape),
        scratch_shapes=[pltpu.VMEM((2, 2, TD, D_MODEL), F32),
                        pltpu.SemaphoreType.DMA((2,))],
        compiler_params=pltpu.CompilerParams(
            dimension_semantics=("arbitrary",), vmem_limit_bytes=VMEM_LIMIT),
        name="combine_norm",
    )(pos3[0], pos3[1], pos3[0], pos3[1], x2d, wts, ys, g)


def _mm_kernel(a_ref, b_ref, o_ref, *, first_block_scale):
    acc = jnp.dot(a_ref[...], b_ref[...], preferred_element_type=F32)
    if first_block_scale is not None:
        acc = acc * jnp.where(pl.program_id(0) == 0, first_block_scale, 1.0)
    o_ref[...] = acc.astype(o_ref.dtype)


def _matmul(a, b, out_dtype, tn, first_block_scale=None):
    m, k = a.shape
    n = b.shape[1]
    return pl.pallas_call(
        functools.partial(_mm_kernel, first_block_scale=first_block_scale),
        grid=(n // tn, m // MM_TM),
        in_specs=[pl.BlockSpec((MM_TM, k), lambda j, i: (i, 0)),
                  pl.BlockSpec((k, tn), lambda j, i: (0, j))],
        out_specs=pl.BlockSpec((MM_TM, tn), lambda j, i: (i, j)),
        out_shape=jax.ShapeDtypeStruct((m, n), out_dtype),
        compiler_params=pltpu.CompilerParams(
            dimension_semantics=("arbitrary", "arbitrary"), vmem_limit_bytes=VMEM_LIMIT),
        name="dense_matmul",
    )(a, b)


def _attn_kernel(lam_ref, q_ref, k_ref, v_ref, subg_ref, o_ref, *scratch, lambda_init):
    m_ref, l_ref, acc_ref, s_ref, p_ref, a_ref = (scratch[2 * n:2 * n + 2] for n in range(6))
    lane_rep = lambda a, width: jnp.concatenate([a] * (width // LANES), axis=1)
    lv = lam_ref[...]
    lam = (jnp.exp(jnp.sum(lv[0:1] * lv[1:2], axis=-1, keepdims=True))
           - jnp.exp(jnp.sum(lv[2:3] * lv[3:4], axis=-1, keepdims=True)) + lambda_init)
    width = 2 * HEAD_DIM
    sub_g = subg_ref[...] * (1.0 - lambda_init)

    def query_tile(qi, carry):
        q0 = pl.multiple_of(qi * TQ, TQ)
        q = q_ref[pl.ds(q0, TQ), :]
        qs = (q[:, :HEAD_DIM], q[:, HEAD_DIM:])

        def tile_update(j, diagonal, first):
            k0 = pl.multiple_of(j * TK, TK)
            kt = k_ref[pl.ds(k0, TK), :]
            vt = v_ref[pl.ds(k0, TK), :]
            for c in range(2):
                kc = kt[:, c * HEAD_DIM:(c + 1) * HEAD_DIM]
                s_ref[c][...] = lax.dot_general(qs[c], kc, (((1,), (1,)), ((), ())),
                                                preferred_element_type=F32)
            for c in range(2):
                for rb in range(TQ // CHUNK):
                    rows = pl.ds(rb * CHUNK, CHUNK)
                    ncol = TK
                    if diagonal:
                        seen = (rb + 1) * CHUNK
                        ncol = min(TK, -(-seen // LANES) * LANES)
                    s = s_ref[c][rows, :ncol]
                    if diagonal and seen < ncol:
                        col = lax.broadcasted_iota(I32, (CHUNK, ncol), 1)
                        s = jnp.where(col < seen, s, NEG_INF)
                    s_max = jnp.max(s, axis=-1, keepdims=True)
                    if first:
                        m_new = jnp.zeros((CHUNK, LANES), F32) + s_max
                    else:
                        m_old = m_ref[c][rows, :]
                        m_new = jnp.maximum(m_old, s_max)
                        alpha = jnp.exp2(m_old - m_new)
                        a_ref[c][rows, :] = alpha
                    p = jnp.exp2(s - lane_rep(m_new, ncol))
                    p_sum = jnp.sum(p, axis=-1, keepdims=True)
                    if first:
                        l_ref[c][rows, :] = jnp.zeros((CHUNK, LANES), F32) + p_sum
                    else:
                        l_ref[c][rows, :] = alpha * l_ref[c][rows, :] + p_sum
                    m_ref[c][rows, :] = m_new
                    p_ref[c][rows, :ncol] = p.astype(BF16)
                    if ncol < TK:
                        p_ref[c][rows, ncol:] = jnp.zeros((CHUNK, TK - ncol), BF16)
                pv = jnp.dot(p_ref[c][...], vt, preferred_element_type=F32)
                if first:
                    acc_ref[c][...] = pv
                else:
                    acc_ref[c][...] = lane_rep(a_ref[c][...], width) * acc_ref[c][...] + pv

        def full_body(j, c):
            tile_update(j, False, False)
            return c

        @pl.when(qi == 0)
        def _():
            tile_update(0, True, True)

        @pl.when(qi > 0)
        def _():
            tile_update(0, False, True)
            lax.fori_loop(1, qi, full_body, 0)
            tile_update(qi, True, False)

        for rb in range(TQ // CHUNK):
            rows = pl.ds(rb * CHUNK, CHUNK)
            o = (acc_ref[0][rows, :] * lane_rep(1.0 / l_ref[0][rows, :], width)
                 - acc_ref[1][rows, :] * lane_rep(lam * (1.0 / l_ref[1][rows, :]), width))
            o_ref[pl.ds(pl.multiple_of(q0 + rb * CHUNK, CHUNK), CHUNK), :] = _rms(
                o, sub_g, SUBLN_EPS).astype(o_ref.dtype)
        return carry

    lax.fori_loop(0, SEQ // TQ, query_tile, 0)


def _diff_attention(qkv, lam_rows, sub_g, lambda_init):
    assert TQ == TK
    head_w = 2 * HEAD_DIM
    return pl.pallas_call(
        functools.partial(_attn_kernel, lambda_init=lambda_init),
        grid=(BATCH, N_HEADS),
        in_specs=[
            pl.BlockSpec((SUBLANES, HEAD_DIM), lambda b, h: (0, 0)),
            pl.BlockSpec((SEQ, head_w), lambda b, h: (b, h)),
            pl.BlockSpec((SEQ, head_w), lambda b, h: (b, N_HEADS + h)),
            pl.BlockSpec((SEQ, head_w), lambda b, h: (b, 2 * N_HEADS + h)),
            pl.BlockSpec((1, head_w), lambda b, h: (0, 0)),
        ],
        out_specs=pl.BlockSpec((SEQ, head_w), lambda b, h: (b, h)),
        out_shape=jax.ShapeDtypeStruct((N_TOK, D_MODEL), BF16),
        scratch_shapes=(
            [pltpu.VMEM((TQ, LANES), F32)] * 2
            + [pltpu.VMEM((TQ, LANES), F32)] * 2
            + [pltpu.VMEM((TQ, head_w), F32)] * 2
            + [pltpu.VMEM((TQ, TK), F32)] * 2
            + [pltpu.VMEM((TQ, TK), BF16)] * 2
            + [pltpu.VMEM((TQ, LANES), F32)] * 2),
        compiler_params=pltpu.CompilerParams(
            dimension_semantics=("arbitrary", "arbitrary"),
            vmem_limit_bytes=VMEM_LIMIT),
        name="diff_attention",
    )(lam_rows, qkv, qkv, qkv, sub_g)


def _router_params(wr_g, br_g, wr_e, br_e):
    w = jnp.zeros((D_MODEL, ROUTER_COLS), F32)
    w = w.at[:, 0:N_GROUPS].set(wr_g).at[:, E_COL0:E_COL0 + N_EXPERTS].set(wr_e)
    b = jnp.full((1, ROUTER_COLS), NEG_INF, F32)
    b = b.at[0, 0:N_GROUPS].set(br_g).at[0, E_COL0:E_COL0 + N_EXPERTS].set(br_e)
    w_hi = w.astype(BF16)
    w_lo = (w - w_hi.astype(F32)).astype(BF16)
    return jnp.concatenate([w_hi, w_lo], axis=1), b


def kernel(x, norm_mix_g, norm_ffn_g, pool_w, pool_scale, attn_wq, attn_wk, attn_wv, attn_wo,
           lam_q1, lam_k1, lam_q2, lam_k2, attn_sub_g, router_g_w, router_g_b, router_e_w,
           router_e_b, exp_w_gate, exp_w_up, exp_w_down, final_g):
    x2d = x.reshape(N_TOK, D_MODEL)
    row = lambda v: v.reshape(1, -1)

    wr0, br0 = _router_params(router_g_w[0], router_g_b[0], router_e_w[0], router_e_b[0])
    x1, h, ids, wts = _pool_layer(x2d, row(norm_mix_g[0]), pool_w[0].astype(BF16),
                                  row(pool_scale[0]), row(norm_ffn_g[0]), wr0, br0)
    ys, pos = _moe_layer(0, h, ids, exp_w_gate, exp_w_up, exp_w_down)
    x2, hn = _combine_norm(x1, ys, pos, wts, row(norm_mix_g[1]), emit_sum=True, norm_dtype=BF16)

    lambda_init = 0.8 - 0.6 * math.exp(-0.3 * 1)
    w_qkv = jnp.concatenate([attn_wq[0], attn_wk[0], attn_wv[0]], axis=1).astype(BF16)
    q_scale = HEAD_DIM ** -0.5 * math.log2(math.e)
    qkv = _matmul(hn, w_qkv, BF16, D_MODEL, first_block_scale=q_scale)
    lam_rows = jnp.zeros((SUBLANES, HEAD_DIM), F32)
    lam_rows = lam_rows.at[0].set(lam_q1[0]).at[1].set(lam_k1[0])
    lam_rows = lam_rows.at[2].set(lam_q2[0]).at[3].set(lam_k2[0])
    att = _diff_attention(qkv, lam_rows, row(attn_sub_g[0]), lambda_init)
    wr1, br1 = _router_params(router_g_w[1], router_g_b[1], router_e_w[1], router_e_b[1])
    x3, h, ids, wts = _proj_router(x2, att, attn_wo[0].astype(BF16), row(norm_ffn_g[1]),
                                   wr1, br1)
    ys, pos = _moe_layer(1, h, ids, exp_w_gate, exp_w_up, exp_w_down)
    (out,) = _combine_norm(x3, ys, pos, wts, row(final_g), emit_sum=False, norm_dtype=F32)
    return out.reshape(BATCH, SEQ, D_MODEL)
```

```python
import functools
import math

import jax
import jax.numpy as jnp
from jax import lax
from jax.experimental import pallas as pl
from jax.experimental.pallas import tpu as pltpu

F32 = jnp.float32
BF16 = jnp.bfloat16
I32 = jnp.int32

D_MODEL = 2048
BATCH = 8
SEQ = 2048
N_TOK = BATCH * SEQ
CHUNK = 64
POOL_WINDOWS = (2, 4, 8, 16)
POOL_CG = D_MODEL // len(POOL_WINDOWS)
POOL_HALO = 16
N_HEADS = 8
HEAD_DIM = 128
N_GROUPS = 4
EXPERTS_PER_GROUP = 8
N_EXPERTS = 32
EXPERT_FF = 512
NORM_EPS = 1e-6
SUBLN_EPS = 1e-5
NEG_INF = -1e30

SUBLANES = 8
LANES = 128
RPT = D_MODEL // LANES
RPT_PAD = RPT + 1
MXU_K = 256
ROUTER_COLS = LANES
E_COL0 = 8

TS = 256
TD = 512
TM = 256
N_SORT = 2 * N_TOK + N_EXPERTS * TM
N_TILES = N_SORT // TM
TQ = 512
TK = 512
MM_TM = 1024
RANK_BLOCK = 1024
VMEM_LIMIT = 56 * 1024 * 1024


def _rms(x, g, eps):
    ms = jnp.mean(x * x, axis=-1, keepdims=True)
    return x * lax.rsqrt(ms + eps) * g


def _to_rows(ref, val, n, pitch=RPT):
    for c in range(RPT):
        ref[pl.ds(c, n, stride=pitch), :] = val[:, c * LANES:(c + 1) * LANES]


def _from_rows(ref, n, lo=0, hi=RPT, pitch=RPT):
    return jnp.concatenate(
        [ref[pl.ds(c, n, stride=pitch), :] for c in range(lo, hi)], axis=-1)


def _router_tail(x_new, gffn_ref, wr_ref, br_ref, h_ref, ids_ref, wts_ref):
    h = _rms(x_new, gffn_ref[...], NORM_EPS)
    h_ref[...] = h
    h_hi = h.astype(BF16)
    h_lo = (h - h_hi.astype(F32)).astype(BF16)
    hi_terms = jnp.dot(h_hi, wr_ref[...], preferred_element_type=F32)
    logits = (hi_terms[:, :ROUTER_COLS] + hi_terms[:, ROUTER_COLS:]
              + jnp.dot(h_lo, wr_ref[:, :ROUTER_COLS], preferred_element_type=F32))
    logits = logits + br_ref[...]
    lt = logits.T
    gl = lt[0:SUBLANES, :]
    io8 = lax.broadcasted_iota(I32, gl.shape, 0)
    gmax = jnp.max(gl, axis=0, keepdims=True)
    gtop = jnp.min(jnp.where(gl == gmax, io8, SUBLANES), axis=0, keepdims=True)
    p_g = 1.0 / jnp.sum(jnp.exp(gl - gmax), axis=0, keepdims=True)
    esel = lt[E_COL0:E_COL0 + EXPERTS_PER_GROUP, :]
    for gi in range(1, N_GROUPS):
        lo = E_COL0 + gi * EXPERTS_PER_GROUP
        esel = jnp.where(gtop == gi, lt[lo:lo + EXPERTS_PER_GROUP, :], esel)
    m1 = jnp.max(esel, axis=0, keepdims=True)
    i1 = jnp.min(jnp.where(esel == m1, io8, SUBLANES), axis=0, keepdims=True)
    rest = jnp.where(io8 == i1, -jnp.inf, esel)
    m2 = jnp.max(rest, axis=0, keepdims=True)
    i2 = jnp.min(jnp.where(rest == m2, io8, SUBLANES), axis=0, keepdims=True)
    r = jnp.exp(m2 - m1)
    v1 = 1.0 / (1.0 + r)
    v2 = r / (1.0 + r)
    base = gtop * EXPERTS_PER_GROUP
    ids_ref[0:1, :] = base + i1
    ids_ref[1:2, :] = base + i2
    wts_ref[0:1, :] = p_g * v1
    wts_ref[1:2, :] = p_g * v2


def _router_out_shapes():
    return (jax.ShapeDtypeStruct((N_TOK, D_MODEL), F32),
            jax.ShapeDtypeStruct((2, N_TOK), I32),
            jax.ShapeDtypeStruct((2, N_TOK), F32))


def _router_out_specs(tok_map):
    return (pl.BlockSpec((TS, D_MODEL), lambda *a: (tok_map(*a), 0)),
            pl.BlockSpec((2, TS), lambda *a: (0, tok_map(*a))),
            pl.BlockSpec((2, TS), lambda *a: (0, tok_map(*a))))


def _pool_kernel(x_ref, halo_ref, gmix_ref, wpool_ref, pscale_ref, gffn_ref, wr_ref, br_ref,
                 x1_ref, h_ref, ids_ref, wts_ref, hs_ref):
    s = pl.program_id(1)
    xt = x_ref[...]
    g = gmix_ref[...]
    h = _rms(xt, g, NORM_EPS)
    hh = _rms(halo_ref[...], g, NORM_EPS)
    hs_ref[0:POOL_HALO, :] = jnp.where(s > 0, hh, 0.0)
    hs_ref[POOL_HALO:, :] = h
    pos = s * TS + lax.broadcasted_iota(I32, (TS, 1), 0)
    outs = []
    for gi, w in enumerate(POOL_WINDOWS):
        c0 = gi * POOL_CG
        he = hs_ref[:, c0:c0 + POOL_CG]
        acc = he
        sh = 1
        while sh < w:
            acc = acc + pltpu.roll(acc, sh, axis=0)
            sh *= 2
        inv_cnt = 1.0 / jnp.minimum(pos + 1, w).astype(F32)
        mix = acc[POOL_HALO:, :] * inv_cnt - he[POOL_HALO:, :]
        outs.append(jnp.dot(mix.astype(BF16), wpool_ref[gi], preferred_element_type=F32))
    x1 = xt + jnp.concatenate(outs, axis=-1) * pscale_ref[...]
    x1_ref[...] = x1
    _router_tail(x1, gffn_ref, wr_ref, br_ref, h_ref, ids_ref, wts_ref)


def _pool_layer(x2d, gmix, wpool_bf, pscale, gffn, wr2, br):
    n_s = SEQ // TS
    tok_map = lambda b, s: b * n_s + s
    halo_blocks = TS // POOL_HALO
    const2 = lambda b, s: (0, 0)
    return pl.pallas_call(
        _pool_kernel,
        grid=(BATCH, n_s),
        in_specs=[
            pl.BlockSpec((TS, D_MODEL), lambda b, s: (tok_map(b, s), 0)),
            pl.BlockSpec((POOL_HALO, D_MODEL),
                         lambda b, s: (jnp.maximum(tok_map(b, s) * halo_blocks - 1, 0), 0)),
            pl.BlockSpec((1, D_MODEL), const2),
            pl.BlockSpec((len(POOL_WINDOWS), POOL_CG, POOL_CG), lambda b, s: (0, 0, 0)),
            pl.BlockSpec((1, D_MODEL), const2),
            pl.BlockSpec((1, D_MODEL), const2),
            pl.BlockSpec((D_MODEL, 2 * ROUTER_COLS), const2),
            pl.BlockSpec((1, ROUTER_COLS), const2),
        ],
        out_specs=(pl.BlockSpec((TS, D_MODEL), lambda b, s: (tok_map(b, s), 0)),)
        + _router_out_specs(tok_map),
        out_shape=(jax.ShapeDtypeStruct((N_TOK, D_MODEL), F32),) + _router_out_shapes(),
        scratch_shapes=[pltpu.VMEM((TS + POOL_HALO, D_MODEL), F32)],
        compiler_params=pltpu.CompilerParams(
            dimension_semantics=("arbitrary", "arbitrary"), vmem_limit_bytes=VMEM_LIMIT),
        name="pool_mixer_router",
    )(x2d, x2d, gmix, wpool_bf, pscale, gffn, wr2, br)


def _proj_router_kernel(x_ref, a_ref, wo_ref, gffn_ref, wr_ref, br_ref,
                        xo_ref, h_ref, ids_ref, wts_ref):
    x_new = x_ref[...] + jnp.dot(a_ref[...], wo_ref[...], preferred_element_type=F32)
    xo_ref[...] = x_new
    _router_tail(x_new, gffn_ref, wr_ref, br_ref, h_ref, ids_ref, wts_ref)


def _proj_router(x2d, att, wo_bf, gffn, wr2, br):
    tok_map = lambda i: i
    const2 = lambda i: (0, 0)
    return pl.pallas_call(
        _proj_router_kernel,
        grid=(N_TOK // TS,),
        in_specs=[
            pl.BlockSpec((TS, D_MODEL), lambda i: (i, 0)),
            pl.BlockSpec((TS, D_MODEL), lambda i: (i, 0)),
            pl.BlockSpec((D_MODEL, D_MODEL), const2),
            pl.BlockSpec((1, D_MODEL), const2),
            pl.BlockSpec((D_MODEL, 2 * ROUTER_COLS), const2),
            pl.BlockSpec((1, ROUTER_COLS), const2),
        ],
        out_specs=(pl.BlockSpec((TS, D_MODEL), lambda i: (i, 0)),) + _router_out_specs(tok_map),
        out_shape=(jax.ShapeDtypeStruct((N_TOK, D_MODEL), F32),) + _router_out_shapes(),
        compiler_params=pltpu.CompilerParams(
            dimension_semantics=("arbitrary",), vmem_limit_bytes=VMEM_LIMIT),
        name="proj_router",
    )(x2d, att, wo_bf, gffn, wr2, br)


def _dispatch_kernel(pad0_ref, padn_ref, nv_ref, p0_ref, p1_ref, h_ref, xs_ref,
                     hbuf, zrow, ztile, sem, sem_z):
    i = pl.program_id(0)
    n_blk = pl.num_programs(0)
    slot = i % 2

    def zero_row(p):
        return pltpu.make_async_copy(zrow.at[pl.ds(0, 1), :], xs_ref.at[pl.ds(p, 1), :], sem_z)

    def zero_rows8(p):
        return pltpu.make_async_copy(
            zrow, xs_ref.at[pl.ds(pl.multiple_of(p, SUBLANES), SUBLANES), :], sem_z)

    def zero_tile(t):
        return pltpu.make_async_copy(
            ztile, xs_ref.at[pl.ds(pl.multiple_of(t * TM, TM), TM), :], sem_z)

    def for_each_unowned(row_fn, rows8_fn, tile_fn):
        def per_expert(e, c):
            start = pad0_ref[e]
            n_single = (-start) % SUBLANES
            n_rows8 = (padn_ref[e] - n_single) // SUBLANES

            def per_row(j, c2):
                row_fn(start + j)
                return c2

            def per_rows8(j, c2):
                rows8_fn(start + n_single + j * SUBLANES)
                return c2
            lax.fori_loop(0, n_single, per_row, c)
            return lax.fori_loop(0, n_rows8, per_rows8, c)
        lax.fori_loop(0, N_EXPERTS, per_expert, 0)

        def per_tile(t, c):
            tile_fn(t)
            return c
        lax.fori_loop(nv_ref[0], N_TILES, per_tile, 0)

    @pl.when(i == 0)
    def _():
        zrow[...] = jnp.zeros(zrow.shape, zrow.dtype)
        ztile[...] = jnp.zeros(ztile.shape, ztile.dtype)
        for_each_unowned(lambda p: zero_row(p).start(), lambda p: zero_rows8(p).start(),
                         lambda t: zero_tile(t).start())

    def send_wait(s):
        for _ in range(2):
            pltpu.make_async_copy(hbuf.at[s], xs_ref.at[pl.ds(0, TD), :], sem.at[s]).wait()

    @pl.when(i >= 2)
    def _():
        send_wait(slot)

    hbuf[slot] = h_ref[...]
    for k, p_ref in enumerate((p0_ref, p1_ref)):
        for r in range(TD):
            pltpu.make_async_copy(hbuf.at[slot, pl.ds(r, 1), :],
                                  xs_ref.at[pl.ds(p_ref[0, 0, r], 1), :],
                                  sem.at[slot]).start(priority=(r + k) % 2)

    @pl.when(i == n_blk - 1)
    def _():
        send_wait(1 - slot)
        send_wait(slot)
        for_each_unowned(lambda p: zero_row(p).wait(), lambda p: zero_rows8(p).wait(),
                         lambda t: zero_tile(t).wait())


def _dispatch(h, pos, pad_start, pad_len, n_valid):
    n_blk = N_TOK // TD
    idx_blk = lambda: pl.BlockSpec((1, 1, TD), lambda i, *_: (i, 0, 0), memory_space=pltpu.SMEM)
    pos3 = [pos[k].reshape(n_blk, 1, TD) for k in range(2)]
    grid_spec = pltpu.PrefetchScalarGridSpec(
        num_scalar_prefetch=3,
        grid=(n_blk,),
        in_specs=[idx_blk(), idx_blk(),
                  pl.BlockSpec((TD, D_MODEL), lambda i, *_: (i, 0))],
        out_specs=pl.BlockSpec(memory_space=pl.ANY),
        scratch_shapes=[pltpu.VMEM((2, TD, D_MODEL), F32),
                        pltpu.VMEM((SUBLANES, D_MODEL), F32),
                        pltpu.VMEM((TM, D_MODEL), F32),
                        pltpu.SemaphoreType.DMA((2,)),
                        pltpu.SemaphoreType.DMA(())],
    )
    return pl.pallas_call(
        _dispatch_kernel,
        grid_spec=grid_spec,
        out_shape=jax.ShapeDtypeStruct((N_SORT, D_MODEL), F32),
        compiler_params=pltpu.CompilerParams(
            dimension_semantics=("arbitrary",), vmem_limit_bytes=VMEM_LIMIT),
        name="dispatch_rows",
    )(pad_start, pad_len, n_valid, pos3[0], pos3[1], h)


def _moe_kernel(te_ref, nv_ref, first_ref, ord_ref, nxt_ref, x_ref, wg_hbm, wu_hbm, wd_hbm, o_ref,
                wg_f32, wu_f32, wd_f32, wg_bf, wu_bf, wd_bf, sem_w, *, layer):
    i = pl.program_id(0)
    nv = nv_ref[0]

    def weight_copies(e, s):
        return [pltpu.make_async_copy(hbm.at[layer, e], buf.at[s], sem_w.at[s])
                for hbm, buf in ((wg_hbm, wg_f32), (wu_hbm, wu_f32), (wd_hbm, wd_f32))]

    @pl.when(i == 0)
    def _():
        for cp in weight_copies(te_ref[0], 0):
            cp.start()

    @pl.when((i < nv) & (first_ref[i] > 0))
    def _():
        e = te_ref[i]
        s = ord_ref[i] % 2
        for cp in weight_copies(e, s):
            cp.wait()
        wg_bf[...] = wg_f32[s].astype(BF16)
        wu_bf[...] = wu_f32[s].astype(BF16)
        wd_bf[...] = wd_f32[s].astype(BF16)
        e_next = nxt_ref[i]

        @pl.when(e_next != e)
        def _():
            for cp in weight_copies(e_next, 1 - s):
                cp.start()

    @pl.when(i < nv)
    def _():
        x = x_ref[...].astype(BF16)
        g = jnp.dot(x, wg_bf[...], preferred_element_type=F32)
        u = jnp.dot(x, wu_bf[...], preferred_element_type=F32)
        hid = (g / (1.0 + jnp.exp(-g)) * u).astype(BF16)
        o_ref[...] = jnp.dot(hid, wd_bf[...], preferred_element_type=F32)

    @pl.when(i >= nv)
    def _():
        o_ref[...] = jnp.zeros(o_ref.shape, o_ref.dtype)


def _moe_grouped(layer, tiles, xs, w_gate, w_up, w_down):
    tile_expert, n_valid, first, ordinal, next_expert = tiles
    grid_spec = pltpu.PrefetchScalarGridSpec(
        num_scalar_prefetch=5,
        grid=(N_TILES,),
        in_specs=[
            pl.BlockSpec((TM, D_MODEL), lambda i, te, nv, *_: (jnp.minimum(i, nv[0] - 1), 0)),
            pl.BlockSpec(memory_space=pl.ANY),
            pl.BlockSpec(memory_space=pl.ANY),
            pl.BlockSpec(memory_space=pl.ANY),
        ],
        out_specs=pl.BlockSpec((TM, D_MODEL), lambda i, *_: (i, 0)),
        scratch_shapes=[pltpu.VMEM((2, D_MODEL, EXPERT_FF), F32),
                        pltpu.VMEM((2, D_MODEL, EXPERT_FF), F32),
                        pltpu.VMEM((2, EXPERT_FF, D_MODEL), F32),
                        pltpu.VMEM((D_MODEL, EXPERT_FF), BF16),
                        pltpu.VMEM((D_MODEL, EXPERT_FF), BF16),
                        pltpu.VMEM((EXPERT_FF, D_MODEL), BF16),
                        pltpu.SemaphoreType.DMA((2,))],
    )
    return pl.pallas_call(
        functools.partial(_moe_kernel, layer=layer),
        grid_spec=grid_spec,
        out_shape=jax.ShapeDtypeStruct((N_SORT, D_MODEL), F32),
        compiler_params=pltpu.CompilerParams(
            dimension_semantics=("arbitrary",), vmem_limit_bytes=VMEM_LIMIT),
        name="moe_grouped",
    )(tile_expert, n_valid, first, ordinal, next_expert, xs, w_gate, w_up, w_down)


def _rank_kernel(e_ref, tri_ref, rank_ref, cnt_ref, carry):
    b = pl.program_id(0)

    @pl.when(b == 0)
    def _():
        carry[...] = jnp.zeros(carry.shape, carry.dtype)

    e = e_ref[...]
    is_e = lax.broadcasted_iota(I32, (N_EXPERTS, RANK_BLOCK), 0) == e
    onehot = jnp.where(is_e, 1.0, 0.0).astype(BF16)
    prefix = jnp.dot(onehot, tri_ref[...], preferred_element_type=F32)
    before = carry[...]
    seen = prefix + before[:, 0:1]
    rank = jnp.sum(jnp.where(is_e, seen, 0.0), axis=0, keepdims=True) - 1.0
    rank_ref[...] = rank.astype(I32)
    total = before + prefix[:, RANK_BLOCK - 1:RANK_BLOCK]
    carry[...] = total
    cnt_ref[...] = total.astype(I32)


def _rank_in_expert(e_flat):
    n = e_flat.shape[1]
    tri = jnp.triu(jnp.ones((RANK_BLOCK, RANK_BLOCK), BF16))
    rank, cnt = pl.pallas_call(
        _rank_kernel,
        grid=(n // RANK_BLOCK,),
        in_specs=[pl.BlockSpec((1, RANK_BLOCK), lambda b: (0, b)),
                  pl.BlockSpec((RANK_BLOCK, RANK_BLOCK), lambda b: (0, 0))],
        out_specs=(pl.BlockSpec((1, RANK_BLOCK), lambda b: (0, b)),
                   pl.BlockSpec((N_EXPERTS, LANES), lambda b: (0, 0))),
        out_shape=(jax.ShapeDtypeStruct((1, n), I32),
                   jax.ShapeDtypeStruct((N_EXPERTS, LANES), I32)),
        scratch_shapes=[pltpu.VMEM((N_EXPERTS, LANES), F32)],
        compiler_params=pltpu.CompilerParams(
            dimension_semantics=("arbitrary",), vmem_limit_bytes=VMEM_LIMIT),
        name="rank_in_expert",
    )(e_flat, tri)
    return rank, cnt[:, 0]


def _route_plan(ids):
    e = ids.reshape(-1)
    rank, counts = _rank_in_expert(ids.reshape(1, -1))
    padded = (counts + TM - 1) // TM * TM
    ends = jnp.cumsum(padded)
    offs = ends - padded
    is_e = e[:, None] == jnp.arange(N_EXPERTS, dtype=I32)[None, :]
    pos = jnp.sum(jnp.where(is_e, offs[None, :], 0), axis=1) + rank.reshape(-1)
    pos = pos.reshape(2, N_TOK)
    n_valid = (ends[-1] // TM).astype(I32)
    tile = jnp.arange(N_TILES, dtype=I32)
    tile_start = jnp.minimum(tile, n_valid - 1) * TM
    n_done = jnp.sum((ends[None, :] <= tile_start[:, None]).astype(I32), axis=1)
    tile_expert = jnp.minimum(n_done, N_EXPERTS - 1)
    prev_expert = jnp.concatenate([jnp.full((1,), -1, I32), tile_expert[:-1]])
    first = ((tile_expert != prev_expert) & (tile < n_valid)).astype(I32)
    ordinal = jnp.cumsum(first) - 1
    expert = jnp.arange(N_EXPERTS, dtype=I32)
    used_id = jnp.where(counts > 0, expert, N_EXPERTS)
    later = expert[None, :] > expert[:, None]
    next_used = jnp.min(jnp.where(later, used_id[None, :], N_EXPERTS), axis=1)
    next_used = jnp.where(next_used < N_EXPERTS, next_used, expert)
    next_expert = jnp.sum(jnp.where(tile_expert[:, None] == expert[None, :],
                                    next_used[None, :], 0), axis=1)
    tiles = (tile_expert, n_valid.reshape(1), first, ordinal, next_expert)
    return pos, offs + counts, padded - counts, tiles


def _moe_layer(layer, h, ids, w_gate, w_up, w_down):
    pos, pad_start, pad_len, tiles = _route_plan(ids)
    xs = _dispatch(h, pos, pad_start, pad_len, tiles[1])
    ys = _moe_grouped(layer, tiles, xs, w_gate, w_up, w_down)
    return ys, pos


def _combine_norm_kernel(p0f_ref, p1f_ref, p0n_ref, p1n_ref, x_ref, w_ref, ys_ref, g_ref, *rest,
                         emit_sum):
    out_refs, (ybuf, sem) = rest[:-2], rest[-2:]
    i = pl.program_id(0)
    n_blk = pl.num_programs(0)
    slot = i % 2
    other = 1 - slot

    def fetch(idx_refs, s):
        for k in range(2):
            for r in range(TD):
                pltpu.make_async_copy(ys_ref.at[pl.ds(idx_refs[k][0, 0, r], 1), :],
                                      ybuf.at[s, k, pl.ds(r, 1), :],
                                      sem.at[s]).start(priority=r % 2)

    @pl.when(i == 0)
    def _():
        fetch((p0f_ref, p1f_ref), 0)

    for k in range(2):
        pltpu.make_async_copy(ys_ref.at[pl.ds(0, TD), :], ybuf.at[slot, k], sem.at[slot]).wait()

    @pl.when(i + 1 < n_blk)
    def _():
        fetch((p0n_ref, p1n_ref), other)

    w = w_ref[...]
    eye = (lax.broadcasted_iota(I32, (TD, TD), 0) == lax.broadcasted_iota(I32, (TD, TD), 1))
    x_new = x_ref[...]
    for k in range(2):
        w_col = jnp.sum(jnp.where(eye, w[k:k + 1, :], 0.0), axis=1, keepdims=True)
        x_new = x_new + w_col * ybuf[slot, k]
    normed = _rms(x_new, g_ref[...], NORM_EPS)
    if emit_sum:
        out_refs[0][...] = x_new
        out_refs[1][...] = normed.astype(out_refs[1].dtype)
    else:
        out_refs[0][...] = normed.astype(out_refs[0].dtype)


def _combine_norm(x2d, ys, pos, wts, g, *, emit_sum, norm_dtype):
    n_blk = N_TOK // TD
    tok_spec = pl.BlockSpec((TD, D_MODEL), lambda i: (i, 0))
    out_specs = [tok_spec]
    out_shape = [jax.ShapeDtypeStruct((N_TOK, D_MODEL), norm_dtype)]
    if emit_sum:
        out_specs = [tok_spec, tok_spec]
        out_shape = [jax.ShapeDtypeStruct((N_TOK, D_MODEL), F32)] + out_shape
    idx_blk = lambda f: pl.BlockSpec((1, 1, TD), f, memory_space=pltpu.SMEM)
    first = lambda i: (0, 0, 0)
    nxt = lambda i: (jnp.minimum(i + 1, n_blk - 1), 0, 0)
    pos3 = [pos[k].reshape(n_blk, 1, TD) for k in range(2)]
    return pl.pallas_call(
        functools.partial(_combine_norm_kernel, emit_sum=emit_sum),
        grid=(n_blk,),
        in_specs=[
            idx_blk(first), idx_blk(first), idx_blk(nxt), idx_blk(nxt),
            tok_spec,
            pl.BlockSpec((2, TD), lambda i: (0, i)),
            pl.BlockSpec(memory_space=pl.ANY),
            pl.BlockSpec((1, D_MODEL), lambda i: (0, 0)),
        ],
        out_specs=tuple(out_specs),
        out_shape=tuple(out_shape),
        scratch_shapes=[pltpu.VMEM((2, 2, TD, D_MODEL), F32),
                        pltpu.SemaphoreType.DMA((2,))],
        compiler_params=pltpu.CompilerParams(
            dimension_semantics=("arbitrary",), vmem_limit_bytes=VMEM_LIMIT),
        name="combine_norm",
    )(pos3[0], pos3[1], pos3[0], pos3[1], x2d, wts, ys, g)


def _mm_kernel(a_ref, b_ref, o_ref, *, first_block_scale):
    acc = jnp.dot(a_ref[...], b_ref[...], preferred_element_type=F32)
    if first_block_scale is not None:
        acc = acc * jnp.where(pl.program_id(0) == 0, first_block_scale, 1.0)
    o_ref[...] = acc.astype(o_ref.dtype)


def _matmul(a, b, out_dtype, tn, first_block_scale=None):
    m, k = a.shape
    n = b.shape[1]
    return pl.pallas_call(
        functools.partial(_mm_kernel, first_block_scale=first_block_scale),
        grid=(n // tn, m // MM_TM),
        in_specs=[pl.BlockSpec((MM_TM, k), lambda j, i: (i, 0)),
                  pl.BlockSpec((k, tn), lambda j, i: (0, j))],
        out_specs=pl.BlockSpec((MM_TM, tn), lambda j, i: (i, j)),
        out_shape=jax.ShapeDtypeStruct((m, n), out_dtype),
        compiler_params=pltpu.CompilerParams(
            dimension_semantics=("arbitrary", "arbitrary"), vmem_limit_bytes=VMEM_LIMIT),
        name="dense_matmul",
    )(a, b)


def _attn_kernel(lam_ref, q_ref, k_ref, v_ref, subg_ref, o_ref, *scratch, lambda_init):
    m_ref, l_ref, acc_ref, s_ref, p_ref, a_ref = (scratch[2 * n:2 * n + 2] for n in range(6))
    lane_rep = lambda a, width: jnp.concatenate([a] * (width // LANES), axis=1)
    lv = lam_ref[...]
    lam = (jnp.exp(jnp.sum(lv[0:1] * lv[1:2], axis=-1, keepdims=True))
           - jnp.exp(jnp.sum(lv[2:3] * lv[3:4], axis=-1, keepdims=True)) + lambda_init)
    width = 2 * HEAD_DIM
    sub_g = subg_ref[...] * (1.0 - lambda_init)

    def query_tile(qi, carry):
        q0 = pl.multiple_of(qi * TQ, TQ)
        q = q_ref[pl.ds(q0, TQ), :]
        qs = (q[:, :HEAD_DIM], q[:, HEAD_DIM:])

        def tile_update(j, diagonal, first):
            k0 = pl.multiple_of(j * TK, TK)
            kt = k_ref[pl.ds(k0, TK), :]
            vt = v_ref[pl.ds(k0, TK), :]
            for c in range(2):
                kc = kt[:, c * HEAD_DIM:(c + 1) * HEAD_DIM]
                s_ref[c][...] = lax.dot_general(qs[c], kc, (((1,), (1,)), ((), ())),
                                                preferred_element_type=F32)
            for c in range(2):
                for rb in range(TQ // CHUNK):
                    rows = pl.ds(rb * CHUNK, CHUNK)
                    ncol = TK
                    if diagonal:
                        seen = (rb + 1) * CHUNK
                        ncol = min(TK, -(-seen // LANES) * LANES)
                    s = s_ref[c][rows, :ncol]
                    if diagonal and seen < ncol:
                        col = lax.broadcasted_iota(I32, (CHUNK, ncol), 1)
                        s = jnp.where(col < seen, s, NEG_INF)
                    s_max = jnp.max(s, axis=-1, keepdims=True)
                    if first:
                        m_new = jnp.zeros((CHUNK, LANES), F32) + s_max
                    else:
                        m_old = m_ref[c][rows, :]
                        m_new = jnp.maximum(m_old, s_max)
                        alpha = jnp.exp2(m_old - m_new)
                        a_ref[c][rows, :] = alpha
                    p = jnp.exp2(s - lane_rep(m_new, ncol))
                    p_sum = jnp.sum(p, axis=-1, keepdims=True)
                    if first:
                        l_ref[c][rows, :] = jnp.zeros((CHUNK, LANES), F32) + p_sum
                    else:
                        l_ref[c][rows, :] = alpha * l_ref[c][rows, :] + p_sum
                    m_ref[c][rows, :] = m_new
                    p_ref[c][rows, :ncol] = p.astype(BF16)
                    if ncol < TK:
                        p_ref[c][rows, ncol:] = jnp.zeros((CHUNK, TK - ncol), BF16)
                pv = jnp.dot(p_ref[c][...], vt, preferred_element_type=F32)
                if first:
                    acc_ref[c][...] = pv
                else:
                    acc_ref[c][...] = lane_rep(a_ref[c][...], width) * acc_ref[c][...] + pv

        def full_body(j, c):
            tile_update(j, False, False)
            return c

        @pl.when(qi == 0)
        def _():
            tile_update(0, True, True)

        @pl.when(qi > 0)
        def _():
            tile_update(0, False, True)
            lax.fori_loop(1, qi, full_body, 0)
            tile_update(qi, True, False)

        for rb in range(TQ // CHUNK):
            rows = pl.ds(rb * CHUNK, CHUNK)
            o = (acc_ref[0][rows, :] * lane_rep(1.0 / l_ref[0][rows, :], width)
                 - acc_ref[1][rows, :] * lane_rep(lam * (1.0 / l_ref[1][rows, :]), width))
            o_ref[pl.ds(pl.multiple_of(q0 + rb * CHUNK, CHUNK), CHUNK), :] = _rms(
                o, sub_g, SUBLN_EPS).astype(o_ref.dtype)
        return carry

    lax.fori_loop(0, SEQ // TQ, query_tile, 0)


def _diff_attention(qkv, lam_rows, sub_g, lambda_init):
    assert TQ == TK
    head_w = 2 * HEAD_DIM
    return pl.pallas_call(
        functools.partial(_attn_kernel, lambda_init=lambda_init),
        grid=(BATCH, N_HEADS),
        in_specs=[
            pl.BlockSpec((SUBLANES, HEAD_DIM), lambda b, h: (0, 0)),
            pl.BlockSpec((SEQ, head_w), lambda b, h: (b, h)),
            pl.BlockSpec((SEQ, head_w), lambda b, h: (b, N_HEADS + h)),
            pl.BlockSpec((SEQ, head_w), lambda b, h: (b, 2 * N_HEADS + h)),
            pl.BlockSpec((1, head_w), lambda b, h: (0, 0)),
        ],
        out_specs=pl.BlockSpec((SEQ, head_w), lambda b, h: (b, h)),
        out_shape=jax.ShapeDtypeStruct((N_TOK, D_MODEL), BF16),
        scratch_shapes=(
            [pltpu.VMEM((TQ, LANES), F32)] * 2
            + [pltpu.VMEM((TQ, LANES), F32)] * 2
            + [pltpu.VMEM((TQ, head_w), F32)] * 2
            + [pltpu.VMEM((TQ, TK), F32)] * 2
            + [pltpu.VMEM((TQ, TK), BF16)] * 2
            + [pltpu.VMEM((TQ, LANES), F32)] * 2),
        compiler_params=pltpu.CompilerParams(
            dimension_semantics=("arbitrary", "arbitrary"),
            vmem_limit_bytes=VMEM_LIMIT),
        name="diff_attention",
    )(lam_rows, qkv, qkv, qkv, sub_g)


def _router_params(wr_g, br_g, wr_e, br_e):
    w = jnp.zeros((D_MODEL, ROUTER_COLS), F32)
    w = w.at[:, 0:N_GROUPS].set(wr_g).at[:, E_COL0:E_COL0 + N_EXPERTS].set(wr_e)
    b = jnp.full((1, ROUTER_COLS), NEG_INF, F32)
    b = b.at[0, 0:N_GROUPS].set(br_g).at[0, E_COL0:E_COL0 + N_EXPERTS].set(br_e)
    w_hi = w.astype(BF16)
    w_lo = (w - w_hi.astype(F32)).astype(BF16)
    return jnp.concatenate([w_hi, w_lo], axis=1), b


def kernel(x, norm_mix_g, norm_ffn_g, pool_w, pool_scale, attn_wq, attn_wk, attn_wv, attn_wo,
           lam_q1, lam_k1, lam_q2, lam_k2, attn_sub_g, router_g_w, router_g_b, router_e_w,
           router_e_b, exp_w_gate, exp_w_up, exp_w_down, final_g):
    x2d = x.reshape(N_TOK, D_MODEL)
    row = lambda v: v.reshape(1, -1)

    wr0, br0 = _router_params(router_g_w[0], router_g_b[0], router_e_w[0], router_e_b[0])
    x1, h, ids, wts = _pool_layer(x2d, row(norm_mix_g[0]), pool_w[0].astype(BF16),
                                  row(pool_scale[0]), row(norm_ffn_g[0]), wr0, br0)
    ys, pos = _moe_layer(0, h, ids, exp_w_gate, exp_w_up, exp_w_down)
    x2, hn = _combine_norm(x1, ys, pos, wts, row(norm_mix_g[1]), emit_sum=True, norm_dtype=BF16)

    lambda_init = 0.8 - 0.6 * math.exp(-0.3 * 1)
    w_qkv = jnp.concatenate([attn_wq[0], attn_wk[0], attn_wv[0]], axis=1).astype(BF16)
    q_scale = HEAD_DIM ** -0.5 * math.log2(math.e)
    qkv = _matmul(hn, w_qkv, BF16, D_MODEL, first_block_scale=q_scale)
    lam_rows = jnp.zeros((SUBLANES, HEAD_DIM), F32)
    lam_rows = lam_rows.at[0].set(lam_q1[0]).at[1].set(lam_k1[0])
    lam_rows = lam_rows.at[2].set(lam_q2[0]).at[3].set(lam_k2[0])
    att = _diff_attention(qkv, lam_rows, row(attn_sub_g[0]), lambda_init)
    wr1, br1 = _router_params(router_g_w[1], router_g_b[1], router_e_w[1], router_e_b[1])
    x3, h, ids, wts = _proj_router(x2, att, attn_wo[0].astype(BF16), row(norm_ffn_g[1]),
                                   wr1, br1)
    ys, pos = _moe_layer(1, h, ids, exp_w_gate, exp_w_up, exp_w_down)
    (out,) = _combine_norm(x3, ys, pos, wts, row(final_g), emit_sum=False, norm_dtype=F32)
    return out.reshape(BATCH, SEQ, D_MODEL)
```

```python
import functools
import math

import jax
import jax.numpy as jnp
from jax import lax
from jax.experimental import pallas as pl
from jax.experimental.pallas import tpu as pltpu

F32 = jnp.float32
BF16 = jnp.bfloat16
I32 = jnp.int32

D_MODEL = 2048
BATCH = 8
SEQ = 2048
N_TOK = BATCH * SEQ
CHUNK = 64
POOL_WINDOWS = (2, 4, 8, 16)
POOL_CG = D_MODEL // len(POOL_WINDOWS)
POOL_HALO = 16
N_HEADS = 8
HEAD_DIM = 128
N_GROUPS = 4
EXPERTS_PER_GROUP = 8
N_EXPERTS = 32
EXPERT_FF = 512
NORM_EPS = 1e-6
SUBLN_EPS = 1e-5
NEG_INF = -1e30

SUBLANES = 8
LANES = 128
ROUTER_COLS = LANES
E_COL0 = 8

TS = 256
TD = 512
TM = 256
N_SORT = 2 * N_TOK + N_EXPERTS * TM
N_TILES = N_SORT // TM
TQ = 512
TK = 512
MM_TM = 1024
RANK_BLOCK = 1024
VMEM_LIMIT = 56 * 1024 * 1024


def _rms(x, g, eps):
    ms = jnp.mean(x * x, axis=-1, keepdims=True)
    return x * lax.rsqrt(ms + eps) * g


def _router_tail(x_new, gffn_ref, wr_ref, br_ref, h_ref, ids_ref, wts_ref):
    h = _rms(x_new, gffn_ref[...], NORM_EPS)
    h_ref[...] = h
    h_hi = h.astype(BF16)
    h_lo = (h - h_hi.astype(F32)).astype(BF16)
    hi_terms = jnp.dot(h_hi, wr_ref[...], preferred_element_type=F32)
    logits = (hi_terms[:, :ROUTER_COLS] + hi_terms[:, ROUTER_COLS:]
              + jnp.dot(h_lo, wr_ref[:, :ROUTER_COLS], preferred_element_type=F32))
    logits = logits + br_ref[...]
    lt = logits.T
    gl = lt[0:SUBLANES, :]
    io8 = lax.broadcasted_iota(I32, gl.shape, 0)
    gmax = jnp.max(gl, axis=0, keepdims=True)
    gtop = jnp.min(jnp.where(gl == gmax, io8, SUBLANES), axis=0, keepdims=True)
    p_g = 1.0 / jnp.sum(jnp.exp(gl - gmax), axis=0, keepdims=True)
    esel = lt[E_COL0:E_COL0 + EXPERTS_PER_GROUP, :]
    for gi in range(1, N_GROUPS):
        lo = E_COL0 + gi * EXPERTS_PER_GROUP
        esel = jnp.where(gtop == gi, lt[lo:lo + EXPERTS_PER_GROUP, :], esel)
    m1 = jnp.max(esel, axis=0, keepdims=True)
    i1 = jnp.min(jnp.where(esel == m1, io8, SUBLANES), axis=0, keepdims=True)
    rest = jnp.where(io8 == i1, -jnp.inf, esel)
    m2 = jnp.max(rest, axis=0, keepdims=True)
    i2 = jnp.min(jnp.where(rest == m2, io8, SUBLANES), axis=0, keepdims=True)
    r = jnp.exp(m2 - m1)
    v1 = 1.0 / (1.0 + r)
    v2 = r / (1.0 + r)
    base = gtop * EXPERTS_PER_GROUP
    ids_ref[0:1, :] = base + i1
    ids_ref[1:2, :] = base + i2
    wts_ref[0:1, :] = p_g * v1
    wts_ref[1:2, :] = p_g * v2


def _router_out_shapes():
    return (jax.ShapeDtypeStruct((N_TOK, D_MODEL), F32),
            jax.ShapeDtypeStruct((2, N_TOK), I32),
            jax.ShapeDtypeStruct((2, N_TOK), F32))


def _router_out_specs(tok_map):
    return (pl.BlockSpec((TS, D_MODEL), lambda *a: (tok_map(*a), 0)),
            pl.BlockSpec((2, TS), lambda *a: (0, tok_map(*a))),
            pl.BlockSpec((2, TS), lambda *a: (0, tok_map(*a))))


def _pool_kernel(x_ref, halo_ref, gmix_ref, wpool_ref, pscale_ref, gffn_ref, wr_ref, br_ref,
                 x1_ref, h_ref, ids_ref, wts_ref, hs_ref):
    s = pl.program_id(1)
    xt = x_ref[...]
    g = gmix_ref[...]
    h = _rms(xt, g, NORM_EPS)
    hh = _rms(halo_ref[...], g, NORM_EPS)
    hs_ref[0:POOL_HALO, :] = jnp.where(s > 0, hh, 0.0)
    hs_ref[POOL_HALO:, :] = h
    pos = s * TS + lax.broadcasted_iota(I32, (TS, 1), 0)
    outs = []
    for gi, w in enumerate(POOL_WINDOWS):
        c0 = gi * POOL_CG
        he = hs_ref[:, c0:c0 + POOL_CG]
        acc = he
        sh = 1
        while sh < w:
            acc = acc + pltpu.roll(acc, sh, axis=0)
            sh *= 2
        inv_cnt = 1.0 / jnp.minimum(pos + 1, w).astype(F32)
        mix = acc[POOL_HALO:, :] * inv_cnt - he[POOL_HALO:, :]
        outs.append(jnp.dot(mix.astype(BF16), wpool_ref[gi], preferred_element_type=F32))
    x1 = xt + jnp.concatenate(outs, axis=-1) * pscale_ref[...]
    x1_ref[...] = x1
    _router_tail(x1, gffn_ref, wr_ref, br_ref, h_ref, ids_ref, wts_ref)


def _pool_layer(x2d, gmix, wpool_bf, pscale, gffn, wr2, br):
    n_s = SEQ // TS
    tok_map = lambda b, s: b * n_s + s
    halo_blocks = TS // POOL_HALO
    const2 = lambda b, s: (0, 0)
    return pl.pallas_call(
        _pool_kernel,
        grid=(BATCH, n_s),
        in_specs=[
            pl.BlockSpec((TS, D_MODEL), lambda b, s: (tok_map(b, s), 0)),
            pl.BlockSpec((POOL_HALO, D_MODEL),
                         lambda b, s: (jnp.maximum(tok_map(b, s) * halo_blocks - 1, 0), 0)),
            pl.BlockSpec((1, D_MODEL), const2),
            pl.BlockSpec((len(POOL_WINDOWS), POOL_CG, POOL_CG), lambda b, s: (0, 0, 0)),
            pl.BlockSpec((1, D_MODEL), const2),
            pl.BlockSpec((1, D_MODEL), const2),
            pl.BlockSpec((D_MODEL, 2 * ROUTER_COLS), const2),
            pl.BlockSpec((1, ROUTER_COLS), const2),
        ],
        out_specs=(pl.BlockSpec((TS, D_MODEL), lambda b, s: (tok_map(b, s), 0)),)
        + _router_out_specs(tok_map),
        out_shape=(jax.ShapeDtypeStruct((N_TOK, D_MODEL), F32),) + _router_out_shapes(),
        scratch_shapes=[pltpu.VMEM((TS + POOL_HALO, D_MODEL), F32)],
        compiler_params=pltpu.CompilerParams(
            dimension_semantics=("arbitrary", "arbitrary"), vmem_limit_bytes=VMEM_LIMIT),
        name="pool_mixer_router",
    )(x2d, x2d, gmix, wpool_bf, pscale, gffn, wr2, br)


def _proj_router_kernel(x_ref, a_ref, wo_ref, gffn_ref, wr_ref, br_ref,
                        xo_ref, h_ref, ids_ref, wts_ref):
    x_new = x_ref[...] + jnp.dot(a_ref[...], wo_ref[...], preferred_element_type=F32)
    xo_ref[...] = x_new
    _router_tail(x_new, gffn_ref, wr_ref, br_ref, h_ref, ids_ref, wts_ref)


def _proj_router(x2d, att, wo_bf, gffn, wr2, br):
    tok_map = lambda i: i
    const2 = lambda i: (0, 0)
    return pl.pallas_call(
        _proj_router_kernel,
        grid=(N_TOK // TS,),
        in_specs=[
            pl.BlockSpec((TS, D_MODEL), lambda i: (i, 0)),
            pl.BlockSpec((TS, D_MODEL), lambda i: (i, 0)),
            pl.BlockSpec((D_MODEL, D_MODEL), const2),
            pl.BlockSpec((1, D_MODEL), const2),
            pl.BlockSpec((D_MODEL, 2 * ROUTER_COLS), const2),
            pl.BlockSpec((1, ROUTER_COLS), const2),
        ],
        out_specs=(pl.BlockSpec((TS, D_MODEL), lambda i: (i, 0)),) + _router_out_specs(tok_map),
        out_shape=(jax.ShapeDtypeStruct((N_TOK, D_MODEL), F32),) + _router_out_shapes(),
        compiler_params=pltpu.CompilerParams(
            dimension_semantics=("arbitrary",), vmem_limit_bytes=VMEM_LIMIT),
        name="proj_router",
    )(x2d, att, wo_bf, gffn, wr2, br)


def _dispatch_kernel(pad0_ref, padn_ref, nv_ref, p0_ref, p1_ref, h_ref, xs_ref,
                     hbuf, zrow, ztile, sem, sem_z):
    i = pl.program_id(0)
    n_blk = pl.num_programs(0)
    slot = i % 2

    def zero_row(p):
        return pltpu.make_async_copy(zrow.at[pl.ds(0, 1), :], xs_ref.at[pl.ds(p, 1), :], sem_z)

    def zero_rows8(p):
        return pltpu.make_async_copy(
            zrow, xs_ref.at[pl.ds(pl.multiple_of(p, SUBLANES), SUBLANES), :], sem_z)

    def zero_tile(t):
        return pltpu.make_async_copy(
            ztile, xs_ref.at[pl.ds(pl.multiple_of(t * TM, TM), TM), :], sem_z)

    def for_each_unowned(row_fn, rows8_fn, tile_fn):
        def per_expert(e, c):
            start = pad0_ref[e]
            n_single = (-start) % SUBLANES
            n_rows8 = (padn_ref[e] - n_single) // SUBLANES

            def per_row(j, c2):
                row_fn(start + j)
                return c2

            def per_rows8(j, c2):
                rows8_fn(start + n_single + j * SUBLANES)
                return c2
            lax.fori_loop(0, n_single, per_row, c)
            return lax.fori_loop(0, n_rows8, per_rows8, c)
        lax.fori_loop(0, N_EXPERTS, per_expert, 0)

        def per_tile(t, c):
            tile_fn(t)
            return c
        lax.fori_loop(nv_ref[0], N_TILES, per_tile, 0)

    @pl.when(i == 0)
    def _():
        zrow[...] = jnp.zeros(zrow.shape, zrow.dtype)
        ztile[...] = jnp.zeros(ztile.shape, ztile.dtype)
        for_each_unowned(lambda p: zero_row(p).start(), lambda p: zero_rows8(p).start(),
                         lambda t: zero_tile(t).start())

    def send_wait(s):
        for _ in range(2):
            pltpu.make_async_copy(hbuf.at[s], xs_ref.at[pl.ds(0, TD), :], sem.at[s]).wait()

    @pl.when(i >= 2)
    def _():
        send_wait(slot)

    hbuf[slot] = h_ref[...]
    for k, p_ref in enumerate((p0_ref, p1_ref)):
        for r in range(TD):
            pltpu.make_async_copy(hbuf.at[slot, pl.ds(r, 1), :],
                                  xs_ref.at[pl.ds(p_ref[0, 0, r], 1), :],
                                  sem.at[slot]).start(priority=(r + k) % 2)

    @pl.when(i == n_blk - 1)
    def _():
        send_wait(1 - slot)
        send_wait(slot)
        for_each_unowned(lambda p: zero_row(p).wait(), lambda p: zero_rows8(p).wait(),
                         lambda t: zero_tile(t).wait())


def _dispatch(h, pos, pad_start, pad_len, n_valid):
    n_blk = N_TOK // TD
    idx_blk = lambda: pl.BlockSpec((1, 1, TD), lambda i, *_: (i, 0, 0), memory_space=pltpu.SMEM)
    pos3 = [pos[k].reshape(n_blk, 1, TD) for k in range(2)]
    grid_spec = pltpu.PrefetchScalarGridSpec(
        num_scalar_prefetch=3,
        grid=(n_blk,),
        in_specs=[idx_blk(), idx_blk(),
                  pl.BlockSpec((TD, D_MODEL), lambda i, *_: (i, 0))],
        out_specs=pl.BlockSpec(memory_space=pl.ANY),
        scratch_shapes=[pltpu.VMEM((2, TD, D_MODEL), F32),
                        pltpu.VMEM((SUBLANES, D_MODEL), F32),
                        pltpu.VMEM((TM, D_MODEL), F32),
                        pltpu.SemaphoreType.DMA((2,)),
                        pltpu.SemaphoreType.DMA(())],
    )
    return pl.pallas_call(
        _dispatch_kernel,
        grid_spec=grid_spec,
        out_shape=jax.ShapeDtypeStruct((N_SORT, D_MODEL), F32),
        compiler_params=pltpu.CompilerParams(
            dimension_semantics=("arbitrary",), vmem_limit_bytes=VMEM_LIMIT),
        name="dispatch_rows",
    )(pad_start, pad_len, n_valid, pos3[0], pos3[1], h)


def _moe_kernel(te_ref, nv_ref, first_ref, ord_ref, nxt_ref, x_ref, wg_hbm, wu_hbm, wd_hbm, o_ref,
                wg_f32, wu_f32, wd_f32, wg_bf, wu_bf, wd_bf, sem_w, *, layer):
    i = pl.program_id(0)
    nv = nv_ref[0]

    def weight_copies(e, s):
        return [pltpu.make_async_copy(hbm.at[layer, e], buf.at[s], sem_w.at[s])
                for hbm, buf in ((wg_hbm, wg_f32), (wu_hbm, wu_f32), (wd_hbm, wd_f32))]

    @pl.when(i == 0)
    def _():
        for cp in weight_copies(te_ref[0], 0):
            cp.start()

    @pl.when((i < nv) & (first_ref[i] > 0))
    def _():
        e = te_ref[i]
        s = ord_ref[i] % 2
        for cp in weight_copies(e, s):
            cp.wait()
        wg_bf[...] = wg_f32[s].astype(BF16)
        wu_bf[...] = wu_f32[s].astype(BF16)
        wd_bf[...] = wd_f32[s].astype(BF16)
        e_next = nxt_ref[i]

        @pl.when(e_next != e)
        def _():
            for cp in weight_copies(e_next, 1 - s):
                cp.start()

    @pl.when(i < nv)
    def _():
        x = x_ref[...].astype(BF16)
        g = jnp.dot(x, wg_bf[...], preferred_element_type=F32)
        u = jnp.dot(x, wu_bf[...], preferred_element_type=F32)
        hid = (g / (1.0 + jnp.exp(-g)) * u).astype(BF16)
        o_ref[...] = jnp.dot(hid, wd_bf[...], preferred_element_type=F32)

    @pl.when(i >= nv)
    def _():
        o_ref[...] = jnp.zeros(o_ref.shape, o_ref.dtype)


def _moe_grouped(layer, tiles, xs, w_gate, w_up, w_down):
    tile_expert, n_valid, first, ordinal, next_expert = tiles
    grid_spec = pltpu.PrefetchScalarGridSpec(
        num_scalar_prefetch=5,
        grid=(N_TILES,),
        in_specs=[
            pl.BlockSpec((TM, D_MODEL), lambda i, te, nv, *_: (jnp.minimum(i, nv[0] - 1), 0)),
            pl.BlockSpec(memory_space=pl.ANY),
            pl.BlockSpec(memory_space=pl.ANY),
            pl.BlockSpec(memory_space=pl.ANY),
        ],
        out_specs=pl.BlockSpec((TM, D_MODEL), lambda i, *_: (i, 0)),
        scratch_shapes=[pltpu.VMEM((2, D_MODEL, EXPERT_FF), F32),
                        pltpu.VMEM((2, D_MODEL, EXPERT_FF), F32),
                        pltpu.VMEM((2, EXPERT_FF, D_MODEL), F32),
                        pltpu.VMEM((D_MODEL, EXPERT_FF), BF16),
                        pltpu.VMEM((D_MODEL, EXPERT_FF), BF16),
                        pltpu.VMEM((EXPERT_FF, D_MODEL), BF16),
                        pltpu.SemaphoreType.DMA((2,))],
    )
    return pl.pallas_call(
        functools.partial(_moe_kernel, layer=layer),
        grid_spec=grid_spec,
        out_shape=jax.ShapeDtypeStruct((N_SORT, D_MODEL), F32),
        compiler_params=pltpu.CompilerParams(
            dimension_semantics=("arbitrary",), vmem_limit_bytes=VMEM_LIMIT),
        name="moe_grouped",
    )(tile_expert, n_valid, first, ordinal, next_expert, xs, w_gate, w_up, w_down)


def _rank_kernel(e_ref, tri_ref, rank_ref, cnt_ref, carry):
    b = pl.program_id(0)

    @pl.when(b == 0)
    def _():
        carry[...] = jnp.zeros(carry.shape, carry.dtype)

    e = e_ref[...]
    is_e = lax.broadcasted_iota(I32, (N_EXPERTS, RANK_BLOCK), 0) == e
    onehot = jnp.where(is_e, 1.0, 0.0).astype(BF16)
    prefix = jnp.dot(onehot, tri_ref[...], preferred_element_type=F32)
    before = carry[...]
    seen = prefix + before[:, 0:1]
    rank = jnp.sum(jnp.where(is_e, seen, 0.0), axis=0, keepdims=True) - 1.0
    rank_ref[...] = rank.astype(I32)
    total = before + prefix[:, RANK_BLOCK - 1:RANK_BLOCK]
    carry[...] = total
    cnt_ref[...] = total.astype(I32)


def _rank_in_expert(e_flat):
    n = e_flat.shape[1]
    tri = jnp.triu(jnp.ones((RANK_BLOCK, RANK_BLOCK), BF16))
    rank, cnt = pl.pallas_call(
        _rank_kernel,
        grid=(n // RANK_BLOCK,),
        in_specs=[pl.BlockSpec((1, RANK_BLOCK), lambda b: (0, b)),
                  pl.BlockSpec((RANK_BLOCK, RANK_BLOCK), lambda b: (0, 0))],
        out_specs=(pl.BlockSpec((1, RANK_BLOCK), lambda b: (0, b)),
                   pl.BlockSpec((N_EXPERTS, LANES), lambda b: (0, 0))),
        out_shape=(jax.ShapeDtypeStruct((1, n), I32),
                   jax.ShapeDtypeStruct((N_EXPERTS, LANES), I32)),
        scratch_shapes=[pltpu.VMEM((N_EXPERTS, LANES), F32)],
        compiler_params=pltpu.CompilerParams(
            dimension_semantics=("arbitrary",), vmem_limit_bytes=VMEM_LIMIT),
        name="rank_in_expert",
    )(e_flat, tri)
    return rank, cnt[:, 0]


def _route_plan(ids):
    e = ids.reshape(-1)
    rank, counts = _rank_in_expert(ids.reshape(1, -1))
    padded = (counts + TM - 1) // TM * TM
    ends = jnp.cumsum(padded)
    offs = ends - padded
    is_e = e[:, None] == jnp.arange(N_EXPERTS, dtype=I32)[None, :]
    pos = jnp.sum(jnp.where(is_e, offs[None, :], 0), axis=1) + rank.reshape(-1)
    pos = pos.reshape(2, N_TOK)
    n_valid = (ends[-1] // TM).astype(I32)
    tile = jnp.arange(N_TILES, dtype=I32)
    tile_start = jnp.minimum(tile, n_valid - 1) * TM
    n_done = jnp.sum((ends[None, :] <= tile_start[:, None]).astype(I32), axis=1)
    tile_expert = jnp.minimum(n_done, N_EXPERTS - 1)
    prev_expert = jnp.concatenate([jnp.full((1,), -1, I32), tile_expert[:-1]])
    first = ((tile_expert != prev_expert) & (tile < n_valid)).astype(I32)
    ordinal = jnp.cumsum(first) - 1
    expert = jnp.arange(N_EXPERTS, dtype=I32)
    used_id = jnp.where(counts > 0, expert, N_EXPERTS)
    later = expert[None, :] > expert[:, None]
    next_used = jnp.min(jnp.where(later, used_id[None, :], N_EXPERTS), axis=1)
    next_used = jnp.where(next_used < N_EXPERTS, next_used, expert)
    next_expert = jnp.sum(jnp.where(tile_expert[:, None] == expert[None, :],
                                    next_used[None, :], 0), axis=1)
    tiles = (tile_expert, n_valid.reshape(1), first, ordinal, next_expert)
    return pos, offs + counts, padded - counts, tiles


def _moe_layer(layer, h, ids, w_gate, w_up, w_down):
    pos, pad_start, pad_len, tiles = _route_plan(ids)
    xs = _dispatch(h, pos, pad_start, pad_len, tiles[1])
    ys = _moe_grouped(layer, tiles, xs, w_gate, w_up, w_down)
    return ys, pos


def _combine_norm_kernel(p0f_ref, p1f_ref, p0n_ref, p1n_ref, x_ref, w_ref, ys_ref, g_ref, *rest,
                         emit_sum):
    out_refs, (ybuf, sem) = rest[:-2], rest[-2:]
    i = pl.program_id(0)
    n_blk = pl.num_programs(0)
    slot = i % 2
    other = 1 - slot

    def fetch(idx_refs, s):
        for k in range(2):
            for r in range(TD):
                pltpu.make_async_copy(ys_ref.at[pl.ds(idx_refs[k][0, 0, r], 1), :],
                                      ybuf.at[s, k, pl.ds(r, 1), :],
                                      sem.at[s]).start(priority=r % 2)

    @pl.when(i == 0)
    def _():
        fetch((p0f_ref, p1f_ref), 0)

    for k in range(2):
        pltpu.make_async_copy(ys_ref.at[pl.ds(0, TD), :], ybuf.at[slot, k], sem.at[slot]).wait()

    @pl.when(i + 1 < n_blk)
    def _():
        fetch((p0n_ref, p1n_ref), other)

    w = w_ref[...]
    eye = (lax.broadcasted_iota(I32, (TD, TD), 0) == lax.broadcasted_iota(I32, (TD, TD), 1))
    x_new = x_ref[...]
    for k in range(2):
        w_col = jnp.sum(jnp.where(eye, w[k:k + 1, :], 0.0), axis=1, keepdims=True)
        x_new = x_new + w_col * ybuf[slot, k]
    normed = _rms(x_new, g_ref[...], NORM_EPS)
    if emit_sum:
        out_refs[0][...] = x_new
        out_refs[1][...] = normed.astype(out_refs[1].dtype)
    else:
        out_refs[0][...] = normed.astype(out_refs[0].dtype)


def _combine_norm(x2d, ys, pos, wts, g, *, emit_sum, norm_dtype):
    n_blk = N_TOK // TD
    tok_spec = pl.BlockSpec((TD, D_MODEL), lambda i: (i, 0))
    out_specs = [tok_spec]
    out_shape = [jax.ShapeDtypeStruct((N_TOK, D_MODEL), norm_dtype)]
    if emit_sum:
        out_specs = [tok_spec, tok_spec]
        out_shape = [jax.ShapeDtypeStruct((N_TOK, D_MODEL), F32)] + out_shape
    idx_blk = lambda f: pl.BlockSpec((1, 1, TD), f, memory_space=pltpu.SMEM)
    first = lambda i: (0, 0, 0)
    nxt = lambda i: (jnp.minimum(i + 1, n_blk - 1), 0, 0)
    pos3 = [pos[k].reshape(n_blk, 1, TD) for k in range(2)]
    return pl.pallas_call(
        functools.partial(_combine_norm_kernel, emit_sum=emit_sum),
        grid=(n_blk,),
        in_specs=[
            idx_blk(first), idx_blk(first), idx_blk(nxt), idx_blk(nxt),
            tok_spec,
            pl.BlockSpec((2, TD), lambda i: (0, i)),
            pl.BlockSpec(memory_space=pl.ANY),
            pl.BlockSpec((1, D_MODEL), lambda i: (0, 0)),
        ],
        out_specs=tuple(out_specs),
        out_shape=tuple(out_shape),
        scratch_shapes=[pltpu.VMEM((2, 2, TD, D_MODEL), F32),
                        pltpu.SemaphoreType.DMA((2,))],
        compiler_params=pltpu.CompilerParams(
            dimension_semantics=("arbitrary",), vmem_limit_bytes=VMEM_LIMIT),
        name="combine_norm",
    )(pos3[0], pos3[1], pos3[0], pos3[1], x2d, wts, ys, g)


def _mm_kernel(a_ref, b_ref, o_ref, *, first_block_scale):
    acc = jnp.dot(a_ref[...], b_ref[...], preferred_element_type=F32)
    if first_block_scale is not None:
        acc = acc * jnp.where(pl.program_id(0) == 0, first_block_scale, 1.0)
    o_ref[...] = acc.astype(o_ref.dtype)


def _matmul(a, b, out_dtype, tn, first_block_scale=None):
    m, k = a.shape
    n = b.shape[1]
    return pl.pallas_call(
        functools.partial(_mm_kernel, first_block_scale=first_block_scale),
        grid=(n // tn, m // MM_TM),
        in_specs=[pl.BlockSpec((MM_TM, k), lambda j, i: (i, 0)),
                  pl.BlockSpec((k, tn), lambda j, i: (0, j))],
        out_specs=pl.BlockSpec((MM_TM, tn), lambda j, i: (i, j)),
        out_shape=jax.ShapeDtypeStruct((m, n), out_dtype),
        compiler_params=pltpu.CompilerParams(
            dimension_semantics=("arbitrary", "arbitrary"), vmem_limit_bytes=VMEM_LIMIT),
        name="dense_matmul",
    )(a, b)


def _attn_kernel(lam_ref, q_ref, k_ref, v_ref, subg_ref, o_ref, *scratch, lambda_init):
    m_ref, l_ref, acc_ref, s_ref, p_ref, a_ref = (scratch[2 * n:2 * n + 2] for n in range(6))
    lane_rep = lambda a, width: jnp.concatenate([a] * (width // LANES), axis=1)
    lv = lam_ref[...]
    lam = (jnp.exp(jnp.sum(lv[0:1] * lv[1:2], axis=-1, keepdims=True))
           - jnp.exp(jnp.sum(lv[2:3] * lv[3:4], axis=-1, keepdims=True)) + lambda_init)
    width = 2 * HEAD_DIM
    sub_g = subg_ref[...] * (1.0 - lambda_init)

    def query_tile(qi, carry):
        q0 = pl.multiple_of(qi * TQ, TQ)
        q = q_ref[pl.ds(q0, TQ), :]
        qs = (q[:, :HEAD_DIM], q[:, HEAD_DIM:])

        def tile_update(j, diagonal, first):
            k0 = pl.multiple_of(j * TK, TK)
            kt = k_ref[pl.ds(k0, TK), :]
            vt = v_ref[pl.ds(k0, TK), :]
            for c in range(2):
                kc = kt[:, c * HEAD_DIM:(c + 1) * HEAD_DIM]
                s_ref[c][...] = lax.dot_general(qs[c], kc, (((1,), (1,)), ((), ())),
                                                preferred_element_type=F32)
            for c in range(2):
                for rb in range(TQ // CHUNK):
                    rows = pl.ds(rb * CHUNK, CHUNK)
                    ncol = TK
                    if diagonal:
                        seen = (rb + 1) * CHUNK
                        ncol = min(TK, -(-seen // LANES) * LANES)
                    s = s_ref[c][rows, :ncol]
                    if diagonal and seen < ncol:
                        col = lax.broadcasted_iota(I32, (CHUNK, ncol), 1)
                        s = jnp.where(col < seen, s, NEG_INF)
                    s_max = jnp.max(s, axis=-1, keepdims=True)
                    if first:
                        m_new = jnp.zeros((CHUNK, LANES), F32) + s_max
                    else:
                        m_old = m_ref[c][rows, :]
                        m_new = jnp.maximum(m_old, s_max)
                        alpha = jnp.exp2(m_old - m_new)
                        a_ref[c][rows, :] = alpha
                    p = jnp.exp2(s - lane_rep(m_new, ncol))
                    p_sum = jnp.sum(p, axis=-1, keepdims=True)
                    if first:
                        l_ref[c][rows, :] = jnp.zeros((CHUNK, LANES), F32) + p_sum
                    else:
                        l_ref[c][rows, :] = alpha * l_ref[c][rows, :] + p_sum
                    m_ref[c][rows, :] = m_new
                    p_ref[c][rows, :ncol] = p.astype(BF16)
                    if ncol < TK:
                        p_ref[c][rows, ncol:] = jnp.zeros((CHUNK, TK - ncol), BF16)
                pv = jnp.dot(p_ref[c][...], vt, preferred_element_type=F32)
                if first:
                    acc_ref[c][...] = pv
                else:
                    acc_ref[c][...] = lane_rep(a_ref[c][...], width) * acc_ref[c][...] + pv

        def full_body(j, c):
            tile_update(j, False, False)
            return c

        @pl.when(qi == 0)
        def _():
            tile_update(0, True, True)

        @pl.when(qi > 0)
        def _():
            tile_update(0, False, True)
            lax.fori_loop(1, qi, full_body, 0)
            tile_update(qi, True, False)

        for rb in range(TQ // CHUNK):
            rows = pl.ds(rb * CHUNK, CHUNK)
            o = (acc_ref[0][rows, :] * lane_rep(1.0 / l_ref[0][rows, :], width)
                 - acc_ref[1][rows, :] * lane_rep(lam * (1.0 / l_ref[1][rows, :]), width))
            o_ref[pl.ds(pl.multiple_of(q0 + rb * CHUNK, CHUNK), CHUNK), :] = _rms(
                o, sub_g, SUBLN_EPS).astype(o_ref.dtype)
        return carry

    lax.fori_loop(0, SEQ // TQ, query_tile, 0)


def _diff_attention(qkv, lam_rows, sub_g, lambda_init):
    assert TQ == TK
    head_w = 2 * HEAD_DIM
    return pl.pallas_call(
        functools.partial(_attn_kernel, lambda_init=lambda_init),
        grid=(BATCH, N_HEADS),
        in_specs=[
            pl.BlockSpec((SUBLANES, HEAD_DIM), lambda b, h: (0, 0)),
            pl.BlockSpec((SEQ, head_w), lambda b, h: (b, h)),
            pl.BlockSpec((SEQ, head_w), lambda b, h: (b, N_HEADS + h)),
            pl.BlockSpec((SEQ, head_w), lambda b, h: (b, 2 * N_HEADS + h)),
            pl.BlockSpec((1, head_w), lambda b, h: (0, 0)),
        ],
        out_specs=pl.BlockSpec((SEQ, head_w), lambda b, h: (b, h)),
        out_shape=jax.ShapeDtypeStruct((N_TOK, D_MODEL), BF16),
        scratch_shapes=(
            [pltpu.VMEM((TQ, LANES), F32)] * 2
            + [pltpu.VMEM((TQ, LANES), F32)] * 2
            + [pltpu.VMEM((TQ, head_w), F32)] * 2
            + [pltpu.VMEM((TQ, TK), F32)] * 2
            + [pltpu.VMEM((TQ, TK), BF16)] * 2
            + [pltpu.VMEM((TQ, LANES), F32)] * 2),
        compiler_params=pltpu.CompilerParams(
            dimension_semantics=("arbitrary", "arbitrary"),
            vmem_limit_bytes=VMEM_LIMIT),
        name="diff_attention",
    )(lam_rows, qkv, qkv, qkv, sub_g)


def _router_params(wr_g, br_g, wr_e, br_e):
    w = jnp.zeros((D_MODEL, ROUTER_COLS), F32)
    w = w.at[:, 0:N_GROUPS].set(wr_g).at[:, E_COL0:E_COL0 + N_EXPERTS].set(wr_e)
    b = jnp.full((1, ROUTER_COLS), NEG_INF, F32)
    b = b.at[0, 0:N_GROUPS].set(br_g).at[0, E_COL0:E_COL0 + N_EXPERTS].set(br_e)
    w_hi = w.astype(BF16)
    w_lo = (w - w_hi.astype(F32)).astype(BF16)
    return jnp.concatenate([w_hi, w_lo], axis=1), b


def kernel(x, norm_mix_g, norm_ffn_g, pool_w, pool_scale, attn_wq, attn_wk, attn_wv, attn_wo,
           lam_q1, lam_k1, lam_q2, lam_k2, attn_sub_g, router_g_w, router_g_b, router_e_w,
           router_e_b, exp_w_gate, exp_w_up, exp_w_down, final_g):
    x2d = x.reshape(N_TOK, D_MODEL)
    row = lambda v: v.reshape(1, -1)

    wr0, br0 = _router_params(router_g_w[0], router_g_b[0], router_e_w[0], router_e_b[0])
    x1, h, ids, wts = _pool_layer(x2d, row(norm_mix_g[0]), pool_w[0].astype(BF16),
                                  row(pool_scale[0]), row(norm_ffn_g[0]), wr0, br0)
    ys, pos = _moe_layer(0, h, ids, exp_w_gate, exp_w_up, exp_w_down)
    x2, hn = _combine_norm(x1, ys, pos, wts, row(norm_mix_g[1]), emit_sum=True, norm_dtype=BF16)

    lambda_init = 0.8 - 0.6 * math.exp(-0.3 * 1)
    w_qkv = jnp.concatenate([attn_wq[0], attn_wk[0], attn_wv[0]], axis=1).astype(BF16)
    q_scale = HEAD_DIM ** -0.5 * math.log2(math.e)
    qkv = _matmul(hn, w_qkv, BF16, D_MODEL, first_block_scale=q_scale)
    lam_rows = jnp.zeros((SUBLANES, HEAD_DIM), F32)
    lam_rows = lam_rows.at[0].set(lam_q1[0]).at[1].set(lam_k1[0])
    lam_rows = lam_rows.at[2].set(lam_q2[0]).at[3].set(lam_k2[0])
    att = _diff_attention(qkv, lam_rows, row(attn_sub_g[0]), lambda_init)
    wr1, br1 = _router_params(router_g_w[1], router_g_b[1], router_e_w[1], router_e_b[1])
    x3, h, ids, wts = _proj_router(x2, att, attn_wo[0].astype(BF16), row(norm_ffn_g[1]),
                                   wr1, br1)
    ys, pos = _moe_layer(1, h, ids, exp_w_gate, exp_w_up, exp_w_down)
    (out,) = _combine_norm(x3, ys, pos, wts, row(final_g), emit_sum=False, norm_dtype=F32)
    return out.reshape(BATCH, SEQ, D_MODEL)
```

```python
import functools
import math

import jax
import jax.numpy as jnp
from jax import lax
from jax.experimental import pallas as pl
from jax.experimental.pallas import tpu as pltpu

F32 = jnp.float32
BF16 = jnp.bfloat16
I32 = jnp.int32

D_MODEL = 2048
BATCH = 8
SEQ = 2048
N_TOK = BATCH * SEQ
CHUNK = 64
POOL_WINDOWS = (2, 4, 8, 16)
POOL_CG = D_MODEL // len(POOL_WINDOWS)
POOL_HALO = 16
N_HEADS = 8
HEAD_DIM = 128
N_GROUPS = 4
EXPERTS_PER_GROUP = 8
N_EXPERTS = 32
EXPERT_FF = 512
NORM_EPS = 1e-6
SUBLN_EPS = 1e-5
NEG_INF = -1e30

SUBLANES = 8
LANES = 128
ROUTER_COLS = LANES
E_COL0 = 8

TS = 256
TD = 512
TP = 512
TM = 256
N_SORT = 2 * N_TOK + N_EXPERTS * TM
N_TILES = N_SORT // TM
TQ = 512
TK = 512
MM_TM = 1024
RANK_BLOCK = 1024
VMEM_LIMIT = 56 * 1024 * 1024


def _rms(x, g, eps):
    ms = jnp.mean(x * x, axis=-1, keepdims=True)
    return x * lax.rsqrt(ms + eps) * g


def _router_tail(x_new, gffn_ref, wr_ref, br_ref, h_ref, ids_ref, wts_ref):
    h = _rms(x_new, gffn_ref[...], NORM_EPS)
    h_ref[...] = h
    h_hi = h.astype(BF16)
    h_lo = (h - h_hi.astype(F32)).astype(BF16)
    hi_terms = jnp.dot(h_hi, wr_ref[...], preferred_element_type=F32)
    logits = (hi_terms[:, :ROUTER_COLS] + hi_terms[:, ROUTER_COLS:]
              + jnp.dot(h_lo, wr_ref[:, :ROUTER_COLS], preferred_element_type=F32))
    logits = logits + br_ref[...]
    lt = logits.T
    gl = lt[0:SUBLANES, :]
    io8 = lax.broadcasted_iota(I32, gl.shape, 0)
    gmax = jnp.max(gl, axis=0, keepdims=True)
    gtop = jnp.min(jnp.where(gl == gmax, io8, SUBLANES), axis=0, keepdims=True)
    p_g = 1.0 / jnp.sum(jnp.exp(gl - gmax), axis=0, keepdims=True)
    esel = lt[E_COL0:E_COL0 + EXPERTS_PER_GROUP, :]
    for gi in range(1, N_GROUPS):
        lo = E_COL0 + gi * EXPERTS_PER_GROUP
        esel = jnp.where(gtop == gi, lt[lo:lo + EXPERTS_PER_GROUP, :], esel)
    m1 = jnp.max(esel, axis=0, keepdims=True)
    i1 = jnp.min(jnp.where(esel == m1, io8, SUBLANES), axis=0, keepdims=True)
    rest = jnp.where(io8 == i1, -jnp.inf, esel)
    m2 = jnp.max(rest, axis=0, keepdims=True)
    i2 = jnp.min(jnp.where(rest == m2, io8, SUBLANES), axis=0, keepdims=True)
    r = jnp.exp(m2 - m1)
    v1 = 1.0 / (1.0 + r)
    v2 = r / (1.0 + r)
    base = gtop * EXPERTS_PER_GROUP
    ids_ref[0:1, :] = base + i1
    ids_ref[1:2, :] = base + i2
    wts_ref[0:1, :] = p_g * v1
    wts_ref[1:2, :] = p_g * v2


def _router_out_shapes():
    return (jax.ShapeDtypeStruct((N_TOK, D_MODEL), F32),
            jax.ShapeDtypeStruct((2, N_TOK), I32),
            jax.ShapeDtypeStruct((2, N_TOK), F32))


def _router_out_specs(tok_map, tile=TS):
    return (pl.BlockSpec((tile, D_MODEL), lambda *a: (tok_map(*a), 0)),
            pl.BlockSpec((2, tile), lambda *a: (0, tok_map(*a))),
            pl.BlockSpec((2, tile), lambda *a: (0, tok_map(*a))))


def _pool_kernel(x_ref, halo_ref, gmix_ref, wpool_ref, pscale_ref, gffn_ref, wr_ref, br_ref,
                 x1_ref, h_ref, ids_ref, wts_ref, hs_ref):
    s = pl.program_id(1)
    xt = x_ref[...]
    g = gmix_ref[...]
    h = _rms(xt, g, NORM_EPS)
    hh = _rms(halo_ref[...], g, NORM_EPS)
    hs_ref[0:POOL_HALO, :] = jnp.where(s > 0, hh, 0.0)
    hs_ref[POOL_HALO:, :] = h
    pos = s * TS + lax.broadcasted_iota(I32, (TS, 1), 0)
    outs = []
    for gi, w in enumerate(POOL_WINDOWS):
        c0 = gi * POOL_CG
        he = hs_ref[:, c0:c0 + POOL_CG]
        acc = he
        sh = 1
        while sh < w:
            acc = acc + pltpu.roll(acc, sh, axis=0)
            sh *= 2
        inv_cnt = 1.0 / jnp.minimum(pos + 1, w).astype(F32)
        mix = acc[POOL_HALO:, :] * inv_cnt - he[POOL_HALO:, :]
        outs.append(jnp.dot(mix.astype(BF16), wpool_ref[gi], preferred_element_type=F32))
    x1 = xt + jnp.concatenate(outs, axis=-1) * pscale_ref[...]
    x1_ref[...] = x1
    _router_tail(x1, gffn_ref, wr_ref, br_ref, h_ref, ids_ref, wts_ref)


def _pool_layer(x2d, gmix, wpool_bf, pscale, gffn, wr2, br):
    n_s = SEQ // TS
    tok_map = lambda b, s: b * n_s + s
    halo_blocks = TS // POOL_HALO
    const2 = lambda b, s: (0, 0)
    return pl.pallas_call(
        _pool_kernel,
        grid=(BATCH, n_s),
        in_specs=[
            pl.BlockSpec((TS, D_MODEL), lambda b, s: (tok_map(b, s), 0)),
            pl.BlockSpec((POOL_HALO, D_MODEL),
                         lambda b, s: (jnp.maximum(tok_map(b, s) * halo_blocks - 1, 0), 0)),
            pl.BlockSpec((1, D_MODEL), const2),
            pl.BlockSpec((len(POOL_WINDOWS), POOL_CG, POOL_CG), lambda b, s: (0, 0, 0)),
            pl.BlockSpec((1, D_MODEL), const2),
            pl.BlockSpec((1, D_MODEL), const2),
            pl.BlockSpec((D_MODEL, 2 * ROUTER_COLS), const2),
            pl.BlockSpec((1, ROUTER_COLS), const2),
        ],
        out_specs=(pl.BlockSpec((TS, D_MODEL), lambda b, s: (tok_map(b, s), 0)),)
        + _router_out_specs(tok_map),
        out_shape=(jax.ShapeDtypeStruct((N_TOK, D_MODEL), F32),) + _router_out_shapes(),
        scratch_shapes=[pltpu.VMEM((TS + POOL_HALO, D_MODEL), F32)],
        compiler_params=pltpu.CompilerParams(
            dimension_semantics=("arbitrary", "arbitrary"), vmem_limit_bytes=VMEM_LIMIT),
        name="pool_mixer_router",
    )(x2d, x2d, gmix, wpool_bf, pscale, gffn, wr2, br)


def _proj_router_kernel(x_ref, a_ref, wo_ref, gffn_ref, wr_ref, br_ref,
                        xo_ref, h_ref, ids_ref, wts_ref):
    x_new = x_ref[...] + jnp.dot(a_ref[...], wo_ref[...], preferred_element_type=F32)
    xo_ref[...] = x_new
    _router_tail(x_new, gffn_ref, wr_ref, br_ref, h_ref, ids_ref, wts_ref)


def _proj_router(x2d, att, wo_bf, gffn, wr2, br):
    tok_map = lambda i: i
    const2 = lambda i: (0, 0)
    return pl.pallas_call(
        _proj_router_kernel,
        grid=(N_TOK // TP,),
        in_specs=[
            pl.BlockSpec((TP, D_MODEL), lambda i: (i, 0)),
            pl.BlockSpec((TP, D_MODEL), lambda i: (i, 0)),
            pl.BlockSpec((D_MODEL, D_MODEL), const2, pipeline_mode=pl.Buffered(1)),
            pl.BlockSpec((1, D_MODEL), const2),
            pl.BlockSpec((D_MODEL, 2 * ROUTER_COLS), const2),
            pl.BlockSpec((1, ROUTER_COLS), const2),
        ],
        out_specs=(pl.BlockSpec((TP, D_MODEL), lambda i: (i, 0)),)
        + _router_out_specs(tok_map, TP),
        out_shape=(jax.ShapeDtypeStruct((N_TOK, D_MODEL), F32),) + _router_out_shapes(),
        compiler_params=pltpu.CompilerParams(
            dimension_semantics=("arbitrary",), vmem_limit_bytes=VMEM_LIMIT),
        name="proj_router",
    )(x2d, att, wo_bf, gffn, wr2, br)


def _dispatch_kernel(pad0_ref, padn_ref, nv_ref, p0_ref, p1_ref, h_ref, xs_ref,
                     hbuf, zrow, ztile, sem, sem_z):
    i = pl.program_id(0)
    n_blk = pl.num_programs(0)
    slot = i % 2

    def zero_row(p):
        return pltpu.make_async_copy(zrow.at[pl.ds(0, 1), :], xs_ref.at[pl.ds(p, 1), :], sem_z)

    def zero_rows8(p):
        return pltpu.make_async_copy(
            zrow, xs_ref.at[pl.ds(pl.multiple_of(p, SUBLANES), SUBLANES), :], sem_z)

    def zero_tile(t):
        return pltpu.make_async_copy(
            ztile, xs_ref.at[pl.ds(pl.multiple_of(t * TM, TM), TM), :], sem_z)

    def for_each_unowned(row_fn, rows8_fn, tile_fn):
        def per_expert(e, c):
            start = pad0_ref[e]
            n_single = (-start) % SUBLANES
            n_rows8 = (padn_ref[e] - n_single) // SUBLANES

            def per_row(j, c2):
                row_fn(start + j)
                return c2

            def per_rows8(j, c2):
                rows8_fn(start + n_single + j * SUBLANES)
                return c2
            lax.fori_loop(0, n_single, per_row, c)
            return lax.fori_loop(0, n_rows8, per_rows8, c)
        lax.fori_loop(0, N_EXPERTS, per_expert, 0)

        def per_tile(t, c):
            tile_fn(t)
            return c
        lax.fori_loop(nv_ref[0], N_TILES, per_tile, 0)

    @pl.when(i == 0)
    def _():
        zrow[...] = jnp.zeros(zrow.shape, zrow.dtype)
        ztile[...] = jnp.zeros(ztile.shape, ztile.dtype)
        for_each_unowned(lambda p: zero_row(p).start(), lambda p: zero_rows8(p).start(),
                         lambda t: zero_tile(t).start())

    def send_wait(s):
        for _ in range(2):
            pltpu.make_async_copy(hbuf.at[s], xs_ref.at[pl.ds(0, TD), :], sem.at[s]).wait()

    @pl.when(i >= 2)
    def _():
        send_wait(slot)

    hbuf[slot] = h_ref[...]
    for k, p_ref in enumerate((p0_ref, p1_ref)):
        for r in range(TD):
            pltpu.make_async_copy(hbuf.at[slot, pl.ds(r, 1), :],
                                  xs_ref.at[pl.ds(p_ref[0, 0, r], 1), :],
                                  sem.at[slot]).start(priority=(r + k) % 2)

    @pl.when(i == n_blk - 1)
    def _():
        send_wait(1 - slot)
        send_wait(slot)
        for_each_unowned(lambda p: zero_row(p).wait(), lambda p: zero_rows8(p).wait(),
                         lambda t: zero_tile(t).wait())


def _dispatch(h, pos, pad_start, pad_len, n_valid):
    n_blk = N_TOK // TD
    idx_blk = lambda: pl.BlockSpec((1, 1, TD), lambda i, *_: (i, 0, 0), memory_space=pltpu.SMEM)
    pos3 = [pos[k].reshape(n_blk, 1, TD) for k in range(2)]
    grid_spec = pltpu.PrefetchScalarGridSpec(
        num_scalar_prefetch=3,
        grid=(n_blk,),
        in_specs=[idx_blk(), idx_blk(),
                  pl.BlockSpec((TD, D_MODEL), lambda i, *_: (i, 0))],
        out_specs=pl.BlockSpec(memory_space=pl.ANY),
        scratch_shapes=[pltpu.VMEM((2, TD, D_MODEL), F32),
                        pltpu.VMEM((SUBLANES, D_MODEL), F32),
                        pltpu.VMEM((TM, D_MODEL), F32),
                        pltpu.SemaphoreType.DMA((2,)),
                        pltpu.SemaphoreType.DMA(())],
    )
    return pl.pallas_call(
        _dispatch_kernel,
        grid_spec=grid_spec,
        out_shape=jax.ShapeDtypeStruct((N_SORT, D_MODEL), F32),
        compiler_params=pltpu.CompilerParams(
            dimension_semantics=("arbitrary",), vmem_limit_bytes=VMEM_LIMIT),
        name="dispatch_rows",
    )(pad_start, pad_len, n_valid, pos3[0], pos3[1], h)


def _moe_kernel(te_ref, nv_ref, first_ref, ord_ref, nxt_ref, x_ref, wg_hbm, wu_hbm, wd_hbm, o_ref,
                wg_f32, wu_f32, wd_f32, wg_bf, wu_bf, wd_bf, sem_w, *, layer):
    i = pl.program_id(0)
    nv = nv_ref[0]

    def weight_copies(e, s):
        return [pltpu.make_async_copy(hbm.at[layer, e], buf.at[s], sem_w.at[s])
                for hbm, buf in ((wg_hbm, wg_f32), (wu_hbm, wu_f32), (wd_hbm, wd_f32))]

    @pl.when(i == 0)
    def _():
        for cp in weight_copies(te_ref[0], 0):
            cp.start()

    @pl.when((i < nv) & (first_ref[i] > 0))
    def _():
        e = te_ref[i]
        s = ord_ref[i] % 2
        for cp in weight_copies(e, s):
            cp.wait()
        wg_bf[...] = wg_f32[s].astype(BF16)
        wu_bf[...] = wu_f32[s].astype(BF16)
        wd_bf[...] = wd_f32[s].astype(BF16)
        e_next = nxt_ref[i]

        @pl.when(e_next != e)
        def _():
            for cp in weight_copies(e_next, 1 - s):
                cp.start()

    @pl.when(i < nv)
    def _():
        x = x_ref[...].astype(BF16)
        g = jnp.dot(x, wg_bf[...], preferred_element_type=F32)
        u = jnp.dot(x, wu_bf[...], preferred_element_type=F32)
        hid = (g / (1.0 + jnp.exp(-g)) * u).astype(BF16)
        o_ref[...] = jnp.dot(hid, wd_bf[...], preferred_element_type=F32)

    @pl.when(i >= nv)
    def _():
        o_ref[...] = jnp.zeros(o_ref.shape, o_ref.dtype)


def _moe_grouped(layer, tiles, xs, w_gate, w_up, w_down):
    tile_expert, n_valid, first, ordinal, next_expert = tiles
    grid_spec = pltpu.PrefetchScalarGridSpec(
        num_scalar_prefetch=5,
        grid=(N_TILES,),
        in_specs=[
            pl.BlockSpec((TM, D_MODEL), lambda i, te, nv, *_: (jnp.minimum(i, nv[0] - 1), 0)),
            pl.BlockSpec(memory_space=pl.ANY),
            pl.BlockSpec(memory_space=pl.ANY),
            pl.BlockSpec(memory_space=pl.ANY),
        ],
        out_specs=pl.BlockSpec((TM, D_MODEL), lambda i, *_: (i, 0)),
        scratch_shapes=[pltpu.VMEM((2, D_MODEL, EXPERT_FF), F32),
                        pltpu.VMEM((2, D_MODEL, EXPERT_FF), F32),
                        pltpu.VMEM((2, EXPERT_FF, D_MODEL), F32),
                        pltpu.VMEM((D_MODEL, EXPERT_FF), BF16),
                        pltpu.VMEM((D_MODEL, EXPERT_FF), BF16),
                        pltpu.VMEM((EXPERT_FF, D_MODEL), BF16),
                        pltpu.SemaphoreType.DMA((2,))],
    )
    return pl.pallas_call(
        functools.partial(_moe_kernel, layer=layer),
        grid_spec=grid_spec,
        out_shape=jax.ShapeDtypeStruct((N_SORT, D_MODEL), F32),
        compiler_params=pltpu.CompilerParams(
            dimension_semantics=("arbitrary",), vmem_limit_bytes=VMEM_LIMIT),
        name="moe_grouped",
    )(tile_expert, n_valid, first, ordinal, next_expert, xs, w_gate, w_up, w_down)


def _rank_kernel(e_ref, tri_ref, rank_ref, cnt_ref, carry):
    b = pl.program_id(0)

    @pl.when(b == 0)
    def _():
        carry[...] = jnp.zeros(carry.shape, carry.dtype)

    e = e_ref[...]
    is_e = lax.broadcasted_iota(I32, (N_EXPERTS, RANK_BLOCK), 0) == e
    onehot = jnp.where(is_e, 1.0, 0.0).astype(BF16)
    prefix = jnp.dot(onehot, tri_ref[...], preferred_element_type=F32)
    before = carry[...]
    seen = prefix + before[:, 0:1]
    rank = jnp.sum(jnp.where(is_e, seen, 0.0), axis=0, keepdims=True) - 1.0
    rank_ref[...] = rank.astype(I32)
    total = before + prefix[:, RANK_BLOCK - 1:RANK_BLOCK]
    carry[...] = total
    cnt_ref[...] = total.astype(I32)


def _rank_in_expert(e_flat):
    n = e_flat.shape[1]
    tri = jnp.triu(jnp.ones((RANK_BLOCK, RANK_BLOCK), BF16))
    rank, cnt = pl.pallas_call(
        _rank_kernel,
        grid=(n // RANK_BLOCK,),
        in_specs=[pl.BlockSpec((1, RANK_BLOCK), lambda b: (0, b)),
                  pl.BlockSpec((RANK_BLOCK, RANK_BLOCK), lambda b: (0, 0))],
        out_specs=(pl.BlockSpec((1, RANK_BLOCK), lambda b: (0, b)),
                   pl.BlockSpec((N_EXPERTS, LANES), lambda b: (0, 0))),
        out_shape=(jax.ShapeDtypeStruct((1, n), I32),
                   jax.ShapeDtypeStruct((N_EXPERTS, LANES), I32)),
        scratch_shapes=[pltpu.VMEM((N_EXPERTS, LANES), F32)],
        compiler_params=pltpu.CompilerParams(
            dimension_semantics=("arbitrary",), vmem_limit_bytes=VMEM_LIMIT),
        name="rank_in_expert",
    )(e_flat, tri)
    return rank, cnt[:, 0]


def _route_plan(ids):
    e = ids.reshape(-1)
    rank, counts = _rank_in_expert(ids.reshape(1, -1))
    padded = (counts + TM - 1) // TM * TM
    ends = jnp.cumsum(padded)
    offs = ends - padded
    is_e = e[:, None] == jnp.arange(N_EXPERTS, dtype=I32)[None, :]
    pos = jnp.sum(jnp.where(is_e, offs[None, :], 0), axis=1) + rank.reshape(-1)
    pos = pos.reshape(2, N_TOK)
    n_valid = (ends[-1] // TM).astype(I32)
    tile = jnp.arange(N_TILES, dtype=I32)
    tile_start = jnp.minimum(tile, n_valid - 1) * TM
    n_done = jnp.sum((ends[None, :] <= tile_start[:, None]).astype(I32), axis=1)
    tile_expert = jnp.minimum(n_done, N_EXPERTS - 1)
    prev_expert = jnp.concatenate([jnp.full((1,), -1, I32), tile_expert[:-1]])
    first = ((tile_expert != prev_expert) & (tile < n_valid)).astype(I32)
    ordinal = jnp.cumsum(first) - 1
    expert = jnp.arange(N_EXPERTS, dtype=I32)
    used_id = jnp.where(counts > 0, expert, N_EXPERTS)
    later = expert[None, :] > expert[:, None]
    next_used = jnp.min(jnp.where(later, used_id[None, :], N_EXPERTS), axis=1)
    next_used = jnp.where(next_used < N_EXPERTS, next_used, expert)
    next_expert = jnp.sum(jnp.where(tile_expert[:, None] == expert[None, :],
                                    next_used[None, :], 0), axis=1)
    tiles = (tile_expert, n_valid.reshape(1), first, ordinal, next_expert)
    return pos, offs + counts, padded - counts, tiles


def _moe_layer(layer, h, ids, w_gate, w_up, w_down):
    pos, pad_start, pad_len, tiles = _route_plan(ids)
    xs = _dispatch(h, pos, pad_start, pad_len, tiles[1])
    ys = _moe_grouped(layer, tiles, xs, w_gate, w_up, w_down)
    return ys, pos


def _combine_norm_kernel(p0f_ref, p1f_ref, p0n_ref, p1n_ref, x_ref, w_ref, ys_ref, g_ref, *rest,
                         emit_sum):
    out_refs, (ybuf, sem) = rest[:-2], rest[-2:]
    i = pl.program_id(0)
    n_blk = pl.num_programs(0)
    slot = i % 2
    other = 1 - slot

    def fetch(idx_refs, s):
        for k in range(2):
            for r in range(TD):
                pltpu.make_async_copy(ys_ref.at[pl.ds(idx_refs[k][0, 0, r], 1), :],
                                      ybuf.at[s, k, pl.ds(r, 1), :],
                                      sem.at[s]).start(priority=r % 2)

    @pl.when(i == 0)
    def _():
        fetch((p0f_ref, p1f_ref), 0)

    for k in range(2):
        pltpu.make_async_copy(ys_ref.at[pl.ds(0, TD), :], ybuf.at[slot, k], sem.at[slot]).wait()

    @pl.when(i + 1 < n_blk)
    def _():
        fetch((p0n_ref, p1n_ref), other)

    w = w_ref[...]
    eye = (lax.broadcasted_iota(I32, (TD, TD), 0) == lax.broadcasted_iota(I32, (TD, TD), 1))
    x_new = x_ref[...]
    for k in range(2):
        w_col = jnp.sum(jnp.where(eye, w[k:k + 1, :], 0.0), axis=1, keepdims=True)
        x_new = x_new + w_col * ybuf[slot, k]
    normed = _rms(x_new, g_ref[...], NORM_EPS)
    if emit_sum:
        out_refs[0][...] = x_new
        out_refs[1][...] = normed.astype(out_refs[1].dtype)
    else:
        out_refs[0][...] = normed.astype(out_refs[0].dtype)


def _combine_norm(x2d, ys, pos, wts, g, *, emit_sum, norm_dtype):
    n_blk = N_TOK // TD
    tok_spec = pl.BlockSpec((TD, D_MODEL), lambda i: (i, 0))
    out_specs = [tok_spec]
    out_shape = [jax.ShapeDtypeStruct((N_TOK, D_MODEL), norm_dtype)]
    if emit_sum:
        out_specs = [tok_spec, tok_spec]
        out_shape = [jax.ShapeDtypeStruct((N_TOK, D_MODEL), F32)] + out_shape
    idx_blk = lambda f: pl.BlockSpec((1, 1, TD), f, memory_space=pltpu.SMEM)
    first = lambda i: (0, 0, 0)
    nxt = lambda i: (jnp.minimum(i + 1, n_blk - 1), 0, 0)
    pos3 = [pos[k].reshape(n_blk, 1, TD) for k in range(2)]
    return pl.pallas_call(
        functools.partial(_combine_norm_kernel, emit_sum=emit_sum),
        grid=(n_blk,),
        in_specs=[
            idx_blk(first), idx_blk(first), idx_blk(nxt), idx_blk(nxt),
            tok_spec,
            pl.BlockSpec((2, TD), lambda i: (0, i)),
            pl.BlockSpec(memory_space=pl.ANY),
            pl.BlockSpec((1, D_MODEL), lambda i: (0, 0)),
        ],
        out_specs=tuple(out_specs),
        out_shape=tuple(out_shape),
        scratch_shapes=[pltpu.VMEM((2, 2, TD, D_MODEL), F32),
                        pltpu.SemaphoreType.DMA((2,))],
        compiler_params=pltpu.CompilerParams(
            dimension_semantics=("arbitrary",), vmem_limit_bytes=VMEM_LIMIT),
        name="combine_norm",
    )(pos3[0], pos3[1], pos3[0], pos3[1], x2d, wts, ys, g)


def _mm_kernel(a_ref, b_ref, o_ref, *, first_block_scale):
    acc = jnp.dot(a_ref[...], b_ref[...], preferred_element_type=F32)
    if first_block_scale is not None:
        acc = acc * jnp.where(pl.program_id(0) == 0, first_block_scale, 1.0)
    o_ref[...] = acc.astype(o_ref.dtype)


def _matmul(a, b, out_dtype, tn, first_block_scale=None):
    m, k = a.shape
    n = b.shape[1]
    return pl.pallas_call(
        functools.partial(_mm_kernel, first_block_scale=first_block_scale),
        grid=(n // tn, m // MM_TM),
        in_specs=[pl.BlockSpec((MM_TM, k), lambda j, i: (i, 0)),
                  pl.BlockSpec((k, tn), lambda j, i: (0, j))],
        out_specs=pl.BlockSpec((MM_TM, tn), lambda j, i: (i, j)),
        out_shape=jax.ShapeDtypeStruct((m, n), out_dtype),
        compiler_params=pltpu.CompilerParams(
            dimension_semantics=("arbitrary", "arbitrary"), vmem_limit_bytes=VMEM_LIMIT),
        name="dense_matmul",
    )(a, b)


def _attn_kernel(lam_ref, q_ref, k_ref, v_ref, subg_ref, o_ref, *scratch, lambda_init):
    m_ref, l_ref, acc_ref, s_ref, p_ref, a_ref = (scratch[2 * n:2 * n + 2] for n in range(6))
    lane_rep = lambda a, width: jnp.concatenate([a] * (width // LANES), axis=1)
    lv = lam_ref[...]
    lam = (jnp.exp(jnp.sum(lv[0:1] * lv[1:2], axis=-1, keepdims=True))
           - jnp.exp(jnp.sum(lv[2:3] * lv[3:4], axis=-1, keepdims=True)) + lambda_init)
    width = 2 * HEAD_DIM
    sub_g = subg_ref[...] * (1.0 - lambda_init)

    def query_tile(qi, carry):
        q0 = pl.multiple_of(qi * TQ, TQ)
        q = q_ref[pl.ds(q0, TQ), :]
        qs = (q[:, :HEAD_DIM], q[:, HEAD_DIM:])

        def tile_update(j, diagonal, first):
            k0 = pl.multiple_of(j * TK, TK)
            kt = k_ref[pl.ds(k0, TK), :]
            vt = v_ref[pl.ds(k0, TK), :]
            for c in range(2):
                kc = kt[:, c * HEAD_DIM:(c + 1) * HEAD_DIM]
                s_ref[c][...] = lax.dot_general(qs[c], kc, (((1,), (1,)), ((), ())),
                                                preferred_element_type=F32)
            for c in range(2):
                for rb in range(TQ // CHUNK):
                    rows = pl.ds(rb * CHUNK, CHUNK)
                    ncol = TK
                    if diagonal:
                        seen = (rb + 1) * CHUNK
                        ncol = min(TK, -(-seen // LANES) * LANES)
                    s = s_ref[c][rows, :ncol]
                    if diagonal and seen < ncol:
                        col = lax.broadcasted_iota(I32, (CHUNK, ncol), 1)
                        s = jnp.where(col < seen, s, NEG_INF)
                    s_max = jnp.max(s, axis=-1, keepdims=True)
                    if first:
                        m_new = jnp.zeros((CHUNK, LANES), F32) + s_max
                    else:
                        m_old = m_ref[c][rows, :]
                        m_new = jnp.maximum(m_old, s_max)
                        alpha = jnp.exp2(m_old - m_new)
                        a_ref[c][rows, :] = alpha
                    p = jnp.exp2(s - lane_rep(m_new, ncol))
                    p_sum = jnp.sum(p, axis=-1, keepdims=True)
                    if first:
                        l_ref[c][rows, :] = jnp.zeros((CHUNK, LANES), F32) + p_sum
                    else:
                        l_ref[c][rows, :] = alpha * l_ref[c][rows, :] + p_sum
                    m_ref[c][rows, :] = m_new
                    p_ref[c][rows, :ncol] = p.astype(BF16)
                    if ncol < TK:
                        p_ref[c][rows, ncol:] = jnp.zeros((CHUNK, TK - ncol), BF16)
                pv = jnp.dot(p_ref[c][...], vt, preferred_element_type=F32)
                if first:
                    acc_ref[c][...] = pv
                else:
                    acc_ref[c][...] = lane_rep(a_ref[c][...], width) * acc_ref[c][...] + pv

        def full_body(j, c):
            tile_update(j, False, False)
            return c

        @pl.when(qi == 0)
        def _():
            tile_update(0, True, True)

        @pl.when(qi > 0)
        def _():
            tile_update(0, False, True)
            lax.fori_loop(1, qi, full_body, 0)
            tile_update(qi, True, False)

        for rb in range(TQ // CHUNK):
            rows = pl.ds(rb * CHUNK, CHUNK)
            o = (acc_ref[0][rows, :] * lane_rep(1.0 / l_ref[0][rows, :], width)
                 - acc_ref[1][rows, :] * lane_rep(lam * (1.0 / l_ref[1][rows, :]), width))
            o_ref[pl.ds(pl.multiple_of(q0 + rb * CHUNK, CHUNK), CHUNK), :] = _rms(
                o, sub_g, SUBLN_EPS).astype(o_ref.dtype)
        return carry

    lax.fori_loop(0, SEQ // TQ, query_tile, 0)


def _diff_attention(qkv, lam_rows, sub_g, lambda_init):
    assert TQ == TK
    head_w = 2 * HEAD_DIM
    return pl.pallas_call(
        functools.partial(_attn_kernel, lambda_init=lambda_init),
        grid=(BATCH, N_HEADS),
        in_specs=[
            pl.BlockSpec((SUBLANES, HEAD_DIM), lambda b, h: (0, 0)),
            pl.BlockSpec((SEQ, head_w), lambda b, h: (b, h)),
            pl.BlockSpec((SEQ, head_w), lambda b, h: (b, N_HEADS + h)),
            pl.BlockSpec((SEQ, head_w), lambda b, h: (b, 2 * N_HEADS + h)),
            pl.BlockSpec((1, head_w), lambda b, h: (0, 0)),
        ],
        out_specs=pl.BlockSpec((SEQ, head_w), lambda b, h: (b, h)),
        out_shape=jax.ShapeDtypeStruct((N_TOK, D_MODEL), BF16),
        scratch_shapes=(
            [pltpu.VMEM((TQ, LANES), F32)] * 2
            + [pltpu.VMEM((TQ, LANES), F32)] * 2
            + [pltpu.VMEM((TQ, head_w), F32)] * 2
            + [pltpu.VMEM((TQ, TK), F32)] * 2
            + [pltpu.VMEM((TQ, TK), BF16)] * 2
            + [pltpu.VMEM((TQ, LANES), F32)] * 2),
        compiler_params=pltpu.CompilerParams(
            dimension_semantics=("arbitrary", "arbitrary"),
            vmem_limit_bytes=VMEM_LIMIT),
        name="diff_attention",
    )(lam_rows, qkv, qkv, qkv, sub_g)


def _router_params(wr_g, br_g, wr_e, br_e):
    w = jnp.zeros((D_MODEL, ROUTER_COLS), F32)
    w = w.at[:, 0:N_GROUPS].set(wr_g).at[:, E_COL0:E_COL0 + N_EXPERTS].set(wr_e)
    b = jnp.full((1, ROUTER_COLS), NEG_INF, F32)
    b = b.at[0, 0:N_GROUPS].set(br_g).at[0, E_COL0:E_COL0 + N_EXPERTS].set(br_e)
    w_hi = w.astype(BF16)
    w_lo = (w - w_hi.astype(F32)).astype(BF16)
    return jnp.concatenate([w_hi, w_lo], axis=1), b


def kernel(x, norm_mix_g, norm_ffn_g, pool_w, pool_scale, attn_wq, attn_wk, attn_wv, attn_wo,
           lam_q1, lam_k1, lam_q2, lam_k2, attn_sub_g, router_g_w, router_g_b, router_e_w,
           router_e_b, exp_w_gate, exp_w_up, exp_w_down, final_g):
    x2d = x.reshape(N_TOK, D_MODEL)
    row = lambda v: v.reshape(1, -1)

    wr0, br0 = _router_params(router_g_w[0], router_g_b[0], router_e_w[0], router_e_b[0])
    x1, h, ids, wts = _pool_layer(x2d, row(norm_mix_g[0]), pool_w[0].astype(BF16),
                                  row(pool_scale[0]), row(norm_ffn_g[0]), wr0, br0)
    ys, pos = _moe_layer(0, h, ids, exp_w_gate, exp_w_up, exp_w_down)
    x2, hn = _combine_norm(x1, ys, pos, wts, row(norm_mix_g[1]), emit_sum=True, norm_dtype=BF16)

    lambda_init = 0.8 - 0.6 * math.exp(-0.3 * 1)
    w_qkv = jnp.concatenate([attn_wq[0], attn_wk[0], attn_wv[0]], axis=1).astype(BF16)
    q_scale = HEAD_DIM ** -0.5 * math.log2(math.e)
    qkv = _matmul(hn, w_qkv, BF16, D_MODEL, first_block_scale=q_scale)
    lam_rows = jnp.zeros((SUBLANES, HEAD_DIM), F32)
    lam_rows = lam_rows.at[0].set(lam_q1[0]).at[1].set(lam_k1[0])
    lam_rows = lam_rows.at[2].set(lam_q2[0]).at[3].set(lam_k2[0])
    att = _diff_attention(qkv, lam_rows, row(attn_sub_g[0]), lambda_init)
    wr1, br1 = _router_params(router_g_w[1], router_g_b[1], router_e_w[1], router_e_b[1])
    x3, h, ids, wts = _proj_router(x2, att, attn_wo[0].astype(BF16), row(norm_ffn_g[1]),
                                   wr1, br1)
    ys, pos = _moe_layer(1, h, ids, exp_w_gate, exp_w_up, exp_w_down)
    (out,) = _combine_norm(x3, ys, pos, wts, row(final_g), emit_sum=False, norm_dtype=F32)
    return out.reshape(BATCH, SEQ, D_MODEL)
```
